```python
import math
import jax, jax.numpy as jnp
from jax import lax
import numpy as np

D_MODEL = 2048
BATCH = 16
SEQ = 256
DEPTH = 2
DEC_BATCH = 8
DEC_SEQ = 4096
PAST_LEN = 512

GRID_W = 64
N_MIXERS = 2
N_HGRN = (DEPTH + 1) // 2
N_ATTN = DEPTH // 2
HGRN_EXPAND = 128
HGRN_HEADS = D_MODEL // HGRN_EXPAND
HGRN_DK = HGRN_EXPAND
HGRN_DV = D_MODEL // HGRN_HEADS
CHUNK = 64
DIFF_HEADS = 8
DIFF_HD = D_MODEL // DIFF_HEADS // 2
D_FF = 5632
CONV_W = 3
Q_BLOCK = 128
ROPE_BASE = 10000.0
EPS = 1e-6
F32 = jnp.float32

kernel_name = 'hybrid_hgrn2_diffattn_diffusion_step'


def rmsnorm(x, g):
    x32 = x.astype(F32)
    y = x32 * lax.rsqrt(jnp.mean(x32 * x32, axis=-1, keepdims=True) + EPS)
    return (y * g.astype(F32)).astype(x.dtype)


def ada_mod(cvec, w, b):
    m = jax.nn.silu(cvec) @ w + b
    return jnp.split(m[:, None, :], 6, axis=-1)


def modulate(h, shift, scale):
    return h * (1 + scale) + shift


def axial_rope(L, hd):
    rows = L // GRID_W
    row = jnp.repeat(jnp.arange(rows), GRID_W).astype(F32)
    col = jnp.tile(jnp.arange(GRID_W), rows).astype(F32)
    nf = hd // 4
    inv = ROPE_BASE ** (-jnp.arange(nf, dtype=F32) / nf)
    ang = jnp.stack([row[:, None] * inv, col[:, None] * inv], axis=1)
    return jnp.cos(ang), jnp.sin(ang)


def apply_axial_rope(x, cos, sin):
    B, L, H, hd = x.shape
    nf = hd // 4
    xr = x.astype(F32).reshape(B, L, H, 2, 2, nf)
    x1, x2 = xr[..., 0, :], xr[..., 1, :]
    c = cos[None, :, None]
    s = sin[None, :, None]
    out = jnp.stack([x1 * c - x2 * s, x2 * c + x1 * s], axis=-2)
    return out.reshape(B, L, H, hd).astype(x.dtype)


def chunk_scan(q, k, v, log_f, s0):
    B, L, H, _ = q.shape
    DV = v.shape[-1]
    n = L // CHUNK
    to_chunks = lambda t: t.reshape(B, n, CHUNK, H, t.shape[-1]).transpose(1, 0, 3, 2, 4)
    causal = jnp.tril(jnp.ones((CHUNK, CHUNK), bool))

    def step(S, inp):
        qc, kc, vc, gc = inp
        b = jnp.cumsum(gc, axis=2)
        inter = jnp.einsum('bhtk,bhkv->bhtv', qc * jnp.exp(b), S)
        diff = b[:, :, :, None, :] - b[:, :, None, :, :]
        decay = jnp.exp(jnp.where(causal[:, :, None], diff, -jnp.inf))
        scores = jnp.einsum('bhtk,bhsk,bhtsk->bhts', qc, kc, decay)
        intra = jnp.einsum('bhts,bhsv->bhtv', scores, vc)
        b_last = b[:, :, -1:, :]
        S_new = jnp.exp(b_last[:, :, 0, :])[..., None] * S + jnp.einsum('bhsk,bhsv->bhkv', kc * jnp.exp(b_last - b), vc)
        return S_new, inter + intra

    S_fin, o = lax.scan(step, s0, (to_chunks(q), to_chunks(k), to_chunks(v), to_chunks(log_f)))
    o = o.transpose(1, 0, 3, 2, 4).reshape(B, L, H, DV)
    return o, S_fin


def hgrn_direction(q, z, v, lb, s0):
    f = lb + (1 - lb) * jax.nn.sigmoid(z.astype(F32))
    return chunk_scan(q.astype(F32), 1 - f, v.astype(F32), jnp.log(f), s0.astype(F32))


def hgrn_lower_bounds(logits, j):
    p = jax.nn.softmax(logits.astype(F32), axis=1)
    lb = jnp.cumsum(p, axis=1)[:, j].reshape(2, HGRN_HEADS, HGRN_DK)
    return lb[0], lb[1]


def hgrn_mixer(h, w_in, lb_f, lb_b, onorm, w_out, s0_f, s0_b):
    B, L, _ = h.shape
    q, zf, zb, i, g = jnp.split(h @ w_in, 5, axis=-1)
    heads = lambda t: t.reshape(B, L, HGRN_HEADS, -1)
    q, zf, zb, i = heads(q), heads(zf), heads(zb), heads(i)
    o_f, s_f = hgrn_direction(q, zf, i, lb_f, s0_f)
    o_b, s_b = hgrn_direction(q[:, ::-1], zb[:, ::-1], i[:, ::-1], lb_b, s0_b)
    o = (o_f + o_b[:, ::-1]).astype(h.dtype)
    o = rmsnorm(o, onorm).reshape(B, L, D_MODEL) * jax.nn.silu(g)
    return o @ w_out, s_f, s_b


def diff_attn_project(h, w_in):
    B, L, _ = h.shape
    q, k, v = jnp.split(h @ w_in, 3, axis=-1)
    return (q.reshape(B, L, 2 * DIFF_HEADS, DIFF_HD),
            k.reshape(B, L, 2 * DIFF_HEADS, DIFF_HD),
            v.reshape(B, L, DIFF_HEADS, 2 * DIFF_HD))


def diff_lambda(lam_p, lam_init):
    lp = lam_p.astype(F32)
    return jnp.exp(jnp.sum(lp[0] * lp[1])) - jnp.exp(jnp.sum(lp[2] * lp[3])) + lam_init


def diff_attention(q, k, v, lam, subln, lam_init):
    B, Lq = q.shape[:2]
    nb = Lq // Q_BLOCK
    qb = q.reshape(B, nb, Q_BLOCK, DIFF_HEADS, 2, DIFF_HD).transpose(1, 0, 2, 3, 4, 5)
    k5 = k.reshape(B, -1, DIFF_HEADS, 2, DIFF_HD).astype(F32)
    v32 = v.astype(F32)
    scale = DIFF_HD ** -0.5

    def block(qblk):
        s = jnp.einsum('bqhjd,bkhjd->bhjqk', qblk.astype(F32), k5) * scale
        p = jax.nn.softmax(s, axis=-1)
        a = p[:, :, 0] - lam * p[:, :, 1]
        return jnp.einsum('bhqk,bkhe->bqhe', a, v32)

    o = lax.map(block, qb)
    o = o.transpose(1, 0, 2, 3, 4).reshape(B, Lq, DIFF_HEADS, 2 * DIFF_HD)
    o = rmsnorm(o, subln) * (1 - lam_init)
    return o.reshape(B, Lq, D_MODEL).astype(q.dtype)


def conv_ffn(h, w_up, cw, cb, w_down):
    u = h @ w_up
    L = u.shape[1]
    up = jnp.pad(u, ((0, 0), (1, 1), (0, 0)))
    u = up[:, :L] * cw[0] + up[:, 1:L + 1] * cw[1] + up[:, 2:] * cw[2] + cb
    gate, val = jnp.split(u, 2, axis=-1)
    return (jax.nn.silu(gate) * val) @ w_down


def setup_inputs(seed: int = 0) -> dict:
    key = jax.random.key(seed)
    ks = jax.random.split(key, 24)
    D = D_MODEL
    nrm = lambda k, s, sc: jax.random.normal(k, s, F32) * sc
    return {
        'x_prompt': nrm(ks[0], (BATCH, SEQ, D), 1.0),
        'x_sample': nrm(ks[1], (DEC_BATCH, DEC_SEQ, D), 1.0),
        'c': nrm(ks[2], (DEC_BATCH, D), 1.0),
        'cache_hgrn_state': nrm(ks[3], (DEC_BATCH, N_HGRN, 2, HGRN_HEADS, HGRN_DK, HGRN_DV), 0.5),
        'cache_attn_k': nrm(ks[4], (DEC_BATCH, N_ATTN, PAST_LEN, 2 * DIFF_HEADS, DIFF_HD), 1.0),
        'cache_attn_v': nrm(ks[5], (DEC_BATCH, N_ATTN, PAST_LEN, DIFF_HEADS, 2 * DIFF_HD), 1.0),
        'c_ctx': nrm(ks[6], (D,), 1.0),
        'w_mod': nrm(ks[7], (DEPTH, D, 6 * D), 0.5 * D ** -0.5),
        'b_mod': nrm(ks[8], (DEPTH, 6 * D), 0.02),
        'norm_mix': 1.0 + nrm(ks[9], (DEPTH, D), 0.05),
        'norm_ffn': 1.0 + nrm(ks[10], (DEPTH, D), 0.05),
        'w_hgrn_in': nrm(ks[11], (N_HGRN, D, 5 * D), D ** -0.5),
        'hgrn_lb_logits': nrm(ks[12], (2, N_HGRN + 1, D), 0.5),
        'hgrn_onorm': 1.0 + nrm(ks[13], (N_HGRN, HGRN_DV), 0.05),
        'w_hgrn_out': nrm(ks[14], (N_HGRN, D, D), D ** -0.5),
        'w_attn_in': nrm(ks[15], (N_ATTN, D, 3 * D), D ** -0.5),
        'attn_lambda': nrm(ks[16], (N_ATTN, 4, DIFF_HD), 0.1),
        'attn_subln': 1.0 + nrm(ks[17], (N_ATTN, 2 * DIFF_HD), 0.05),
        'w_attn_out': nrm(ks[18], (N_ATTN, D, D), D ** -0.5),
        'w_ffn_up': nrm(ks[19], (DEPTH, D, 2 * D_FF), D ** -0.5),
        'ffn_conv_w': nrm(ks[20], (DEPTH, CONV_W, 2 * D_FF), CONV_W ** -0.5),
        'ffn_conv_b': nrm(ks[21], (DEPTH, 2 * D_FF), 0.02),
        'w_ffn_down': nrm(ks[22], (DEPTH, D_FF, D), D_FF ** -0.5),
        'norm_final': 1.0 + nrm(ks[23], (D,), 0.05),
    }


def reference(x_prompt, x_sample, c, cache_hgrn_state, cache_attn_k, cache_attn_v, c_ctx,
              w_mod, b_mod, norm_mix, norm_ffn, w_hgrn_in, hgrn_lb_logits, hgrn_onorm, w_hgrn_out,
              w_attn_in, attn_lambda, attn_subln, w_attn_out, w_ffn_up, ffn_conv_w, ffn_conv_b,
              w_ffn_down, norm_final):
    xp, xs = x_prompt, x_sample
    hgrn_states, attn_ks, attn_vs = [], [], []
    for l in range(DEPTH):
        mp = ada_mod(c_ctx[None, :], w_mod[l], b_mod[l])
        ms = ada_mod(c, w_mod[l], b_mod[l])
        hp = modulate(rmsnorm(xp, norm_mix[l]), mp[0], mp[1])
        hs = modulate(rmsnorm(xs, norm_mix[l]), ms[0], ms[1])
        j = l // N_MIXERS
        if l % N_MIXERS == 0:
            lb_f, lb_b = hgrn_lower_bounds(hgrn_lb_logits, j)
            zeros = jnp.zeros((xp.shape[0], HGRN_HEADS, HGRN_DK, HGRN_DV), F32)
            op, sf, sb = hgrn_mixer(hp, w_hgrn_in[j], lb_f, lb_b, hgrn_onorm[j], w_hgrn_out[j], zeros, zeros)
            hgrn_states.append(jnp.stack([sf, sb], axis=1).astype(xp.dtype))
            s0 = cache_hgrn_state[:, j]
            os_, _, _ = hgrn_mixer(hs, w_hgrn_in[j], lb_f, lb_b, hgrn_onorm[j], w_hgrn_out[j], s0[:, 0], s0[:, 1])
        else:
            lam_init = 0.8 - 0.6 * math.exp(-0.3 * l)
            lam = diff_lambda(attn_lambda[j], lam_init)
            qp, kp, vp = diff_attn_project(hp, w_attn_in[j])
            op = diff_attention(qp, kp, vp, lam, attn_subln[j], lam_init) @ w_attn_out[j]
            attn_ks.append(kp)
            attn_vs.append(vp)
            qs, ks_, vs_ = diff_attn_project(hs, w_attn_in[j])
            cos, sin = axial_rope(xs.shape[1], DIFF_HD)
            qs = apply_axial_rope(qs, cos, sin)
            ks_ = apply_axial_rope(ks_, cos, sin)
            k_all = jnp.concatenate([ks_, cache_attn_k[:, j].astype(ks_.dtype)], axis=1)
            v_all = jnp.concatenate([vs_, cache_attn_v[:, j].astype(vs_.dtype)], axis=1)
            os_ = diff_attention(qs, k_all, v_all, lam, attn_subln[j], lam_init) @ w_attn_out[j]
        xp = xp + mp[2] * op
        xs = xs + ms[2] * os_
        hp = modulate(rmsnorm(xp, norm_ffn[l]), mp[3], mp[4])
        hs = modulate(rmsnorm(xs, norm_ffn[l]), ms[3], ms[4])
        xp = xp + mp[5] * conv_ffn(hp, w_ffn_up[l], ffn_conv_w[l], ffn_conv_b[l], w_ffn_down[l])
        xs = xs + ms[5] * conv_ffn(hs, w_ffn_up[l], ffn_conv_w[l], ffn_conv_b[l], w_ffn_down[l])
    y_prompt = rmsnorm(xp, norm_final)
    y_sample = rmsnorm(xs, norm_final)
    new_hgrn_state = jnp.stack(hgrn_states, axis=1)
    new_attn_k = jnp.stack(attn_ks, axis=1)
    new_attn_v = jnp.stack(attn_vs, axis=1)
    return (y_prompt, y_sample, new_hgrn_state, new_attn_k, new_attn_v)
```

```python
import functools
import math

import jax
import jax.numpy as jnp
from jax import lax
from jax.experimental import pallas as pl
from jax.experimental.pallas import tpu as pltpu

F32 = jnp.float32
BF16 = jnp.bfloat16

EPS = 1e-6
GRID_W = 64
ROPE_BASE = 10000.0
N_MIXERS = 2

LANES = 128
MOD_ROWS = 16
HGRN_CHUNK = 64
HGRN_SUB = 16
VMEM_LIMIT = 56 * 1024 * 1024

NT_DIMS = (((1,), (1,)), ((), ()))
TN_DIMS = (((0,), (0,)), ((), ()))


def _params(*sem):
    return pltpu.CompilerParams(dimension_semantics=sem, vmem_limit_bytes=VMEM_LIMIT)


def _tile(n, want):
    best = LANES
    for cand in range(LANES, min(n, want) + 1, LANES):
        if n % cand == 0:
            best = cand
    assert n % best == 0
    return best


def _silu(x):
    return x * jax.nn.sigmoid(x)


def _split_bf16(x):
    hi = x.astype(BF16)
    lo = (x - hi.astype(F32)).astype(BF16)
    return hi, lo


def _mod_kernel(c_ref, w_ref, b_ref, o_ref):
    a_hi, a_lo = _split_bf16(_silu(c_ref[...]))
    w_hi, w_lo = _split_bf16(w_ref[...])
    acc = jnp.dot(a_hi, w_hi, preferred_element_type=F32)
    acc += jnp.dot(a_lo, w_hi, preferred_element_type=F32)
    acc += jnp.dot(a_hi, w_lo, preferred_element_type=F32)
    o_ref[...] = acc + b_ref[...]


def _modulation(cvec, w_mod, b_mod, tn=512):
    depth, d, n = w_mod.shape
    tn = _tile(n, tn)
    return pl.pallas_call(
        _mod_kernel,
        grid=(depth, n // tn),
        in_specs=[
            pl.BlockSpec((MOD_ROWS, d), lambda l, j: (0, 0)),
            pl.BlockSpec((None, d, tn), lambda l, j: (l, 0, j)),
            pl.BlockSpec((None, 1, tn), lambda l, j: (l, 0, j)),
        ],
        out_specs=pl.BlockSpec((None, MOD_ROWS, tn), lambda l, j: (l, 0, j)),
        out_shape=jax.ShapeDtypeStruct((depth, MOD_ROWS, n), F32),
        compiler_params=_params("arbitrary", "arbitrary"),
        name="modulation",
    )(cvec, w_mod, b_mod.reshape(depth, 1, n))


def _norm_mod(x, g, shift, scale):
    y = x * lax.rsqrt(jnp.mean(x * x, axis=-1, keepdims=True) + EPS) * g
    return y * (1.0 + scale) + shift


def _seq_of_row(row0, p_rows, s_len):
    return jnp.where(row0 < p_rows, 0, 1 + (row0 - p_rows) // s_len)


def _mod_spec(chunk, d, tm, p_rows, s_len, tile_off=0):
    return pl.BlockSpec(
        (None, 1, d),
        lambda i, j: (_seq_of_row((i + tile_off) * tm, p_rows, s_len), 0, chunk))


def _proj_kernel(x_ref, g_ref, sh_ref, sc_ref, w_ref, *rest, rope, q_tiles, qk_tiles, q_scale):
    if rope:
        cos_ref, sin_ref, o_ref, h_ref = rest
    else:
        o_ref, h_ref = rest
    j = pl.program_id(1)

    @pl.when(j == 0)
    def _():
        h_ref[...] = _norm_mod(x_ref[...], g_ref[...], sh_ref[...], sc_ref[...]).astype(BF16)

    acc = jnp.dot(h_ref[...], w_ref[...], preferred_element_type=F32)
    if not rope:
        o_ref[...] = acc.astype(o_ref.dtype)
        return

    @pl.when(j < qk_tiles)
    def _():
        a = acc * jnp.where(j < q_tiles, q_scale, 1.0)
        cos = cos_ref[...]
        sin = sin_ref[...]
        lane = lax.broadcasted_iota(jnp.int32, cos.shape, 1)
        first_of_pair = (lane // (LANES // 4)) % 2 == 0
        outs = []
        for s in range(a.shape[1] // LANES):
            xs = a[:, s * LANES:(s + 1) * LANES]
            partner = jnp.where(first_of_pair,
                                pltpu.roll(xs, LANES - LANES // 4, 1),
                                pltpu.roll(xs, LANES // 4, 1))
            outs.append(xs * cos + partner * sin)
        o_ref[...] = jnp.concatenate(outs, axis=1).astype(o_ref.dtype)

    @pl.when(j >= qk_tiles)
    def _():
        o_ref[...] = acc.astype(o_ref.dtype)


def _project(x, g, mod, shift_chunk, scale_chunk, w, *, row_off, rows, out_dtype,
             p_rows, s_len, tm=512, tn=512, rope_tables=None, q_scale=1.0):
    t, d = x.shape
    n = w.shape[1]
    tile_off = row_off // tm
    rope = rope_tables is not None
    tn = _tile(n // 3 if rope else n, tn)
    in_specs = [
        pl.BlockSpec((tm, d), lambda i, j: (i + tile_off, 0)),
        pl.BlockSpec((1, d), lambda i, j: (0, 0)),
        _mod_spec(shift_chunk, d, tm, p_rows, s_len, tile_off),
        _mod_spec(scale_chunk, d, tm, p_rows, s_len, tile_off),
        pl.BlockSpec((d, tn), lambda i, j: (0, j)),
    ]
    args = [x, g.reshape(1, d), mod, mod, w]
    q_tiles = qk_tiles = 0
    if rope:
        cos, sin = rope_tables
        tiles_per_seq = s_len // tm
        in_specs += [pl.BlockSpec((tm, LANES), lambda i, j: (i % tiles_per_seq, 0))] * 2
        args += [cos, sin]
        q_tiles = (n // 3) // tn
        qk_tiles = 2 * q_tiles
    kern = functools.partial(_proj_kernel, rope=rope, q_tiles=q_tiles, qk_tiles=qk_tiles,
                             q_scale=q_scale)
    return pl.pallas_call(
        kern,
        grid=(rows // tm, n // tn),
        in_specs=in_specs,
        out_specs=pl.BlockSpec((tm, tn), lambda i, j: (i, j)),
        out_shape=jax.ShapeDtypeStruct((rows, n), out_dtype),
        scratch_shapes=[pltpu.VMEM((tm, d), BF16)],
        compiler_params=_params("arbitrary", "arbitrary"),
        name="norm_mod_project",
    )(*args)


def _hgrn_constants(direction):
    c, m = HGRN_CHUNK, HGRN_SUB
    t = lax.broadcasted_iota(jnp.int32, (c, c), 0)
    s = lax.broadcasted_iota(jnp.int32, (c, c), 1)
    if direction == 0:
        anchor = (t // m) * m + (m // 2 - 1)
        tri = s <= t
        tri_anchor = s <= anchor
    else:
        anchor = (t // m) * m + m // 2
        tri = s >= t
        tri_anchor = s >= anchor
    cum = jnp.concatenate([jnp.where(tri, 1.0, 0.0), jnp.where(tri_anchor, 1.0, 0.0)],
                          axis=0).astype(BF16)
    row = lax.broadcasted_iota(jnp.int32, (c, 1), 0)
    if direction == 0:
        block_masks = [row < m * (i + 1) for i in range(c // m)]
    else:
        block_masks = [row >= m * i for i in range(c // m)]
    return cum, tri, block_masks


def _hgrn_chunk(q, z, v, lb, st, consts, direction):
    c, m = HGRN_CHUNK, HGRN_SUB
    cum, tri, block_masks = consts
    f = lb + (1.0 - lb) * jax.nn.sigmoid(z)
    g = jnp.log(f)
    k = 1.0 - f
    g_hi = g.astype(BF16)
    r1 = g - g_hi.astype(F32)
    g_mid = r1.astype(BF16)
    g_lo = (r1 - g_mid.astype(F32)).astype(BF16)
    sums = jnp.dot(cum, jnp.concatenate([g_hi, g_mid, g_lo], axis=1),
                   preferred_element_type=F32)
    dk = g.shape[1]
    sums = sums[:, :dk] + sums[:, dk:2 * dk] + sums[:, 2 * dk:]
    b = sums[:c]
    b_anchor = sums[c:]
    q_rel = (q * jnp.exp(b - b_anchor)).astype(BF16)
    k_rel = []
    for i in range(c // m):
        a_row = i * m + (m // 2 - 1 if direction == 0 else m // 2)
        ki = k * jnp.exp(b[a_row:a_row + 1, :] - b)
        k_rel.append(jnp.where(block_masks[i], ki, 0.0).astype(BF16))
    scores = lax.dot_general(q_rel, jnp.concatenate(k_rel, axis=0), NT_DIMS,
                             preferred_element_type=F32)
    a = jnp.concatenate([scores[i * m:(i + 1) * m, i * c:(i + 1) * c] for i in range(c // m)],
                        axis=0)
    a = jnp.where(tri, a, 0.0).astype(BF16)
    v_bf = v.astype(BF16)
    o = jnp.dot(a, v_bf, preferred_element_type=F32)
    o += lax.dot_general((q * jnp.exp(b)).astype(BF16), st.astype(BF16), NT_DIMS,
                         preferred_element_type=F32)
    edge = c - 1 if direction == 0 else 0
    total = b[edge:edge + 1, :]
    k_dec = (k * jnp.exp(total - b)).astype(BF16)
    st_new = st * jnp.exp(total) + lax.dot_general(v_bf, k_dec, TN_DIMS,
                                                   preferred_element_type=F32)
    return st_new, o


def _hgrn_kernel(*refs, seq_len, layer_j, n_lb, has_s0, emit_state):
    q_ref, zf_ref, zb_ref, i_ref, g_ref, lg_ref, on_ref = refs[:7]
    pos = 7
    s0_ref = st_ref = None
    if has_s0:
        s0_ref = refs[pos]
        pos += 1
    o_ref = refs[pos]
    pos += 1
    if emit_state:
        st_ref = refs[pos]
        pos += 1
    of_scr, ob_scr = refs[pos], refs[pos + 1]

    c = HGRN_CHUNK
    n = seq_len // c
    dv = q_ref.shape[1]

    lbs = []
    for d in range(2):
        lg = lg_ref[d * n_lb:(d + 1) * n_lb, :]
        e = jnp.exp(lg - jnp.max(lg, axis=0, keepdims=True))
        lbs.append(jnp.sum(e[:layer_j + 1], axis=0, keepdims=True) / jnp.sum(e, axis=0, keepdims=True))
    consts = [_hgrn_constants(0), _hgrn_constants(1)]

    if has_s0:
        st0 = (s0_ref[0].T, s0_ref[1].T)
    else:
        st0 = (jnp.zeros((dv, dv), F32), jnp.zeros((dv, dv), F32))

    def body(ci, carry):
        st_f, st_b = carry
        rf = pl.ds(pl.multiple_of(ci * c, c), c)
        st_f, o_f = _hgrn_chunk(q_ref[rf, :], zf_ref[rf, :], i_ref[rf, :], lbs[0], st_f,
                                consts[0], 0)
        of_scr[rf, :] = o_f
        rb = pl.ds(pl.multiple_of((n - 1 - ci) * c, c), c)
        st_b, o_b = _hgrn_chunk(q_ref[rb, :], zb_ref[rb, :], i_ref[rb, :], lbs[1], st_b,
                                consts[1], 1)
        ob_scr[rb, :] = o_b
        return st_f, st_b

    st_f, st_b = lax.fori_loop(0, n, body, st0)
    if emit_state:
        st_ref[0] = st_f.T
        st_ref[1] = st_b.T

    fin_rows = math.gcd(seq_len, 256)
    onorm = on_ref[...]

    def finish(ri, _):
        r = pl.ds(pl.multiple_of(ri * fin_rows, fin_rows), fin_rows)
        o = of_scr[r, :] + ob_scr[r, :]
        y = o * lax.rsqrt(jnp.mean(o * o, axis=-1, keepdims=True) + EPS) * onorm
        o_ref[r, :] = (y * _silu(g_ref[r, :])).astype(o_ref.dtype)
        return 0

    lax.fori_loop(0, seq_len // fin_rows, finish, 0)


def _hgrn_scan(proj, lb_logits, onorm, s0, *, layer_j, n_seq, seq_len, row_off, heads, emit_state):
    dk = LANES
    d = heads * dk
    blk_off = row_off // seq_len
    n_lb = lb_logits.shape[1]

    def sec(k):
        return pl.BlockSpec((seq_len, dk), lambda b, h: (b + blk_off, k * heads + h))

    in_specs = [sec(0), sec(1), sec(2), sec(3), sec(4),
                pl.BlockSpec((2 * n_lb, dk), lambda b, h: (0, h)),
                pl.BlockSpec((1, dk), lambda b, h: (0, 0))]
    args = [proj] * 5 + [lb_logits.reshape(2 * n_lb, d), onorm.reshape(1, dk)]
    if s0 is not None:
        in_specs.append(pl.BlockSpec((None, None, 2, None, dk, dk),
                                     lambda b, h: (b, layer_j, 0, h, 0, 0)))
        args.append(s0)
    out_shape = [jax.ShapeDtypeStruct((n_seq * seq_len, d), BF16)]
    out_specs = [pl.BlockSpec((seq_len, dk), lambda b, h: (b, h))]
    if emit_state:
        out_shape.append(jax.ShapeDtypeStruct((n_seq, 2, heads, dk, dk), F32))
        out_specs.append(pl.BlockSpec((None, 2, None, dk, dk), lambda b, h: (b, 0, h, 0, 0)))
    kern = functools.partial(_hgrn_kernel, seq_len=seq_len, layer_j=layer_j, n_lb=n_lb,
                             has_s0=s0 is not None, emit_state=emit_state)
    return pl.pallas_call(
        kern,
        grid=(n_seq, heads),
        in_specs=in_specs,
        out_specs=out_specs,
        out_shape=out_shape,
        scratch_shapes=[pltpu.VMEM((seq_len, dk), F32), pltpu.VMEM((seq_len, dk), F32)],
        compiler_params=_params("arbitrary", "arbitrary"),
        name="hgrn_scan",
    )(*args)


def _attn_kernel(*refs, has_cache, lam_init, q_scale):
    q_ref, k_ref, v_ref = refs[:3]
    pos = 3
    kc_ref = vc_ref = None
    if has_cache:
        kc_ref, vc_ref = refs[3], refs[4]
        pos = 5
    lam_ref, sub_ref, o_ref = refs[pos:pos + 3]

    lp = lam_ref[...]
    lam = (jnp.exp(jnp.sum(lp[0:1] * lp[1:2], axis=-1, keepdims=True))
           - jnp.exp(jnp.sum(lp[2:3] * lp[3:4], axis=-1, keepdims=True)) + lam_init)

    hd = LANES
    v_new = v_ref[...].astype(BF16)
    v_old = vc_ref[...].astype(BF16) if has_cache else None
    outs = []
    for j in range(2):
        cols = slice(j * hd, (j + 1) * hd)
        qj = q_ref[:, cols]
        if q_scale != 1.0:
            qj = qj.astype(F32) * q_scale
        qj = qj.astype(BF16)
        s_new = lax.dot_general(qj, k_ref[:, cols].astype(BF16), NT_DIMS,
                                preferred_element_type=F32)
        mx = jnp.max(s_new, axis=-1, keepdims=True)
        if has_cache:
            s_old = lax.dot_general(qj, kc_ref[:, cols].astype(BF16), NT_DIMS,
                                    preferred_element_type=F32)
            mx = jnp.maximum(mx, jnp.max(s_old, axis=-1, keepdims=True))
        p_new = jnp.exp(s_new - mx)
        den = jnp.sum(p_new, axis=-1, keepdims=True)
        acc = jnp.dot(p_new.astype(BF16), v_new, preferred_element_type=F32)
        if has_cache:
            p_old = jnp.exp(s_old - mx)
            den += jnp.sum(p_old, axis=-1, keepdims=True)
            acc += jnp.dot(p_old.astype(BF16), v_old, preferred_element_type=F32)
        outs.append(acc / den)
    o = outs[0] - lam * outs[1]
    y = o * lax.rsqrt(jnp.mean(o * o, axis=-1, keepdims=True) + EPS) * sub_ref[...]
    o_ref[...] = (y * (1.0 - lam_init)).astype(o_ref.dtype)


def _diff_attention(qkv, cache_k, cache_v, lam_p, subln, *, n_seq, seq_len, heads, lam_init,
                    q_scale, tq):
    hd2 = 2 * LANES
    d = heads * hd2
    qb = seq_len // tq
    in_specs = [
        pl.BlockSpec((tq, hd2), lambda b, h, i: (b * qb + i, h)),
        pl.BlockSpec((seq_len, hd2), lambda b, h, i: (b, heads + h)),
        pl.BlockSpec((seq_len, hd2), lambda b, h, i: (b, 2 * heads + h)),
    ]
    args = [qkv, qkv, qkv]
    if cache_k is not None:
        past = cache_k.shape[1]
        in_specs += [pl.BlockSpec((None, past, hd2), lambda b, h, i: (b, 0, h))] * 2
        args += [cache_k, cache_v]
    in_specs += [pl.BlockSpec(lam_p.shape, lambda b, h, i: (0, 0)),
                 pl.BlockSpec((1, hd2), lambda b, h, i: (0, 0))]
    args += [lam_p, subln.reshape(1, hd2)]
    kern = functools.partial(_attn_kernel, has_cache=cache_k is not None, lam_init=lam_init,
                             q_scale=q_scale)
    return pl.pallas_call(
        kern,
        grid=(n_seq, heads, qb),
        in_specs=in_specs,
        out_specs=pl.BlockSpec((tq, hd2), lambda b, h, i: (b * qb + i, h)),
        out_shape=jax.ShapeDtypeStruct((n_seq * seq_len, d), BF16),
        compiler_params=_params("arbitrary", "arbitrary", "arbitrary"),
        name="diff_attention",
    )(*args)


def _out_proj_kernel(*refs, seg_tiles):
    nseg = len(seg_tiles)
    a_refs = refs[:nseg]
    w_ref, x_ref, gate_ref, o_ref = refs[nseg:]
    i = pl.program_id(0)
    lo = 0
    for a_ref, cnt in zip(a_refs, seg_tiles):
        @pl.when((i >= lo) & (i < lo + cnt))
        def _(a_ref=a_ref):
            y = jnp.dot(a_ref[...], w_ref[...], preferred_element_type=F32)
            o_ref[...] = x_ref[...] + gate_ref[...] * y
        lo += cnt


def _out_project(a_segs, w, x, mod, gate_chunk, *, p_rows, s_len, tm=512, tn=1024):
    t, d = x.shape
    kdim = w.shape[0]
    tn = _tile(d, tn)
    seg_tiles = tuple(a.shape[0] // tm for a in a_segs)
    in_specs = []
    lo = 0
    for cnt in seg_tiles:
        in_specs.append(pl.BlockSpec(
            (tm, kdim), lambda i, j, lo=lo, cnt=cnt: (jnp.clip(i - lo, 0, cnt - 1), 0)))
        lo += cnt
    in_specs += [
        pl.BlockSpec((kdim, tn), lambda i, j: (0, j)),
        pl.BlockSpec((tm, tn), lambda i, j: (i, j)),
        pl.BlockSpec((None, 1, tn),
                     lambda i, j: (_seq_of_row(i * tm, p_rows, s_len), 0,
                                   gate_chunk * (d // tn) + j)),
    ]
    return pl.pallas_call(
        functools.partial(_out_proj_kernel, seg_tiles=seg_tiles),
        grid=(t // tm, d // tn),
        in_specs=in_specs,
        out_specs=pl.BlockSpec((tm, tn), lambda i, j: (i, j)),
        out_shape=jax.ShapeDtypeStruct((t, d), F32),
        compiler_params=_params("arbitrary", "arbitrary"),
        name="out_project_residual",
    )(*a_segs, w, x, mod)


def _ffn_up_kernel(x_ref, xp_ref, xn_ref, g_ref, sh_ref, sc_ref, wg_ref, wv_ref,
                   cwg_ref, cwv_ref, cbg_ref, cbv_ref, o_ref, h_ref, halo_ref,
                   *, tm, p_rows, p_len, s_len):
    i = pl.program_id(0)
    j = pl.program_id(1)
    hs = xp_ref.shape[0]

    @pl.when(j == 0)
    def _():
        g, sh, sc = g_ref[...], sh_ref[...], sc_ref[...]
        h_ref[...] = _norm_mod(x_ref[...], g, sh, sc).astype(BF16)
        halo_ref[0:hs, :] = _norm_mod(xp_ref[...], g, sh, sc).astype(BF16)
        halo_ref[hs:2 * hs, :] = _norm_mod(xn_ref[...], g, sh, sc).astype(BF16)

    local = lax.broadcasted_iota(jnp.int32, (tm, 1), 0)
    row = i * tm + local
    pos = jnp.where(row < p_rows, lax.rem(row, p_len), lax.rem(jnp.maximum(row - p_rows, 0), s_len))
    length = jnp.where(row < p_rows, p_len, s_len)
    seq_first = pos == 0
    seq_last = pos == length - 1

    def conv(w_ref, cw_ref, cb_ref):
        u = jnp.dot(h_ref[...], w_ref[...], preferred_element_type=F32)
        uh = jnp.dot(halo_ref[...], w_ref[...], preferred_element_type=F32)
        up = jnp.where(local == 0, uh[hs - 1:hs, :], pltpu.roll(u, 1, 0))
        up = jnp.where(seq_first, 0.0, up)
        un = jnp.where(local == tm - 1, uh[hs:hs + 1, :], pltpu.roll(u, tm - 1, 0))
        un = jnp.where(seq_last, 0.0, un)
        cw = cw_ref[...]
        return up * cw[0:1] + u * cw[1:2] + un * cw[2:3] + cb_ref[...]

    gate = conv(wg_ref, cwg_ref, cbg_ref)
    val = conv(wv_ref, cwv_ref, cbv_ref)
    o_ref[...] = (_silu(gate) * val).astype(o_ref.dtype)


def _ffn_up(x, g, mod, shift_chunk, scale_chunk, w_up, cw, cb, *, p_rows, p_len, s_len,
            tm=512, tn=512):
    t, d = x.shape
    ff = w_up.shape[1] // 2
    tn = _tile(ff, tn)
    nj = ff // tn
    hs = 16
    per = tm // hs
    last_blk = t // hs - 1
    in_specs = [
        pl.BlockSpec((tm, d), lambda i, j: (i, 0)),
        pl.BlockSpec((hs, d), lambda i, j: (jnp.maximum(i * per - 1, 0), 0)),
        pl.BlockSpec((hs, d), lambda i, j: (jnp.minimum((i + 1) * per, last_blk), 0)),
        pl.BlockSpec((1, d), lambda i, j: (0, 0)),
        _mod_spec(shift_chunk, d, tm, p_rows, s_len),
        _mod_spec(scale_chunk, d, tm, p_rows, s_len),
        pl.BlockSpec((d, tn), lambda i, j: (0, j)),
        pl.BlockSpec((d, tn), lambda i, j: (0, j + nj)),
        pl.BlockSpec((cw.shape[0], tn), lambda i, j: (0, j)),
        pl.BlockSpec((cw.shape[0], tn), lambda i, j: (0, j + nj)),
        pl.BlockSpec((1, tn), lambda i, j: (0, j)),
        pl.BlockSpec((1, tn), lambda i, j: (0, j + nj)),
    ]
    cb2 = cb.reshape(1, 2 * ff)
    kern = functools.partial(_ffn_up_kernel, tm=tm, p_rows=p_rows, p_len=p_len, s_len=s_len)
    return pl.pallas_call(
        kern,
        grid=(t // tm, nj),
        in_specs=in_specs,
        out_specs=pl.BlockSpec((tm, tn), lambda i, j: (i, j)),
        out_shape=jax.ShapeDtypeStruct((t, ff), BF16),
        scratch_shapes=[pltpu.VMEM((tm, d), BF16), pltpu.VMEM((2 * hs, d), BF16)],
        compiler_params=_params("arbitrary", "arbitrary"),
        name="ffn_up_conv",
    )(x, x, x, g.reshape(1, d), mod, mod, w_up, w_up, cw, cw, cb2, cb2)


def _final_norm_kernel(x_ref, g_ref, o_ref):
    x = x_ref[...]
    o_ref[...] = x * lax.rsqrt(jnp.mean(x * x, axis=-1, keepdims=True) + EPS) * g_ref[...]


def _final_norm(x, g, *, row_off, rows, tm=512):
    d = x.shape[1]
    off = row_off // tm
    return pl.pallas_call(
        _final_norm_kernel,
        grid=(rows // tm,),
        in_specs=[pl.BlockSpec((tm, d), lambda i: (i + off, 0)),
                  pl.BlockSpec((1, d), lambda i: (0, 0))],
        out_specs=pl.BlockSpec((tm, d), lambda i: (i, 0)),
        out_shape=jax.ShapeDtypeStruct((rows, d), F32),
        compiler_params=_params("arbitrary"),
        name="final_norm",
    )(x, g.reshape(1, d))


def _rope_tables(seq_len, hd):
    pos = jnp.arange(seq_len)
    row = (pos // GRID_W).astype(F32)
    col = (pos % GRID_W).astype(F32)
    nf = hd // 4
    inv = ROPE_BASE ** (-jnp.arange(nf, dtype=F32) / nf)
    ar = row[:, None] * inv
    ac = col[:, None] * inv
    cos = jnp.concatenate([jnp.cos(ar), jnp.cos(ar), jnp.cos(ac), jnp.cos(ac)], axis=1)
    sin = jnp.concatenate([-jnp.sin(ar), jnp.sin(ar), -jnp.sin(ac), jnp.sin(ac)], axis=1)
    return cos, sin


def kernel(x_prompt, x_sample, c, cache_hgrn_state, cache_attn_k, cache_attn_v, c_ctx, w_mod, b_mod, norm_mix, norm_ffn, w_hgrn_in, hgrn_lb_logits, hgrn_onorm, w_hgrn_out, w_attn_in, attn_lambda, attn_subln, w_attn_out, w_ffn_up, ffn_conv_w, ffn_conv_b, w_ffn_down, norm_final):
    batch, p_len, d = x_prompt.shape
    dec_batch, s_len, _ = x_sample.shape
    depth = w_mod.shape[0]
    p_rows = batch * p_len
    s_rows = dec_batch * s_len
    hgrn_heads = d // LANES
    diff_heads = d // (2 * LANES)
    hd = LANES
    assert 1 + dec_batch <= MOD_ROWS and p_rows % s_len == 0

    x = jnp.concatenate([x_prompt.reshape(p_rows, d), x_sample.reshape(s_rows, d)], axis=0)
    cvec = jnp.concatenate(
        [c_ctx[None, :], c, jnp.zeros((MOD_ROWS - 1 - dec_batch, d), F32)], axis=0)
    mods = _modulation(cvec, w_mod, b_mod)
    seg = dict(p_rows=p_rows, s_len=s_len)

    hgrn_states, attn_ks, attn_vs = [], [], []
    for l in range(depth):
        mod = mods[l][:, None, :]
        j = l // N_MIXERS
        if l % N_MIXERS == 0:
            proj = _project(x, norm_mix[l], mod, 0, 1, w_hgrn_in[j].astype(BF16),
                            row_off=0, rows=p_rows + s_rows, out_dtype=F32, tn=1024, **seg)
            mix_p, st = _hgrn_scan(proj, hgrn_lb_logits, hgrn_onorm[j], None, layer_j=j,
                                   n_seq=batch, seq_len=p_len, row_off=0, heads=hgrn_heads,
                                   emit_state=True)
            hgrn_states.append(st)
            mix_s, = _hgrn_scan(proj, hgrn_lb_logits, hgrn_onorm[j], cache_hgrn_state,
                                layer_j=j, n_seq=dec_batch, seq_len=s_len, row_off=p_rows,
                                heads=hgrn_heads, emit_state=False)
            w_out = w_hgrn_out[j]
        else:
            lam_init = 0.8 - 0.6 * math.exp(-0.3 * l)
            q_scale = hd ** -0.5
            w_in = w_attn_in[j].astype(BF16)
            qkv_p = _project(x, norm_mix[l], mod, 0, 1, w_in, row_off=0, rows=p_rows,
                             out_dtype=F32, **seg)
            attn_ks.append(qkv_p[:, d:2 * d].reshape(batch, p_len, 2 * diff_heads, hd))
            attn_vs.append(qkv_p[:, 2 * d:].reshape(batch, p_len, diff_heads, 2 * hd))
            mix_p = _diff_attention(qkv_p, None, None, attn_lambda[j], attn_subln[j],
                                    n_seq=batch, seq_len=p_len, heads=diff_heads,
                                    lam_init=lam_init, q_scale=q_scale, tq=p_len)
            qkv_s = _project(x, norm_mix[l], mod, 0, 1, w_in, row_off=p_rows, rows=s_rows,
                             out_dtype=BF16, rope_tables=_rope_tables(s_len, hd),
                             q_scale=q_scale, **seg)
            past = cache_attn_k.shape[2]
            mix_s = _diff_attention(qkv_s, cache_attn_k[:, j].reshape(dec_batch, past, d),
                                    cache_attn_v[:, j].reshape(dec_batch, past, d),
                                    attn_lambda[j], attn_subln[j], n_seq=dec_batch,
                                    seq_len=s_len, heads=diff_heads, lam_init=lam_init,
                                    q_scale=1.0, tq=256)
            w_out = w_attn_out[j]
        x = _out_project([mix_p, mix_s], w_out.astype(BF16), x, mod, 2, **seg)
        act = _ffn_up(x, norm_ffn[l], mod, 3, 4, w_ffn_up[l].astype(BF16), ffn_conv_w[l],
                      ffn_conv_b[l], p_len=p_len, **seg)
        x = _out_project([act], w_ffn_down[l].astype(BF16), x, mod, 5, **seg)

    y_prompt = _final_norm(x, norm_final, row_off=0, rows=p_rows).reshape(batch, p_len, d)
    y_sample = _final_norm(x, norm_final, row_off=p_rows, rows=s_rows).reshape(dec_batch, s_len, d)
    new_hgrn_state = jnp.stack(hgrn_states, axis=1)
    new_attn_k = jnp.stack(attn_ks, axis=1)
    new_attn_v = jnp.stack(attn_vs, axis=1)
    return (y_prompt, y_sample, new_hgrn_state, new_attn_k, new_attn_v)
```

```python
import functools
import math

import jax
import jax.numpy as jnp
from jax import lax
from jax.experimental import pallas as pl
from jax.experimental.pallas import tpu as pltpu

F32 = jnp.float32
BF16 = jnp.bfloat16

EPS = 1e-6
GRID_W = 64
ROPE_BASE = 10000.0
N_MIXERS = 2

LANES = 128
MXU_COLS = 256
MOD_ROWS = 16
HGRN_CHUNK = 64
HGRN_SUB = 16
HGRN_GROUP = 4
ATTN_ROW_BLOCK = 128
LOG2_E = 1.4426950408889634
VMEM_LIMIT = 56 * 1024 * 1024

NT_DIMS = (((1,), (1,)), ((), ()))
TN_DIMS = (((0,), (0,)), ((), ()))


def _params(*sem):
    return pltpu.CompilerParams(dimension_semantics=sem, vmem_limit_bytes=VMEM_LIMIT)


def _tile(n, want):
    best = LANES
    for cand in range(LANES, min(n, want) + 1, LANES):
        if n % cand == 0:
            best = cand
    assert n % best == 0
    return best


def _silu(x):
    return x * jax.nn.sigmoid(x)


def _split_bf16(x):
    hi = x.astype(BF16)
    lo = (x - hi.astype(F32)).astype(BF16)
    return hi, lo


def _mod_kernel(c_ref, w_ref, b_ref, o_ref):
    a_hi, a_lo = _split_bf16(_silu(c_ref[...]))
    w_hi, w_lo = _split_bf16(w_ref[...])
    acc = jnp.dot(a_hi, w_hi, preferred_element_type=F32)
    acc += jnp.dot(a_lo, w_hi, preferred_element_type=F32)
    acc += jnp.dot(a_hi, w_lo, preferred_element_type=F32)
    o_ref[...] = acc + b_ref[...]


def _modulation(cvec, w_mod, b_mod, tn=512):
    depth, d, n = w_mod.shape
    tn = _tile(n, tn)
    return pl.pallas_call(
        _mod_kernel,
        grid=(depth, n // tn),
        in_specs=[
            pl.BlockSpec((MOD_ROWS, d), lambda l, j: (0, 0)),
            pl.BlockSpec((None, d, tn), lambda l, j: (l, 0, j)),
            pl.BlockSpec((None, 1, tn), lambda l, j: (l, 0, j)),
        ],
        out_specs=pl.BlockSpec((None, MOD_ROWS, tn), lambda l, j: (l, 0, j)),
        out_shape=jax.ShapeDtypeStruct((depth, MOD_ROWS, n), F32),
        compiler_params=_params("arbitrary", "arbitrary"),
        name="modulation",
    )(cvec, w_mod, b_mod.reshape(depth, 1, n))


def _norm_mod(x, g, shift, scale):
    y = x * lax.rsqrt(jnp.mean(x * x, axis=-1, keepdims=True) + EPS) * g
    return y * (1.0 + scale) + shift


def _seq_of_row(row0, p_rows, s_len):
    return jnp.where(row0 < p_rows, 0, 1 + (row0 - p_rows) // s_len)


def _mod_spec(chunk, d, tm, p_rows, s_len, tile_off=0):
    return pl.BlockSpec(
        (None, 1, d),
        lambda i, j: (_seq_of_row((i + tile_off) * tm, p_rows, s_len), 0, chunk))


def _proj_kernel(x_ref, g_ref, sh_ref, sc_ref, w_ref, *rest, rope, q_tiles, qk_tiles, q_scale):
    if rope:
        cos_ref, sin_ref, o_ref, h_ref = rest
    else:
        o_ref, h_ref = rest
    j = pl.program_id(1)

    @pl.when(j == 0)
    def _():
        h_ref[...] = _norm_mod(x_ref[...], g_ref[...], sh_ref[...], sc_ref[...]).astype(BF16)

    if not rope:
        o_ref[...] = jnp.dot(h_ref[...], w_ref[...],
                             preferred_element_type=F32).astype(o_ref.dtype)
        return

    @pl.when(j < qk_tiles)
    def _():
        tn = o_ref.shape[1]
        sub = min(tn, MXU_COLS)
        h = h_ref[...]
        accs = [jnp.dot(h, w_ref[:, c:c + sub], preferred_element_type=F32)
                for c in range(0, tn, sub)]
        scale = jnp.where(j < q_tiles, q_scale, 1.0)
        cos = cos_ref[...] * scale
        sin = sin_ref[...] * scale
        lane = lax.broadcasted_iota(jnp.int32, cos.shape, 1)
        first_of_pair = (lane // (LANES // 4)) % 2 == 0
        for ci, acc in enumerate(accs):
            for s in range(sub // LANES):
                xs = acc[:, s * LANES:(s + 1) * LANES]
                partner = jnp.where(first_of_pair,
                                    pltpu.roll(xs, LANES - LANES // 4, 1),
                                    pltpu.roll(xs, LANES // 4, 1))
                col = ci * sub + s * LANES
                o_ref[:, col:col + LANES] = (xs * cos + partner * sin).astype(o_ref.dtype)

    @pl.when(j >= qk_tiles)
    def _():
        o_ref[...] = jnp.dot(h_ref[...], w_ref[...],
                             preferred_element_type=F32).astype(o_ref.dtype)


def _project(x, g, mod, shift_chunk, scale_chunk, w, *, row_off, rows, out_dtype,
             p_rows, s_len, tm=512, tn=512, rope_tables=None, q_scale=1.0):
    t, d = x.shape
    n = w.shape[1]
    tm = math.gcd(tm, p_rows, s_len)
    tile_off = row_off // tm
    rope = rope_tables is not None
    tn = _tile(n // 3 if rope else n, tn)
    in_specs = [
        pl.BlockSpec((tm, d), lambda i, j: (i + tile_off, 0)),
        pl.BlockSpec((1, d), lambda i, j: (0, 0)),
        _mod_spec(shift_chunk, d, tm, p_rows, s_len, tile_off),
        _mod_spec(scale_chunk, d, tm, p_rows, s_len, tile_off),
        pl.BlockSpec((d, tn), lambda i, j: (0, j)),
    ]
    args = [x, g.reshape(1, d), mod, mod, w]
    q_tiles = qk_tiles = 0
    if rope:
        cos, sin = rope_tables
        tiles_per_seq = s_len // tm
        in_specs += [pl.BlockSpec((tm, LANES), lambda i, j: (i % tiles_per_seq, 0))] * 2
        args += [cos, sin]
        q_tiles = (n // 3) // tn
        qk_tiles = 2 * q_tiles
    kern = functools.partial(_proj_kernel, rope=rope, q_tiles=q_tiles, qk_tiles=qk_tiles,
                             q_scale=q_scale)
    return pl.pallas_call(
        kern,
        grid=(rows // tm, n // tn),
        in_specs=in_specs,
        out_specs=pl.BlockSpec((tm, tn), lambda i, j: (i, j)),
        out_shape=jax.ShapeDtypeStruct((rows, n), out_dtype),
        scratch_shapes=[pltpu.VMEM((tm, d), BF16)],
        compiler_params=_params("arbitrary", "arbitrary"),
        name="norm_mod_project",
    )(*args)


def _hgrn_group_local(groups):
    c, m = HGRN_CHUNK, HGRN_SUB
    nb = c // m
    t = lax.broadcasted_iota(jnp.int32, (c, c), 0)
    s = lax.broadcasted_iota(jnp.int32, (c, c), 1)

    jobs = []
    for q, z, v, lb, direction in groups:
        dk = q.shape[1]
        tri = (s <= t) if direction == 0 else (s >= t)
        cum = jnp.where(tri, 1.0, 0.0).astype(BF16)
        f = lb + (1.0 - lb) * jax.nn.sigmoid(z)
        g2 = jnp.log(f) * LOG2_E
        k = 1.0 - f
        g_hi = g2.astype(BF16)
        g_lo = (g2 - g_hi.astype(F32)).astype(BF16)
        v_bf = v.astype(BF16)
        for ci in range(q.shape[0] // c):
            rows = slice(ci * c, (ci + 1) * c)
            jobs.append(dict(direction=direction, tri=tri, cum=cum, q=q[rows], k=k[rows],
                             v=v_bf[rows], g=jnp.concatenate([g_hi[rows], g_lo[rows]], axis=1)))

    def padded(x, lo, hi):
        parts = [jnp.zeros((lo, dk), BF16)] if lo else []
        parts.append(x)
        if hi < c:
            parts.append(jnp.zeros((c - hi, dk), BF16))
        return jnp.concatenate(parts, axis=0) if len(parts) > 1 else x

    for job in jobs:
        job["sums"] = jnp.dot(job["cum"], job["g"], preferred_element_type=F32)

    for job in jobs:
        direction = job["direction"]
        sums = job["sums"]
        b = sums[:, :dk] + sums[:, dk:]
        anchors = []
        for i in range(nb):
            a_row = i * m + (m // 2 - 1 if direction == 0 else m // 2)
            anchors.append(b[a_row:a_row + 1, :])
        b_anchor = jnp.concatenate([jnp.broadcast_to(r, (m, dk)) for r in anchors], axis=0)
        q_rel = (job["q"] * jnp.exp2(b - b_anchor)).astype(BF16)
        full = nb - 1 if direction == 0 else 0
        q_cat, k_cat = [], []
        k_full = None
        for i in range(nb):
            lo, hi = (0, m * (i + 1)) if direction == 0 else (m * i, c)
            ki = job["k"][lo:hi] * jnp.exp2(anchors[i] - b[lo:hi])
            if i == full:
                k_full = ki
            k_cat.append(padded(ki.astype(BF16), lo, hi))
            q_cat.append(padded(q_rel[m * i:m * (i + 1)], m * i, m * (i + 1)))
        job["a"] = lax.dot_general(jnp.concatenate(q_cat, axis=1),
                                   jnp.concatenate(k_cat, axis=1), NT_DIMS,
                                   preferred_element_type=F32)
        edge = c - 1 if direction == 0 else 0
        total = b[edge:edge + 1, :]
        job["k_dec"] = (k_full * jnp.exp2(total - anchors[full])).astype(BF16)
        job["q_in"] = (job["q"] * jnp.exp2(b)).astype(BF16)
        job["decay"] = jnp.exp2(total)

    for job in jobs:
        job["u_t"] = lax.dot_general(job["v"], job["k_dec"], TN_DIMS,
                                     preferred_element_type=F32)

    out, pos = [], 0
    for q, *_ in groups:
        n = q.shape[0] // c
        out.append([(jnp.where(j["tri"], j["a"], 0.0).astype(BF16), j["v"], j["q_in"], j["u_t"],
                     j["decay"]) for j in jobs[pos:pos + n]])
        pos += n
    return out


def _hgrn_kernel(*refs, seq_len, layer_j, n_lb, has_s0, emit_state):
    q_ref, zf_ref, zb_ref, i_ref, g_ref, lg_ref, on_ref = refs[:7]
    pos = 7
    s0_ref = st_ref = None
    if has_s0:
        s0_ref = refs[pos]
        pos += 1
    o_ref = refs[pos]
    pos += 1
    if emit_state:
        st_ref = refs[pos]
        pos += 1
    of_scr, ob_scr = refs[pos], refs[pos + 1]

    grp = math.gcd(HGRN_GROUP, seq_len // HGRN_CHUNK)
    rows = grp * HGRN_CHUNK
    n = seq_len // rows
    dv = q_ref.shape[1]

    lbs = []
    for d in range(2):
        lg = lg_ref[d * n_lb:(d + 1) * n_lb, :]
        e = jnp.exp(lg - jnp.max(lg, axis=0, keepdims=True))
        lbs.append(jnp.sum(e[:layer_j + 1], axis=0, keepdims=True) / jnp.sum(e, axis=0, keepdims=True))

    if has_s0:
        st0 = (s0_ref[0].T, s0_ref[1].T)
    else:
        st0 = (jnp.zeros((dv, dv), F32), jnp.zeros((dv, dv), F32))

    def advance(local, st):
        _, _, q_in, u_t, decay = local
        o = lax.dot_general(q_in, st.astype(BF16), NT_DIMS, preferred_element_type=F32)
        return st * decay + u_t, o

    def add_intra(local, o_inter):
        a, v = local[:2]
        return o_inter + jnp.dot(a, v, preferred_element_type=F32)

    def body(gi, carry):
        st_f, st_b = carry
        rf = pl.ds(pl.multiple_of(gi * rows, rows), rows)
        rb = pl.ds(pl.multiple_of((n - 1 - gi) * rows, rows), rows)
        loc_f, loc_b = _hgrn_group_local([
            (q_ref[rf, :], zf_ref[rf, :], i_ref[rf, :], lbs[0], 0),
            (q_ref[rb, :], zb_ref[rb, :], i_ref[rb, :], lbs[1], 1)])
        o_f, o_b = [None] * grp, [None] * grp
        for ci in range(grp):
            st_f, o_f[ci] = advance(loc_f[ci], st_f)
            st_b, o_b[grp - 1 - ci] = advance(loc_b[grp - 1 - ci], st_b)
        o_f = [add_intra(l, o) for l, o in zip(loc_f, o_f)]
        o_b = [add_intra(l, o) for l, o in zip(loc_b, o_b)]
        of_scr[rf, :] = jnp.concatenate(o_f, axis=0)
        ob_scr[rb, :] = jnp.concatenate(o_b, axis=0)
        return st_f, st_b

    st_f, st_b = lax.fori_loop(0, n, body, st0)
    if emit_state:
        st_ref[0] = st_f.T
        st_ref[1] = st_b.T

    fin_rows = math.gcd(seq_len, 256)
    onorm = on_ref[...]

    def finish(ri, _):
        r = pl.ds(pl.multiple_of(ri * fin_rows, fin_rows), fin_rows)
        o = of_scr[r, :] + ob_scr[r, :]
        y = o * lax.rsqrt(jnp.mean(o * o, axis=-1, keepdims=True) + EPS) * onorm
        o_ref[r, :] = (y * _silu(g_ref[r, :])).astype(o_ref.dtype)
        return 0

    lax.fori_loop(0, seq_len // fin_rows, finish, 0)


def _hgrn_scan(proj, lb_logits, onorm, s0, *, layer_j, n_seq, seq_len, row_off, heads, emit_state):
    dk = LANES
    d = heads * dk
    blk_off = row_off // seq_len
    n_lb = lb_logits.shape[1]

    def sec(k):
        return pl.BlockSpec((seq_len, dk), lambda b, h: (b + blk_off, k * heads + h))

    in_specs = [sec(0), sec(1), sec(2), sec(3), sec(4),
                pl.BlockSpec((2 * n_lb, dk), lambda b, h: (0, h)),
                pl.BlockSpec((1, dk), lambda b, h: (0, 0))]
    args = [proj] * 5 + [lb_logits.reshape(2 * n_lb, d), onorm.reshape(1, dk)]
    if s0 is not None:
        in_specs.append(pl.BlockSpec((None, None, 2, None, dk, dk),
                                     lambda b, h: (b, layer_j, 0, h, 0, 0)))
        args.append(s0)
    out_shape = [jax.ShapeDtypeStruct((n_seq * seq_len, d), BF16)]
    out_specs = [pl.BlockSpec((seq_len, dk), lambda b, h: (b, h))]
    if emit_state:
        out_shape.append(jax.ShapeDtypeStruct((n_seq, 2, heads, dk, dk), F32))
        out_specs.append(pl.BlockSpec((None, 2, None, dk, dk), lambda b, h: (b, 0, h, 0, 0)))
    kern = functools.partial(_hgrn_kernel, seq_len=seq_len, layer_j=layer_j, n_lb=n_lb,
                             has_s0=s0 is not None, emit_state=emit_state)
    return pl.pallas_call(
        kern,
        grid=(n_seq, heads),
        in_specs=in_specs,
        out_specs=out_specs,
        out_shape=out_shape,
        scratch_shapes=[pltpu.VMEM((seq_len, dk), F32), pltpu.VMEM((seq_len, dk), F32)],
        compiler_params=_params("arbitrary", "arbitrary"),
        name="hgrn_scan",
    )(*args)


def _attn_kernel(*refs, has_cache, lam_init, q_scale):
    q_ref, k_ref, v_ref = refs[:3]
    pos = 3
    kc_ref = vc_ref = None
    if has_cache:
        kc_ref, vc_ref = refs[3], refs[4]
        pos = 5
    lam_ref, sub_ref, o_ref = refs[pos:pos + 3]

    lp = lam_ref[...]
    lam = (jnp.exp(jnp.sum(lp[0:1] * lp[1:2], axis=-1, keepdims=True))
           - jnp.exp(jnp.sum(lp[2:3] * lp[3:4], axis=-1, keepdims=True)) + lam_init)

    hd = LANES
    v_new = v_ref[...].astype(BF16)
    v_old = vc_ref[...].astype(BF16) if has_cache else None
    tq = q_ref.shape[0]
    rb = min(tq, ATTN_ROW_BLOCK)
    chains = [(r, j) for r in range(0, tq, rb) for j in range(2)]
    scores = []
    for r, j in chains:
        cols = slice(j * hd, (j + 1) * hd)
        qj = q_ref[r:r + rb, cols]
        if q_scale != 1.0:
            qj = qj.astype(F32) * q_scale
        qj = qj.astype(BF16)
        s_new = lax.dot_general(qj, k_ref[:, cols].astype(BF16), NT_DIMS,
                                preferred_element_type=F32)
        s_old = None
        if has_cache:
            s_old = lax.dot_general(qj, kc_ref[:, cols].astype(BF16), NT_DIMS,
                                    preferred_element_type=F32)
        scores.append((s_new, s_old))
    outs = {}
    for (r, j), (s_new, s_old) in zip(chains, scores):
        mx = jnp.max(s_new, axis=-1, keepdims=True)
        if has_cache:
            mx = jnp.maximum(mx, jnp.max(s_old, axis=-1, keepdims=True))
        p_new = jnp.exp2(s_new - mx)
        den = jnp.sum(p_new, axis=-1, keepdims=True)
        acc = jnp.dot(p_new.astype(BF16), v_new, preferred_element_type=F32)
        if has_cache:
            p_old = jnp.exp2(s_old - mx)
            den += jnp.sum(p_old, axis=-1, keepdims=True)
            acc += jnp.dot(p_old.astype(BF16), v_old, preferred_element_type=F32)
        outs[r, j] = acc / den
    sub = sub_ref[...]
    for r in range(0, tq, rb):
        o = outs[r, 0] - lam * outs[r, 1]
        y = o * lax.rsqrt(jnp.mean(o * o, axis=-1, keepdims=True) + EPS) * sub
        o_ref[r:r + rb, :] = (y * (1.0 - lam_init)).astype(o_ref.dtype)


def _diff_attention(qkv, cache_k, cache_v, lam_p, subln, *, n_seq, seq_len, heads, lam_init,
                    q_scale, tq):
    hd2 = 2 * LANES
    d = heads * hd2
    qb = seq_len // tq
    in_specs = [
        pl.BlockSpec((tq, hd2), lambda b, h, i: (b * qb + i, h)),
        pl.BlockSpec((seq_len, hd2), lambda b, h, i: (b, heads + h)),
        pl.BlockSpec((seq_len, hd2), lambda b, h, i: (b, 2 * heads + h)),
    ]
    args = [qkv, qkv, qkv]
    if cache_k is not None:
        past = cache_k.shape[1]
        in_specs += [pl.BlockSpec((None, past, hd2), lambda b, h, i: (b, 0, h))] * 2
        args += [cache_k, cache_v]
    in_specs += [pl.BlockSpec(lam_p.shape, lambda b, h, i: (0, 0)),
                 pl.BlockSpec((1, hd2), lambda b, h, i: (0, 0))]
    args += [lam_p, subln.reshape(1, hd2)]
    kern = functools.partial(_attn_kernel, has_cache=cache_k is not None, lam_init=lam_init,
                             q_scale=q_scale)
    return pl.pallas_call(
        kern,
        grid=(n_seq, heads, qb),
        in_specs=in_specs,
        out_specs=pl.BlockSpec((tq, hd2), lambda b, h, i: (b * qb + i, h)),
        out_shape=jax.ShapeDtypeStruct((n_seq * seq_len, d), BF16),
        compiler_params=_params("arbitrary", "arbitrary", "arbitrary"),
        name="diff_attention",
    )(*args)


def _out_proj_kernel(*refs, seg_tiles):
    nseg = len(seg_tiles)
    a_refs = refs[:nseg]
    w_ref, x_ref, gate_ref, o_ref = refs[nseg:]
    i = pl.program_id(0)
    lo = 0
    for a_ref, cnt in zip(a_refs, seg_tiles):
        @pl.when((i >= lo) & (i < lo + cnt))
        def _(a_ref=a_ref):
            y = jnp.dot(a_ref[...], w_ref[...], preferred_element_type=F32)
            o_ref[...] = x_ref[...] + gate_ref[...] * y
        lo += cnt


def _out_project(a_segs, w, x, mod, gate_chunk, *, p_rows, s_len, tm=512, tn=1024):
    t, d = x.shape
    kdim = w.shape[0]
    tn = _tile(d, tn)
    seg_tiles = tuple(a.shape[0] // tm for a in a_segs)
    in_specs = []
    lo = 0
    for cnt in seg_tiles:
        in_specs.append(pl.BlockSpec(
            (tm, kdim), lambda i, j, lo=lo, cnt=cnt: (jnp.clip(i - lo, 0, cnt - 1), 0)))
        lo += cnt
    in_specs += [
        pl.BlockSpec((kdim, tn), lambda i, j: (0, j)),
        pl.BlockSpec((tm, tn), lambda i, j: (i, j)),
        pl.BlockSpec((None, 1, tn),
                     lambda i, j: (_seq_of_row(i * tm, p_rows, s_len), 0,
                                   gate_chunk * (d // tn) + j)),
    ]
    return pl.pallas_call(
        functools.partial(_out_proj_kernel, seg_tiles=seg_tiles),
        grid=(t // tm, d // tn),
        in_specs=in_specs,
        out_specs=pl.BlockSpec((tm, tn), lambda i, j: (i, j)),
        out_shape=jax.ShapeDtypeStruct((t, d), F32),
        compiler_params=_params("arbitrary", "arbitrary"),
        name="out_project_residual",
    )(*a_segs, w, x, mod)


def _ffn_kernel(x_ref, xp_ref, xn_ref, g_ref, sh_ref, sc_ref, gate_ref, wg_ref, wv_ref,
                cwg_ref, cwv_ref, cbg_ref, cbv_ref, wd_ref, o_ref, h_ref, halo_ref, acc_ref,
                *, tm, nj, p_rows, p_len, s_len):
    i = pl.program_id(0)
    j = pl.program_id(1)
    hs = xp_ref.shape[0]

    @pl.when(j == 0)
    def _():
        g, sh, sc = g_ref[...], sh_ref[...], sc_ref[...]
        h_ref[...] = _norm_mod(x_ref[...], g, sh, sc).astype(BF16)
        halo_ref[0:hs, :] = _norm_mod(xp_ref[...], g, sh, sc).astype(BF16)
        halo_ref[hs:2 * hs, :] = _norm_mod(xn_ref[...], g, sh, sc).astype(BF16)
        acc_ref[...] = jnp.zeros_like(acc_ref)

    local = lax.broadcasted_iota(jnp.int32, (tm, 1), 0)
    row = i * tm + local
    pos = jnp.where(row < p_rows, lax.rem(row, p_len), lax.rem(jnp.maximum(row - p_rows, 0), s_len))
    length = jnp.where(row < p_rows, p_len, s_len)
    seq_first = pos == 0
    seq_last = pos == length - 1

    tn = wg_ref.shape[1]
    sub = min(tn, MXU_COLS)
    h = h_ref[...]
    halo = halo_ref[...]
    halo_u = [(jnp.dot(halo, wg_ref[...], preferred_element_type=F32),
               jnp.dot(halo, wv_ref[...], preferred_element_type=F32))]
    main_u = []
    for c in range(0, tn, sub):
        main_u.append((jnp.dot(h, wg_ref[:, c:c + sub], preferred_element_type=F32),
                       jnp.dot(h, wv_ref[:, c:c + sub], preferred_element_type=F32)))

    def conv(u, uh, cw, cb):
        up = jnp.where(local == 0, uh[hs - 1:hs, :], pltpu.roll(u, 1, 0))
        up = jnp.where(seq_first, 0.0, up)
        un = jnp.where(local == tm - 1, uh[hs:hs + 1, :], pltpu.roll(u, tm - 1, 0))
        un = jnp.where(seq_last, 0.0, un)
        return up * cw[0:1] + u * cw[1:2] + un * cw[2:3] + cb

    for ci, (ug, uv) in enumerate(main_u):
        cols = slice(ci * sub, (ci + 1) * sub)
        gate = conv(ug, halo_u[0][0][:, cols], cwg_ref[:, cols], cbg_ref[:, cols])
        val = conv(uv, halo_u[0][1][:, cols], cwv_ref[:, cols], cbv_ref[:, cols])
        act = (_silu(gate) * val).astype(BF16)
        acc_ref[...] += jnp.dot(act, wd_ref[cols, :], preferred_element_type=F32)

    @pl.when(j == nj - 1)
    def _():
        o_ref[...] = x_ref[...] + gate_ref[...] * acc_ref[...]


def _ffn(x, g, mod, shift_chunk, scale_chunk, gate_chunk, w_up, cw, cb, w_down, *,
         p_rows, p_len, s_len, tm=512, tn=512):
    t, d = x.shape
    ff = w_up.shape[1] // 2
    tn = _tile(ff, tn)
    nj = ff // tn
    hs = 16
    per = tm // hs
    last_blk = t // hs - 1
    in_specs = [
        pl.BlockSpec((tm, d), lambda i, j: (i, 0)),
        pl.BlockSpec((hs, d), lambda i, j: (jnp.maximum(i * per - 1, 0), 0)),
        pl.BlockSpec((hs, d), lambda i, j: (jnp.minimum((i + 1) * per, last_blk), 0)),
        pl.BlockSpec((1, d), lambda i, j: (0, 0)),
        _mod_spec(shift_chunk, d, tm, p_rows, s_len),
        _mod_spec(scale_chunk, d, tm, p_rows, s_len),
        _mod_spec(gate_chunk, d, tm, p_rows, s_len),
        pl.BlockSpec((d, tn), lambda i, j: (0, j)),
        pl.BlockSpec((d, tn), lambda i, j: (0, j + nj)),
        pl.BlockSpec((cw.shape[0], tn), lambda i, j: (0, j)),
        pl.BlockSpec((cw.shape[0], tn), lambda i, j: (0, j + nj)),
        pl.BlockSpec((1, tn), lambda i, j: (0, j)),
        pl.BlockSpec((1, tn), lambda i, j: (0, j + nj)),
        pl.BlockSpec((tn, d), lambda i, j: (j, 0)),
    ]
    cb2 = cb.reshape(1, 2 * ff)
    kern = functools.partial(_ffn_kernel, tm=tm, nj=nj, p_rows=p_rows, p_len=p_len,
                             s_len=s_len)
    return pl.pallas_call(
        kern,
        grid=(t // tm, nj),
        in_specs=in_specs,
        out_specs=pl.BlockSpec((tm, d), lambda i, j: (i, 0)),
        out_shape=jax.ShapeDtypeStruct((t, d), F32),
        scratch_shapes=[pltpu.VMEM((tm, d), BF16), pltpu.VMEM((2 * hs, d), BF16),
                        pltpu.VMEM((tm, d), F32)],
        compiler_params=_params("arbitrary", "arbitrary"),
        name="conv_ffn",
    )(x, x, x, g.reshape(1, d), mod, mod, mod, w_up, w_up, cw, cw, cb2, cb2, w_down)


def _final_norm_kernel(x_ref, g_ref, o_ref):
    x = x_ref[...]
    o_ref[...] = x * lax.rsqrt(jnp.mean(x * x, axis=-1, keepdims=True) + EPS) * g_ref[...]


def _final_norm(x, g, *, row_off, rows, tm=512):
    d = x.shape[1]
    off = row_off // tm
    return pl.pallas_call(
        _final_norm_kernel,
        grid=(rows // tm,),
        in_specs=[pl.BlockSpec((tm, d), lambda i: (i + off, 0)),
                  pl.BlockSpec((1, d), lambda i: (0, 0))],
        out_specs=pl.BlockSpec((tm, d), lambda i: (i, 0)),
        out_shape=jax.ShapeDtypeStruct((rows, d), F32),
        compiler_params=_params("arbitrary"),
        name="final_norm",
    )(x, g.reshape(1, d))


def _rope_tables(seq_len, hd):
    pos = jnp.arange(seq_len)
    row = (pos // GRID_W).astype(F32)
    col = (pos % GRID_W).astype(F32)
    nf = hd // 4
    inv = ROPE_BASE ** (-jnp.arange(nf, dtype=F32) / nf)
    ar = row[:, None] * inv
    ac = col[:, None] * inv
    cos = jnp.concatenate([jnp.cos(ar), jnp.cos(ar), jnp.cos(ac), jnp.cos(ac)], axis=1)
    sin = jnp.concatenate([-jnp.sin(ar), jnp.sin(ar), -jnp.sin(ac), jnp.sin(ac)], axis=1)
    return cos, sin


def kernel(x_prompt, x_sample, c, cache_hgrn_state, cache_attn_k, cache_attn_v, c_ctx, w_mod, b_mod, norm_mix, norm_ffn, w_hgrn_in, hgrn_lb_logits, hgrn_onorm, w_hgrn_out, w_attn_in, attn_lambda, attn_subln, w_attn_out, w_ffn_up, ffn_conv_w, ffn_conv_b, w_ffn_down, norm_final):
    batch, p_len, d = x_prompt.shape
    dec_batch, s_len, _ = x_sample.shape
    depth = w_mod.shape[0]
    p_rows = batch * p_len
    s_rows = dec_batch * s_len
    hgrn_heads = d // LANES
    diff_heads = d // (2 * LANES)
    hd = LANES
    assert 1 + dec_batch <= MOD_ROWS and p_rows % s_len == 0

    x = jnp.concatenate([x_prompt.reshape(p_rows, d), x_sample.reshape(s_rows, d)], axis=0)
    cvec = jnp.concatenate(
        [c_ctx[None, :], c, jnp.zeros((MOD_ROWS - 1 - dec_batch, d), F32)], axis=0)
    mods = _modulation(cvec, w_mod, b_mod)
    seg = dict(p_rows=p_rows, s_len=s_len)

    hgrn_states, attn_ks, attn_vs = [], [], []
    for l in range(depth):
        mod = mods[l][:, None, :]
        j = l // N_MIXERS
        if l % N_MIXERS == 0:
            proj = _project(x, norm_mix[l], mod, 0, 1, w_hgrn_in[j].astype(BF16),
                            row_off=0, rows=p_rows + s_rows, out_dtype=F32, tm=1024, tn=1024,
                            **seg)
            mix_p, st = _hgrn_scan(proj, hgrn_lb_logits, hgrn_onorm[j], None, layer_j=j,
                                   n_seq=batch, seq_len=p_len, row_off=0, heads=hgrn_heads,
                                   emit_state=True)
            hgrn_states.append(st)
            mix_s, = _hgrn_scan(proj, hgrn_lb_logits, hgrn_onorm[j], cache_hgrn_state,
                                layer_j=j, n_seq=dec_batch, seq_len=s_len, row_off=p_rows,
                                heads=hgrn_heads, emit_state=False)
            w_out = w_hgrn_out[j]
        else:
            lam_init = 0.8 - 0.6 * math.exp(-0.3 * l)
            q_scale = hd ** -0.5 * LOG2_E
            w_in = w_attn_in[j].astype(BF16)
            qkv_p = _project(x, norm_mix[l], mod, 0, 1, w_in, row_off=0, rows=p_rows,
                             out_dtype=F32, **seg)
            attn_ks.append(qkv_p[:, d:2 * d].reshape(batch, p_len, 2 * diff_heads, hd))
            attn_vs.append(qkv_p[:, 2 * d:].reshape(batch, p_len, diff_heads, 2 * hd))
            mix_p = _diff_attention(qkv_p, None, None, attn_lambda[j], attn_subln[j],
                                    n_seq=batch, seq_len=p_len, heads=diff_heads,
                                    lam_init=lam_init, q_scale=q_scale, tq=p_len)
            qkv_s = _project(x, norm_mix[l], mod, 0, 1, w_in, row_off=p_rows, rows=s_rows,
                             out_dtype=BF16, rope_tables=_rope_tables(s_len, hd),
                             q_scale=q_scale, tn=1024, **seg)
            past = cache_attn_k.shape[2]
            mix_s = _diff_attention(qkv_s, cache_attn_k[:, j].reshape(dec_batch, past, d),
                                    cache_attn_v[:, j].reshape(dec_batch, past, d),
                                    attn_lambda[j], attn_subln[j], n_seq=dec_batch,
                                    seq_len=s_len, heads=diff_heads, lam_init=lam_init,
                                    q_scale=1.0, tq=256)
            w_out = w_attn_out[j]
        x = _out_project([mix_p, mix_s], w_out.astype(BF16), x, mod, 2, tn=d, **seg)
        x = _ffn(x, norm_ffn[l], mod, 3, 4, 5, w_ffn_up[l].astype(BF16), ffn_conv_w[l],
                 ffn_conv_b[l], w_ffn_down[l].astype(BF16), p_len=p_len, **seg)

    y_prompt = _final_norm(x, norm_final, row_off=0, rows=p_rows).reshape(batch, p_len, d)
    y_sample = _final_norm(x, norm_final, row_off=p_rows, rows=s_rows).reshape(dec_batch, s_len, d)
    new_hgrn_state = jnp.stack(hgrn_states, axis=1)
    new_attn_k = jnp.stack(attn_ks, axis=1)
    new_attn_v = jnp.stack(attn_vs, axis=1)
    return (y_prompt, y_sample, new_hgrn_state, new_attn_k, new_attn_v)
```

```python
import functools
import math

import jax
import jax.numpy as jnp
from jax import lax
from jax.experimental import pallas as pl
from jax.experimental.pallas import tpu as pltpu

F32 = jnp.float32
BF16 = jnp.bfloat16

EPS = 1e-6
GRID_W = 64
ROPE_BASE = 10000.0
N_MIXERS = 2

LANES = 128
MXU_COLS = 256
MOD_ROWS = 16
HGRN_CHUNK = 64
HGRN_SUB = 16
HGRN_GROUP = 8
ATTN_ROW_BLOCK = 128
FFN_TILE = 256
LOG2_E = 1.4426950408889634
VMEM_LIMIT = 56 * 1024 * 1024

NT_DIMS = (((1,), (1,)), ((), ()))
TN_DIMS = (((0,), (0,)), ((), ()))


def _params(*sem):
    return pltpu.CompilerParams(dimension_semantics=sem, vmem_limit_bytes=VMEM_LIMIT)


def _tile(n, want):
    best = LANES
    for cand in range(LANES, min(n, want) + 1, LANES):
        if n % cand == 0:
            best = cand
    assert n % best == 0
    return best


def _silu(x):
    return x * jax.nn.sigmoid(x)


def _split_bf16(x):
    hi = x.astype(BF16)
    lo = (x - hi.astype(F32)).astype(BF16)
    return hi, lo


def _mod_kernel(c_ref, w_ref, b_ref, o_ref):
    a_hi, a_lo = _split_bf16(_silu(c_ref[...]))
    w_hi, w_lo = _split_bf16(w_ref[...])
    acc = jnp.dot(a_hi, w_hi, preferred_element_type=F32)
    acc += jnp.dot(a_lo, w_hi, preferred_element_type=F32)
    acc += jnp.dot(a_hi, w_lo, preferred_element_type=F32)
    o_ref[...] = acc + b_ref[...]


def _modulation(cvec, w_mod, b_mod, tn=512):
    depth, d, n = w_mod.shape
    tn = _tile(n, tn)
    return pl.pallas_call(
        _mod_kernel,
        grid=(depth, n // tn),
        in_specs=[
            pl.BlockSpec((MOD_ROWS, d), lambda l, j: (0, 0)),
            pl.BlockSpec((None, d, tn), lambda l, j: (l, 0, j)),
            pl.BlockSpec((None, 1, tn), lambda l, j: (l, 0, j)),
        ],
        out_specs=pl.BlockSpec((None, MOD_ROWS, tn), lambda l, j: (l, 0, j)),
        out_shape=jax.ShapeDtypeStruct((depth, MOD_ROWS, n), F32),
        compiler_params=_params("arbitrary", "arbitrary"),
        name="modulation",
    )(cvec, w_mod, b_mod.reshape(depth, 1, n))


def _norm_mod(x, g, shift, scale):
    y = x * lax.rsqrt(jnp.mean(x * x, axis=-1, keepdims=True) + EPS) * g
    return y * (1.0 + scale) + shift


def _seq_of_row(row0, p_rows, s_len):
    return jnp.where(row0 < p_rows, 0, 1 + (row0 - p_rows) // s_len)


def _mod_spec(chunk, d, tm, p_rows, s_len, tile_off=0):
    return pl.BlockSpec(
        (None, 1, d),
        lambda i, j: (_seq_of_row((i + tile_off) * tm, p_rows, s_len), 0, chunk))


def _proj_kernel(x_ref, g_ref, sh_ref, sc_ref, w_ref, *rest, rope, q_tiles, qk_tiles, q_scale):
    if rope:
        cos_ref, sin_ref, o_ref, h_ref = rest
    else:
        o_ref, h_ref = rest
    j = pl.program_id(1)

    @pl.when(j == 0)
    def _():
        h_ref[...] = _norm_mod(x_ref[...], g_ref[...], sh_ref[...], sc_ref[...]).astype(BF16)

    if not rope:
        o_ref[...] = jnp.dot(h_ref[...], w_ref[...],
                             preferred_element_type=F32).astype(o_ref.dtype)
        return

    @pl.when(j < qk_tiles)
    def _():
        tn = o_ref.shape[1]
        sub = min(tn, MXU_COLS)
        h = h_ref[...]
        accs = [jnp.dot(h, w_ref[:, c:c + sub], preferred_element_type=F32)
                for c in range(0, tn, sub)]
        scale = jnp.where(j < q_tiles, q_scale, 1.0)
        cos = cos_ref[...] * scale
        sin = sin_ref[...] * scale
        lane = lax.broadcasted_iota(jnp.int32, cos.shape, 1)
        first_of_pair = (lane // (LANES // 4)) % 2 == 0
        for ci, acc in enumerate(accs):
            for s in range(sub // LANES):
                xs = acc[:, s * LANES:(s + 1) * LANES]
                partner = jnp.where(first_of_pair,
                                    pltpu.roll(xs, LANES - LANES // 4, 1),
                                    pltpu.roll(xs, LANES // 4, 1))
                col = ci * sub + s * LANES
                o_ref[:, col:col + LANES] = (xs * cos + partner * sin).astype(o_ref.dtype)

    @pl.when(j >= qk_tiles)
    def _():
        o_ref[...] = jnp.dot(h_ref[...], w_ref[...],
                             preferred_element_type=F32).astype(o_ref.dtype)


def _project(x, g, mod, shift_chunk, scale_chunk, w, *, row_off, rows, out_dtype,
             p_rows, s_len, tm=512, tn=512, rope_tables=None, q_scale=1.0):
    t, d = x.shape
    n = w.shape[1]
    tm = math.gcd(tm, p_rows, s_len)
    tile_off = row_off // tm
    rope = rope_tables is not None
    tn = _tile(n // 3 if rope else n, tn)
    in_specs = [
        pl.BlockSpec((tm, d), lambda i, j: (i + tile_off, 0)),
        pl.BlockSpec((1, d), lambda i, j: (0, 0)),
        _mod_spec(shift_chunk, d, tm, p_rows, s_len, tile_off),
        _mod_spec(scale_chunk, d, tm, p_rows, s_len, tile_off),
        pl.BlockSpec((d, tn), lambda i, j: (0, j)),
    ]
    args = [x, g.reshape(1, d), mod, mod, w]
    q_tiles = qk_tiles = 0
    if rope:
        cos, sin = rope_tables
        tiles_per_seq = s_len // tm
        in_specs += [pl.BlockSpec((tm, LANES), lambda i, j: (i % tiles_per_seq, 0))] * 2
        args += [cos, sin]
        q_tiles = (n // 3) // tn
        qk_tiles = 2 * q_tiles
    kern = functools.partial(_proj_kernel, rope=rope, q_tiles=q_tiles, qk_tiles=qk_tiles,
                             q_scale=q_scale)
    return pl.pallas_call(
        kern,
        grid=(rows // tm, n // tn),
        in_specs=in_specs,
        out_specs=pl.BlockSpec((tm, tn), lambda i, j: (i, j)),
        out_shape=jax.ShapeDtypeStruct((rows, n), out_dtype),
        scratch_shapes=[pltpu.VMEM((tm, d), BF16)],
        compiler_params=_params("arbitrary", "arbitrary"),
        name="norm_mod_project",
    )(*args)


def _hgrn_group_local(groups):
    c, m = HGRN_CHUNK, HGRN_SUB
    nb = c // m
    t = lax.broadcasted_iota(jnp.int32, (c, c), 0)
    s = lax.broadcasted_iota(jnp.int32, (c, c), 1)

    jobs = []
    for q, z, v, lb, direction in groups:
        dk = q.shape[1]
        tri = (s <= t) if direction == 0 else (s >= t)
        cum = jnp.where(tri, 1.0, 0.0).astype(BF16)
        f = lb + (1.0 - lb) * jax.nn.sigmoid(z)
        g2 = jnp.log(f) * LOG2_E
        k = 1.0 - f
        g_hi = g2.astype(BF16)
        g_lo = (g2 - g_hi.astype(F32)).astype(BF16)
        v_bf = v.astype(BF16)
        for ci in range(q.shape[0] // c):
            rows = slice(ci * c, (ci + 1) * c)
            jobs.append(dict(direction=direction, tri=tri, cum=cum, q=q[rows], k=k[rows],
                             v=v_bf[rows], g=jnp.concatenate([g_hi[rows], g_lo[rows]], axis=1)))

    def padded(x, lo, hi):
        parts = [jnp.zeros((lo, dk), BF16)] if lo else []
        parts.append(x)
        if hi < c:
            parts.append(jnp.zeros((c - hi, dk), BF16))
        return jnp.concatenate(parts, axis=0) if len(parts) > 1 else x

    for job in jobs:
        job["sums"] = jnp.dot(job["cum"], job["g"], preferred_element_type=F32)

    for job in jobs:
        direction = job["direction"]
        sums = job["sums"]
        b = sums[:, :dk] + sums[:, dk:]
        anchors = []
        for i in range(nb):
            a_row = i * m + (m // 2 - 1 if direction == 0 else m // 2)
            anchors.append(b[a_row:a_row + 1, :])
        b_anchor = jnp.concatenate([jnp.broadcast_to(r, (m, dk)) for r in anchors], axis=0)
        q_rel = (job["q"] * jnp.exp2(b - b_anchor)).astype(BF16)
        full = nb - 1 if direction == 0 else 0
        q_cat, k_cat = [], []
        k_full = None
        for i in range(nb):
            lo, hi = (0, m * (i + 1)) if direction == 0 else (m * i, c)
            ki = job["k"][lo:hi] * jnp.exp2(anchors[i] - b[lo:hi])
            if i == full:
                k_full = ki
            k_cat.append(padded(ki.astype(BF16), lo, hi))
            q_cat.append(padded(q_rel[m * i:m * (i + 1)], m * i, m * (i + 1)))
        job["a"] = lax.dot_general(jnp.concatenate(q_cat, axis=1),
                                   jnp.concatenate(k_cat, axis=1), NT_DIMS,
                                   preferred_element_type=F32)
        edge = c - 1 if direction == 0 else 0
        total = b[edge:edge + 1, :]
        job["k_dec"] = (k_full * jnp.exp2(total - anchors[full])).astype(BF16)
        job["q_in"] = (job["q"] * jnp.exp2(b)).astype(BF16)
        job["decay"] = jnp.exp2(total)

    for job in jobs:
        job["u_t"] = lax.dot_general(job["v"], job["k_dec"], TN_DIMS,
                                     preferred_element_type=F32)

    out, pos = [], 0
    for q, *_ in groups:
        n = q.shape[0] // c
        out.append([(jnp.where(j["tri"], j["a"], 0.0).astype(BF16), j["v"], j["q_in"], j["u_t"],
                     j["decay"]) for j in jobs[pos:pos + n]])
        pos += n
    return out


def _hgrn_kernel(*refs, seq_len, layer_j, n_lb, has_s0, emit_state):
    q_ref, zf_ref, zb_ref, i_ref, g_ref, lg_ref, on_ref = refs[:7]
    pos = 7
    s0_ref = st_ref = None
    if has_s0:
        s0_ref = refs[pos]
        pos += 1
    o_ref = refs[pos]
    pos += 1
    if emit_state:
        st_ref = refs[pos]
        pos += 1
    of_scr, ob_scr = refs[pos], refs[pos + 1]

    grp = math.gcd(HGRN_GROUP, seq_len // HGRN_CHUNK)
    rows = grp * HGRN_CHUNK
    n = seq_len // rows
    dv = q_ref.shape[1]

    lbs = []
    for d in range(2):
        lg = lg_ref[d * n_lb:(d + 1) * n_lb, :]
        e = jnp.exp(lg - jnp.max(lg, axis=0, keepdims=True))
        lbs.append(jnp.sum(e[:layer_j + 1], axis=0, keepdims=True) / jnp.sum(e, axis=0, keepdims=True))

    if has_s0:
        st0 = (s0_ref[0].T, s0_ref[1].T)
    else:
        st0 = (jnp.zeros((dv, dv), F32), jnp.zeros((dv, dv), F32))

    def advance(local, st):
        _, _, q_in, u_t, decay = local
        o = lax.dot_general(q_in, st.astype(BF16), NT_DIMS, preferred_element_type=F32)
        return st * decay + u_t, o

    def add_intra(local, o_inter):
        a, v = local[:2]
        return o_inter + jnp.dot(a, v, preferred_element_type=F32)

    def body(gi, carry):
        st_f, st_b = carry
        rf = pl.ds(pl.multiple_of(gi * rows, rows), rows)
        rb = pl.ds(pl.multiple_of((n - 1 - gi) * rows, rows), rows)
        loc_f, loc_b = _hgrn_group_local([
            (q_ref[rf, :], zf_ref[rf, :], i_ref[rf, :], lbs[0], 0),
            (q_ref[rb, :], zb_ref[rb, :], i_ref[rb, :], lbs[1], 1)])
        o_f, o_b = [None] * grp, [None] * grp
        for ci in range(grp):
            st_f, o_f[ci] = advance(loc_f[ci], st_f)
            st_b, o_b[grp - 1 - ci] = advance(loc_b[grp - 1 - ci], st_b)
        o_f = [add_intra(l, o) for l, o in zip(loc_f, o_f)]
        o_b = [add_intra(l, o) for l, o in zip(loc_b, o_b)]
        of_scr[rf, :] = jnp.concatenate(o_f, axis=0)
        ob_scr[rb, :] = jnp.concatenate(o_b, axis=0)
        return st_f, st_b

    st_f, st_b = lax.fori_loop(0, n, body, st0)
    if emit_state:
        st_ref[0] = st_f.T
        st_ref[1] = st_b.T

    fin_rows = math.gcd(seq_len, 256)
    onorm = on_ref[...]

    def finish(ri, _):
        r = pl.ds(pl.multiple_of(ri * fin_rows, fin_rows), fin_rows)
        o = of_scr[r, :] + ob_scr[r, :]
        y = o * lax.rsqrt(jnp.mean(o * o, axis=-1, keepdims=True) + EPS) * onorm
        o_ref[r, :] = (y * _silu(g_ref[r, :])).astype(o_ref.dtype)
        return 0

    lax.fori_loop(0, seq_len // fin_rows, finish, 0)


def _hgrn_scan(proj, lb_logits, onorm, s0, *, layer_j, n_seq, seq_len, row_off, heads, emit_state):
    dk = LANES
    d = heads * dk
    blk_off = row_off // seq_len
    n_lb = lb_logits.shape[1]

    def sec(k):
        return pl.BlockSpec((seq_len, dk), lambda b, h: (b + blk_off, k * heads + h))

    in_specs = [sec(0), sec(1), sec(2), sec(3), sec(4),
                pl.BlockSpec((2 * n_lb, dk), lambda b, h: (0, h)),
                pl.BlockSpec((1, dk), lambda b, h: (0, 0))]
    args = [proj] * 5 + [lb_logits.reshape(2 * n_lb, d), onorm.reshape(1, dk)]
    if s0 is not None:
        in_specs.append(pl.BlockSpec((None, None, 2, None, dk, dk),
                                     lambda b, h: (b, layer_j, 0, h, 0, 0)))
        args.append(s0)
    out_shape = [jax.ShapeDtypeStruct((n_seq * seq_len, d), BF16)]
    out_specs = [pl.BlockSpec((seq_len, dk), lambda b, h: (b, h))]
    if emit_state:
        out_shape.append(jax.ShapeDtypeStruct((n_seq, 2, heads, dk, dk), F32))
        out_specs.append(pl.BlockSpec((None, 2, None, dk, dk), lambda b, h: (b, 0, h, 0, 0)))
    kern = functools.partial(_hgrn_kernel, seq_len=seq_len, layer_j=layer_j, n_lb=n_lb,
                             has_s0=s0 is not None, emit_state=emit_state)
    return pl.pallas_call(
        kern,
        grid=(n_seq, heads),
        in_specs=in_specs,
        out_specs=out_specs,
        out_shape=out_shape,
        scratch_shapes=[pltpu.VMEM((seq_len, dk), F32), pltpu.VMEM((seq_len, dk), F32)],
        compiler_params=_params("arbitrary", "arbitrary"),
        name="hgrn_scan",
    )(*args)


def _attn_kernel(*refs, has_cache, lam_init, q_scale):
    q_ref, k_ref, v_ref = refs[:3]
    pos = 3
    kc_ref = vc_ref = None
    if has_cache:
        kc_ref, vc_ref = refs[3], refs[4]
        pos = 5
    lam_ref, sub_ref, o_ref = refs[pos:pos + 3]

    lp = lam_ref[...]
    lam = (jnp.exp(jnp.sum(lp[0:1] * lp[1:2], axis=-1, keepdims=True))
           - jnp.exp(jnp.sum(lp[2:3] * lp[3:4], axis=-1, keepdims=True)) + lam_init)

    hd = LANES
    v_new = v_ref[...].astype(BF16)
    v_old = vc_ref[...].astype(BF16) if has_cache else None
    tq = q_ref.shape[0]
    rb = min(tq, ATTN_ROW_BLOCK)
    chains = [(r, j) for r in range(0, tq, rb) for j in range(2)]
    scores = []
    for r, j in chains:
        cols = slice(j * hd, (j + 1) * hd)
        qj = q_ref[r:r + rb, cols]
        if q_scale != 1.0:
            qj = qj.astype(F32) * q_scale
        qj = qj.astype(BF16)
        s_new = lax.dot_general(qj, k_ref[:, cols].astype(BF16), NT_DIMS,
                                preferred_element_type=F32)
        s_old = None
        if has_cache:
            s_old = lax.dot_general(qj, kc_ref[:, cols].astype(BF16), NT_DIMS,
                                    preferred_element_type=F32)
        scores.append((s_new, s_old))
    outs = {}
    for (r, j), (s_new, s_old) in zip(chains, scores):
        mx = jnp.max(s_new, axis=-1, keepdims=True)
        if has_cache:
            mx = jnp.maximum(mx, jnp.max(s_old, axis=-1, keepdims=True))
        p_new = jnp.exp2(s_new - mx)
        den = jnp.sum(p_new, axis=-1, keepdims=True)
        acc = jnp.dot(p_new.astype(BF16), v_new, preferred_element_type=F32)
        if has_cache:
            p_old = jnp.exp2(s_old - mx)
            den += jnp.sum(p_old, axis=-1, keepdims=True)
            acc += jnp.dot(p_old.astype(BF16), v_old, preferred_element_type=F32)
        outs[r, j] = acc / den
    sub = sub_ref[...]
    for r in range(0, tq, rb):
        o = outs[r, 0] - lam * outs[r, 1]
        y = o * lax.rsqrt(jnp.mean(o * o, axis=-1, keepdims=True) + EPS) * sub
        o_ref[r:r + rb, :] = (y * (1.0 - lam_init)).astype(o_ref.dtype)


def _diff_attention(qkv, cache_k, cache_v, lam_p, subln, *, n_seq, seq_len, heads, lam_init,
                    q_scale, tq):
    hd2 = 2 * LANES
    d = heads * hd2
    qb = seq_len // tq
    in_specs = [
        pl.BlockSpec((tq, hd2), lambda b, h, i: (b * qb + i, h)),
        pl.BlockSpec((seq_len, hd2), lambda b, h, i: (b, heads + h)),
        pl.BlockSpec((seq_len, hd2), lambda b, h, i: (b, 2 * heads + h)),
    ]
    args = [qkv, qkv, qkv]
    if cache_k is not None:
        past = cache_k.shape[1]
        in_specs += [pl.BlockSpec((None, past, hd2), lambda b, h, i: (b, 0, h))] * 2
        args += [cache_k, cache_v]
    in_specs += [pl.BlockSpec(lam_p.shape, lambda b, h, i: (0, 0)),
                 pl.BlockSpec((1, hd2), lambda b, h, i: (0, 0))]
    args += [lam_p, subln.reshape(1, hd2)]
    kern = functools.partial(_attn_kernel, has_cache=cache_k is not None, lam_init=lam_init,
                             q_scale=q_scale)
    return pl.pallas_call(
        kern,
        grid=(n_seq, heads, qb),
        in_specs=in_specs,
        out_specs=pl.BlockSpec((tq, hd2), lambda b, h, i: (b * qb + i, h)),
        out_shape=jax.ShapeDtypeStruct((n_seq * seq_len, d), BF16),
        compiler_params=_params("arbitrary", "arbitrary", "arbitrary"),
        name="diff_attention",
    )(*args)


def _out_proj_kernel(*refs, seg_tiles):
    nseg = len(seg_tiles)
    a_refs = refs[:nseg]
    w_ref, x_ref, gate_ref, o_ref = refs[nseg:]
    i = pl.program_id(0)
    lo = 0
    for a_ref, cnt in zip(a_refs, seg_tiles):
        @pl.when((i >= lo) & (i < lo + cnt))
        def _(a_ref=a_ref):
            y = jnp.dot(a_ref[...], w_ref[...], preferred_element_type=F32)
            o_ref[...] = x_ref[...] + gate_ref[...] * y
        lo += cnt


def _out_project(a_segs, w, x, mod, gate_chunk, *, p_rows, s_len, tm=512, tn=1024):
    t, d = x.shape
    kdim = w.shape[0]
    tn = _tile(d, tn)
    seg_tiles = tuple(a.shape[0] // tm for a in a_segs)
    in_specs = []
    lo = 0
    for cnt in seg_tiles:
        in_specs.append(pl.BlockSpec(
            (tm, kdim), lambda i, j, lo=lo, cnt=cnt: (jnp.clip(i - lo, 0, cnt - 1), 0)))
        lo += cnt
    in_specs += [
        pl.BlockSpec((kdim, tn), lambda i, j: (0, j)),
        pl.BlockSpec((tm, tn), lambda i, j: (i, j)),
        pl.BlockSpec((None, 1, tn),
                     lambda i, j: (_seq_of_row(i * tm, p_rows, s_len), 0,
                                   gate_chunk * (d // tn) + j)),
    ]
    return pl.pallas_call(
        functools.partial(_out_proj_kernel, seg_tiles=seg_tiles),
        grid=(t // tm, d // tn),
        in_specs=in_specs,
        out_specs=pl.BlockSpec((tm, tn), lambda i, j: (i, j)),
        out_shape=jax.ShapeDtypeStruct((t, d), F32),
        compiler_params=_params("arbitrary", "arbitrary"),
        name="out_project_residual",
    )(*a_segs, w, x, mod)


def _ffn_kernel(x_ref, xp_ref, xn_ref, g_ref, sh_ref, sc_ref, gate_ref, wg_ref, wv_ref,
                cwg_a, cwv_a, cbg_a, cbv_a, wd_a, cwg_b, cwv_b, cbg_b, cbv_b, wd_b,
                o_ref, h_ref, ug_ref, uv_ref, acc_ref, *, tm, nj, p_rows, p_len, s_len):
    i = pl.program_id(0)
    j = pl.program_id(1)
    hs = xp_ref.shape[0]
    sub = FFN_TILE

    local = lax.broadcasted_iota(jnp.int32, (tm, 1), 0)
    row = i * tm + local
    pos = jnp.where(row < p_rows, lax.rem(row, p_len), lax.rem(jnp.maximum(row - p_rows, 0), s_len))
    length = jnp.where(row < p_rows, p_len, s_len)
    seq_first = pos == 0
    seq_last = pos == length - 1

    def up(cols):
        h = h_ref[...]
        return (jnp.dot(h, wg_ref[:, cols], preferred_element_type=F32),
                jnp.dot(h, wv_ref[:, cols], preferred_element_type=F32))

    def conv(u_all, cw, cb):
        u = u_all[:tm]
        up_row = u_all[tm + hs - 1:tm + hs, :]
        next_row = u_all[tm + hs:tm + hs + 1, :]
        prev = jnp.where(local == 0, up_row, pltpu.roll(u, 1, 0))
        prev = jnp.where(seq_first, 0.0, prev)
        nxt = jnp.where(local == tm - 1, next_row, pltpu.roll(u, tm - 1, 0))
        nxt = jnp.where(seq_last, 0.0, nxt)
        return prev * cw[0:1] + u * cw[1:2] + nxt * cw[2:3] + cb

    def activate(ug, uv, cwg, cwv, cbg, cbv):
        gate = conv(ug, cwg[...], cbg[...])
        val = conv(uv, cwv[...], cbv[...])
        return (_silu(gate) * val).astype(BF16)

    def activate_a(ug, uv):
        return activate(ug, uv, cwg_a, cwv_a, cbg_a, cbv_a)

    def activate_b():
        return activate(ug_ref[...], uv_ref[...], cwg_b, cwv_b, cbg_b, cbv_b)

    def project_down(act, wd):
        acc_ref[...] += jnp.dot(act, wd[...], preferred_element_type=F32)

    def park_b(ug, uv):
        ug_ref[...] = ug
        uv_ref[...] = uv

    cols_a, cols_b = slice(0, sub), slice(sub, 2 * sub)

    @pl.when(j == 0)
    def _():
        g, sh, sc = g_ref[...], sh_ref[...], sc_ref[...]
        h_ref[0:tm, :] = _norm_mod(x_ref[...], g, sh, sc).astype(BF16)
        h_ref[tm:tm + hs, :] = _norm_mod(xp_ref[...], g, sh, sc).astype(BF16)
        h_ref[tm + hs:tm + 2 * hs, :] = _norm_mod(xn_ref[...], g, sh, sc).astype(BF16)
        acc_ref[...] = jnp.zeros_like(acc_ref)
        ua = up(cols_a)
        park_b(*up(cols_b))
        project_down(activate_a(*ua), wd_a)

    @pl.when((j > 0) & (j < nj))
    def _():
        act_b = activate_b()
        ua = up(cols_a)
        project_down(act_b, wd_b)
        ub = up(cols_b)
        act_a = activate_a(*ua)
        park_b(*ub)
        project_down(act_a, wd_a)

    @pl.when(j == nj)
    def _():
        project_down(activate_b(), wd_b)
        o_ref[...] = x_ref[...] + gate_ref[...] * acc_ref[...]


def _ffn(x, g, mod, shift_chunk, scale_chunk, gate_chunk, w_up, cw, cb, w_down, *,
         p_rows, p_len, s_len, tm=512):
    t, d = x.shape
    ff = w_up.shape[1] // 2
    sub = FFN_TILE
    assert ff % (2 * sub) == 0
    nj = ff // (2 * sub)
    nt = ff // sub
    hs = 16
    per = tm // hs
    last_blk = t // hs - 1

    def tile_a(i, j):
        return jnp.minimum(2 * j, nt - 2)

    def tile_b(i, j):
        return jnp.maximum(2 * j - 1, 1)

    def tile_specs(tile):
        return [
            pl.BlockSpec((cw.shape[0], sub), lambda i, j: (0, tile(i, j))),
            pl.BlockSpec((cw.shape[0], sub), lambda i, j: (0, tile(i, j) + nt)),
            pl.BlockSpec((1, sub), lambda i, j: (0, tile(i, j))),
            pl.BlockSpec((1, sub), lambda i, j: (0, tile(i, j) + nt)),
            pl.BlockSpec((sub, d), lambda i, j: (tile(i, j), 0)),
        ]

    in_specs = [
        pl.BlockSpec((tm, d), lambda i, j: (i, 0)),
        pl.BlockSpec((hs, d), lambda i, j: (jnp.maximum(i * per - 1, 0), 0)),
        pl.BlockSpec((hs, d), lambda i, j: (jnp.minimum((i + 1) * per, last_blk), 0)),
        pl.BlockSpec((1, d), lambda i, j: (0, 0)),
        _mod_spec(shift_chunk, d, tm, p_rows, s_len),
        _mod_spec(scale_chunk, d, tm, p_rows, s_len),
        _mod_spec(gate_chunk, d, tm, p_rows, s_len),
        pl.BlockSpec((d, 2 * sub), lambda i, j: (0, jnp.minimum(j, nj - 1))),
        pl.BlockSpec((d, 2 * sub), lambda i, j: (0, jnp.minimum(j, nj - 1) + nj)),
    ] + tile_specs(tile_a) + tile_specs(tile_b)
    cb2 = cb.reshape(1, 2 * ff)
    tile_args = [cw, cw, cb2, cb2, w_down]
    kern = functools.partial(_ffn_kernel, tm=tm, nj=nj, p_rows=p_rows, p_len=p_len,
                             s_len=s_len)
    return pl.pallas_call(
        kern,
        grid=(t // tm, nj + 1),
        in_specs=in_specs,
        out_specs=pl.BlockSpec((tm, d), lambda i, j: (i, 0)),
        out_shape=jax.ShapeDtypeStruct((t, d), F32),
        scratch_shapes=[pltpu.VMEM((tm + 2 * hs, d), BF16),
                        pltpu.VMEM((tm + 2 * hs, sub), F32),
                        pltpu.VMEM((tm + 2 * hs, sub), F32),
                        pltpu.VMEM((tm, d), F32)],
        compiler_params=_params("arbitrary", "arbitrary"),
        name="conv_ffn",
    )(x, x, x, g.reshape(1, d), mod, mod, mod, w_up, w_up, *tile_args, *tile_args)


def _final_norm_kernel(x_ref, g_ref, o_ref):
    x = x_ref[...]
    o_ref[...] = x * lax.rsqrt(jnp.mean(x * x, axis=-1, keepdims=True) + EPS) * g_ref[...]


def _final_norm(x, g, *, row_off, rows, tm=512):
    d = x.shape[1]
    off = row_off // tm
    return pl.pallas_call(
        _final_norm_kernel,
        grid=(rows // tm,),
        in_specs=[pl.BlockSpec((tm, d), lambda i: (i + off, 0)),
                  pl.BlockSpec((1, d), lambda i: (0, 0))],
        out_specs=pl.BlockSpec((tm, d), lambda i: (i, 0)),
        out_shape=jax.ShapeDtypeStruct((rows, d), F32),
        compiler_params=_params("arbitrary"),
        name="final_norm",
    )(x, g.reshape(1, d))


def _rope_tables(seq_len, hd):
    pos = jnp.arange(seq_len)
    row = (pos // GRID_W).astype(F32)
    col = (pos % GRID_W).astype(F32)
    nf = hd // 4
    inv = ROPE_BASE ** (-jnp.arange(nf, dtype=F32) / nf)
    ar = row[:, None] * inv
    ac = col[:, None] * inv
    cos = jnp.concatenate([jnp.cos(ar), jnp.cos(ar), jnp.cos(ac), jnp.cos(ac)], axis=1)
    sin = jnp.concatenate([-jnp.sin(ar), jnp.sin(ar), -jnp.sin(ac), jnp.sin(ac)], axis=1)
    return cos, sin


def kernel(x_prompt, x_sample, c, cache_hgrn_state, cache_attn_k, cache_attn_v, c_ctx, w_mod, b_mod, norm_mix, norm_ffn, w_hgrn_in, hgrn_lb_logits, hgrn_onorm, w_hgrn_out, w_attn_in, attn_lambda, attn_subln, w_attn_out, w_ffn_up, ffn_conv_w, ffn_conv_b, w_ffn_down, norm_final):
    batch, p_len, d = x_prompt.shape
    dec_batch, s_len, _ = x_sample.shape
    depth = w_mod.shape[0]
    p_rows = batch * p_len
    s_rows = dec_batch * s_len
    hgrn_heads = d // LANES
    diff_heads = d // (2 * LANES)
    hd = LANES
    assert 1 + dec_batch <= MOD_ROWS and p_rows % s_len == 0

    x = jnp.concatenate([x_prompt.reshape(p_rows, d), x_sample.reshape(s_rows, d)], axis=0)
    cvec = jnp.concatenate(
        [c_ctx[None, :], c, jnp.zeros((MOD_ROWS - 1 - dec_batch, d), F32)], axis=0)
    mods = _modulation(cvec, w_mod, b_mod)
    seg = dict(p_rows=p_rows, s_len=s_len)

    hgrn_states, attn_ks, attn_vs = [], [], []
    for l in range(depth):
        mod = mods[l][:, None, :]
        j = l // N_MIXERS
        if l % N_MIXERS == 0:
            proj = _project(x, norm_mix[l], mod, 0, 1, w_hgrn_in[j].astype(BF16),
                            row_off=0, rows=p_rows + s_rows, out_dtype=F32, tm=1024, tn=1024,
                            **seg)
            mix_p, st = _hgrn_scan(proj, hgrn_lb_logits, hgrn_onorm[j], None, layer_j=j,
                                   n_seq=batch, seq_len=p_len, row_off=0, heads=hgrn_heads,
                                   emit_state=True)
            hgrn_states.append(st)
            mix_s, = _hgrn_scan(proj, hgrn_lb_logits, hgrn_onorm[j], cache_hgrn_state,
                                layer_j=j, n_seq=dec_batch, seq_len=s_len, row_off=p_rows,
                                heads=hgrn_heads, emit_state=False)
            w_out = w_hgrn_out[j]
        else:
            lam_init = 0.8 - 0.6 * math.exp(-0.3 * l)
            q_scale = hd ** -0.5 * LOG2_E
            w_in = w_attn_in[j].astype(BF16)
            qkv_p = _project(x, norm_mix[l], mod, 0, 1, w_in, row_off=0, rows=p_rows,
                             out_dtype=F32, **seg)
            attn_ks.append(qkv_p[:, d:2 * d].reshape(batch, p_len, 2 * diff_heads, hd))
            attn_vs.append(qkv_p[:, 2 * d:].reshape(batch, p_len, diff_heads, 2 * hd))
            mix_p = _diff_attention(qkv_p, None, None, attn_lambda[j], attn_subln[j],
                                    n_seq=batch, seq_len=p_len, heads=diff_heads,
                                    lam_init=lam_init, q_scale=q_scale, tq=p_len)
            qkv_s = _project(x, norm_mix[l], mod, 0, 1, w_in, row_off=p_rows, rows=s_rows,
                             out_dtype=BF16, rope_tables=_rope_tables(s_len, hd),
                             q_scale=q_scale, tn=1024, **seg)
            past = cache_attn_k.shape[2]
            mix_s = _diff_attention(qkv_s, cache_attn_k[:, j].reshape(dec_batch, past, d),
                                    cache_attn_v[:, j].reshape(dec_batch, past, d),
                                    attn_lambda[j], attn_subln[j], n_seq=dec_batch,
                                    seq_len=s_len, heads=diff_heads, lam_init=lam_init,
                                    q_scale=1.0, tq=512)
            w_out = w_attn_out[j]
        x = _out_project([mix_p, mix_s], w_out.astype(BF16), x, mod, 2, tn=d, **seg)
        x = _ffn(x, norm_ffn[l], mod, 3, 4, 5, w_ffn_up[l].astype(BF16), ffn_conv_w[l],
                 ffn_conv_b[l], w_ffn_down[l].astype(BF16), p_len=p_len, **seg)

    y_prompt = _final_norm(x, norm_final, row_off=0, rows=p_rows).reshape(batch, p_len, d)
    y_sample = _final_norm(x, norm_final, row_off=p_rows, rows=s_rows).reshape(dec_batch, s_len, d)
    new_hgrn_state = jnp.stack(hgrn_states, axis=1)
    new_attn_k = jnp.stack(attn_ks, axis=1)
    new_attn_v = jnp.stack(attn_vs, axis=1)
    return (y_prompt, y_sample, new_hgrn_state, new_attn_k, new_attn_v)
```

```python
import functools
import math

import jax
import jax.numpy as jnp
from jax import lax
from jax.experimental import pallas as pl
from jax.experimental.pallas import tpu as pltpu

F32 = jnp.float32
BF16 = jnp.bfloat16

EPS = 1e-6
GRID_W = 64
ROPE_BASE = 10000.0
N_MIXERS = 2

LANES = 128
SUBLANES = 8
MXU_COLS = 256
MOD_ROWS = 16
HGRN_CHUNK = 64
HGRN_SUB = 16
HGRN_GROUP = 8
ATTN_ROW_BLOCK = 128
FFN_TILE = 256
LOG2_E = 1.4426950408889634
VMEM_LIMIT = 56 * 1024 * 1024

NT_DIMS = (((1,), (1,)), ((), ()))
TN_DIMS = (((0,), (0,)), ((), ()))


def _params(*sem):
    return pltpu.CompilerParams(dimension_semantics=sem, vmem_limit_bytes=VMEM_LIMIT)


def _tile(n, want):
    best = LANES
    for cand in range(LANES, min(n, want) + 1, LANES):
        if n % cand == 0:
            best = cand
    assert n % best == 0
    return best


def _column_tiles(w, tn):
    k, n = w.shape
    return w.astype(BF16).reshape(k, n // tn, tn).transpose(1, 0, 2)


def _silu(x):
    return x * jax.nn.sigmoid(x)


def _split_bf16(x):
    hi = x.astype(BF16)
    lo = (x - hi.astype(F32)).astype(BF16)
    return hi, lo


def _mod_kernel(c_ref, w_ref, b_ref, o_ref):
    a_hi, a_lo = _split_bf16(_silu(c_ref[...]))
    w_hi, w_lo = _split_bf16(w_ref[...])
    acc = jnp.dot(a_hi, w_hi, preferred_element_type=F32)
    acc += jnp.dot(a_lo, w_hi, preferred_element_type=F32)
    acc += jnp.dot(a_hi, w_lo, preferred_element_type=F32)
    o_ref[...] = acc + b_ref[...]


def _modulation(cvec, w_mod, b_mod, tn=512):
    depth, d, n = w_mod.shape
    tn = _tile(n, tn)
    return pl.pallas_call(
        _mod_kernel,
        grid=(depth, n // tn),
        in_specs=[
            pl.BlockSpec((MOD_ROWS, d), lambda l, j: (0, 0)),
            pl.BlockSpec((None, d, tn), lambda l, j: (l, 0, j)),
            pl.BlockSpec((None, 1, tn), lambda l, j: (l, 0, j)),
        ],
        out_specs=pl.BlockSpec((None, MOD_ROWS, tn), lambda l, j: (l, 0, j)),
        out_shape=jax.ShapeDtypeStruct((depth, MOD_ROWS, n), F32),
        compiler_params=_params("arbitrary", "arbitrary"),
        name="modulation",
    )(cvec, w_mod, b_mod.reshape(depth, 1, n))


def _norm_mod(x, g, shift, scale):
    y = x * lax.rsqrt(jnp.mean(x * x, axis=-1, keepdims=True) + EPS) * g
    return y * (1.0 + scale) + shift


def _seq_of_row(row0, p_rows, s_len):
    return jnp.where(row0 < p_rows, 0, 1 + (row0 - p_rows) // s_len)


def _mod_spec(chunk, d, tm, p_rows, s_len, tile_off=0):
    return pl.BlockSpec(
        (None, 1, d),
        lambda i, j: (_seq_of_row((i + tile_off) * tm, p_rows, s_len), 0, chunk))


def _proj_kernel(x_ref, g_ref, sh_ref, sc_ref, w_ref, *rest, rope, q_tiles, qk_tiles, q_scale):
    if rope:
        cos_ref, sin_ref, o_ref, h_ref = rest
    else:
        o_ref, h_ref = rest
    j = pl.program_id(1)

    @pl.when(j == 0)
    def _():
        h_ref[...] = _norm_mod(x_ref[...], g_ref[...], sh_ref[...], sc_ref[...]).astype(BF16)

    if not rope:
        o_ref[...] = jnp.dot(h_ref[...], w_ref[...],
                             preferred_element_type=F32).astype(o_ref.dtype)
        return

    @pl.when(j < qk_tiles)
    def _():
        tn = o_ref.shape[1]
        sub = min(tn, MXU_COLS)
        h = h_ref[...]
        accs = [jnp.dot(h, w_ref[:, c:c + sub], preferred_element_type=F32)
                for c in range(0, tn, sub)]
        scale = jnp.where(j < q_tiles, q_scale, 1.0)
        cos = cos_ref[...] * scale
        sin = sin_ref[...] * scale
        lane = lax.broadcasted_iota(jnp.int32, cos.shape, 1)
        first_of_pair = (lane // (LANES // 4)) % 2 == 0
        for ci, acc in enumerate(accs):
            for s in range(sub // LANES):
                xs = acc[:, s * LANES:(s + 1) * LANES]
                partner = jnp.where(first_of_pair,
                                    pltpu.roll(xs, LANES - LANES // 4, 1),
                                    pltpu.roll(xs, LANES // 4, 1))
                col = ci * sub + s * LANES
                o_ref[:, col:col + LANES] = (xs * cos + partner * sin).astype(o_ref.dtype)

    @pl.when(j >= qk_tiles)
    def _():
        o_ref[...] = jnp.dot(h_ref[...], w_ref[...],
                             preferred_element_type=F32).astype(o_ref.dtype)


def _project(x, g, mod, shift_chunk, scale_chunk, w, *, row_off, rows, out_dtype,
             p_rows, s_len, tm=512, tn=512, rope_tables=None, q_scale=1.0):
    t, d = x.shape
    n = w.shape[1]
    tm = math.gcd(tm, p_rows, s_len)
    tile_off = row_off // tm
    rope = rope_tables is not None
    tn = _tile(n // 3 if rope else n, tn)
    in_specs = [
        pl.BlockSpec((tm, d), lambda i, j: (i + tile_off, 0)),
        pl.BlockSpec((1, d), lambda i, j: (0, 0)),
        _mod_spec(shift_chunk, d, tm, p_rows, s_len, tile_off),
        _mod_spec(scale_chunk, d, tm, p_rows, s_len, tile_off),
        pl.BlockSpec((None, d, tn), lambda i, j: (j, 0, 0)),
    ]
    args = [x, g.reshape(1, d), mod, mod, _column_tiles(w, tn)]
    q_tiles = qk_tiles = 0
    if rope:
        cos, sin = rope_tables
        tiles_per_seq = s_len // tm
        in_specs += [pl.BlockSpec((tm, LANES), lambda i, j: (i % tiles_per_seq, 0))] * 2
        args += [cos, sin]
        q_tiles = (n // 3) // tn
        qk_tiles = 2 * q_tiles
    kern = functools.partial(_proj_kernel, rope=rope, q_tiles=q_tiles, qk_tiles=qk_tiles,
                             q_scale=q_scale)
    return pl.pallas_call(
        kern,
        grid=(rows // tm, n // tn),
        in_specs=in_specs,
        out_specs=pl.BlockSpec((tm, tn), lambda i, j: (i, j)),
        out_shape=jax.ShapeDtypeStruct((rows, n), out_dtype),
        scratch_shapes=[pltpu.VMEM((tm, d), BF16)],
        compiler_params=_params("arbitrary", "arbitrary"),
        name="norm_mod_project",
    )(*args)


def _hgrn_group_local(groups):
    c, m = HGRN_CHUNK, HGRN_SUB
    nb = c // m
    t = lax.broadcasted_iota(jnp.int32, (c, c), 0)
    s = lax.broadcasted_iota(jnp.int32, (c, c), 1)

    jobs = []
    for q, z, v, lb, direction in groups:
        dk = q.shape[1]
        tri = (s <= t) if direction == 0 else (s >= t)
        cum = jnp.where(tri, 1.0, 0.0).astype(BF16)
        f = lb + (1.0 - lb) * jax.nn.sigmoid(z)
        g2 = jnp.log(f) * LOG2_E
        k = 1.0 - f
        g_hi = g2.astype(BF16)
        g_lo = (g2 - g_hi.astype(F32)).astype(BF16)
        v_bf = v.astype(BF16)
        for ci in range(q.shape[0] // c):
            rows = slice(ci * c, (ci + 1) * c)
            jobs.append(dict(direction=direction, tri=tri, cum=cum, q=q[rows], k=k[rows],
                             v=v_bf[rows], g=jnp.concatenate([g_hi[rows], g_lo[rows]], axis=1)))

    def padded(x, lo, hi):
        parts = [jnp.zeros((lo, dk), BF16)] if lo else []
        parts.append(x)
        if hi < c:
            parts.append(jnp.zeros((c - hi, dk), BF16))
        return jnp.concatenate(parts, axis=0) if len(parts) > 1 else x

    for job in jobs:
        job["sums"] = jnp.dot(job["cum"], job["g"], preferred_element_type=F32)

    for job in jobs:
        direction = job["direction"]
        sums = job["sums"]
        b = sums[:, :dk] + sums[:, dk:]
        anchors = []
        for i in range(nb):
            a_row = i * m + (m // 2 - 1 if direction == 0 else m // 2)
            anchors.append(b[a_row:a_row + 1, :])
        b_anchor = jnp.concatenate([jnp.broadcast_to(r, (m, dk)) for r in anchors], axis=0)
        q_rel = (job["q"] * jnp.exp2(b - b_anchor)).astype(BF16)
        full = nb - 1 if direction == 0 else 0
        q_cat, k_cat = [], []
        k_full = None
        for i in range(nb):
            lo, hi = (0, m * (i + 1)) if direction == 0 else (m * i, c)
            ki = job["k"][lo:hi] * jnp.exp2(anchors[i] - b[lo:hi])
            if i == full:
                k_full = ki
            k_cat.append(padded(ki.astype(BF16), lo, hi))
            q_cat.append(padded(q_rel[m * i:m * (i + 1)], m * i, m * (i + 1)))
        job["a"] = lax.dot_general(jnp.concatenate(q_cat, axis=1),
                                   jnp.concatenate(k_cat, axis=1), NT_DIMS,
                                   preferred_element_type=F32)
        edge = c - 1 if direction == 0 else 0
        total = b[edge:edge + 1, :]
        job["k_dec"] = (k_full * jnp.exp2(total - anchors[full])).astype(BF16)
        job["q_in"] = (job["q"] * jnp.exp2(b)).astype(BF16)
        job["decay"] = jnp.exp2(total)

    for job in jobs:
        job["u_t"] = lax.dot_general(job["v"], job["k_dec"], TN_DIMS,
                                     preferred_element_type=F32)

    out, pos = [], 0
    for q, *_ in groups:
        n = q.shape[0] // c
        out.append([(jnp.where(j["tri"], j["a"], 0.0).astype(BF16), j["v"], j["q_in"], j["u_t"],
                     j["decay"]) for j in jobs[pos:pos + n]])
        pos += n
    return out


def _hgrn_kernel(*refs, seq_len, layer_j, n_lb, has_s0, emit_state):
    q_ref, zf_ref, zb_ref, i_ref, g_ref, lg_ref, on_ref = refs[:7]
    pos = 7
    s0_ref = st_ref = None
    if has_s0:
        s0_ref = refs[pos]
        pos += 1
    o_ref = refs[pos]
    pos += 1
    if emit_state:
        st_ref = refs[pos]
        pos += 1
    of_scr, ob_scr = refs[pos], refs[pos + 1]

    grp = math.gcd(HGRN_GROUP, seq_len // HGRN_CHUNK)
    rows = grp * HGRN_CHUNK
    n = seq_len // rows
    dv = q_ref.shape[1]

    lbs = []
    for d in range(2):
        lg = lg_ref[d * n_lb:(d + 1) * n_lb, :]
        e = jnp.exp(lg - jnp.max(lg, axis=0, keepdims=True))
        lbs.append(jnp.sum(e[:layer_j + 1], axis=0, keepdims=True) / jnp.sum(e, axis=0, keepdims=True))

    if has_s0:
        st0 = (s0_ref[0].T, s0_ref[1].T)
    else:
        st0 = (jnp.zeros((dv, dv), F32), jnp.zeros((dv, dv), F32))

    def advance(local, st):
        _, _, q_in, u_t, decay = local
        o = lax.dot_general(q_in, st.astype(BF16), NT_DIMS, preferred_element_type=F32)
        return st * decay + u_t, o

    def add_intra(local, o_inter):
        a, v = local[:2]
        return o_inter + jnp.dot(a, v, preferred_element_type=F32)

    def body(gi, carry):
        st_f, st_b = carry
        rf = pl.ds(pl.multiple_of(gi * rows, rows), rows)
        rb = pl.ds(pl.multiple_of((n - 1 - gi) * rows, rows), rows)
        loc_f, loc_b = _hgrn_group_local([
            (q_ref[rf, :], zf_ref[rf, :], i_ref[rf, :], lbs[0], 0),
            (q_ref[rb, :], zb_ref[rb, :], i_ref[rb, :], lbs[1], 1)])
        o_f, o_b = [None] * grp, [None] * grp
        for ci in range(grp):
            st_f, o_f[ci] = advance(loc_f[ci], st_f)
            st_b, o_b[grp - 1 - ci] = advance(loc_b[grp - 1 - ci], st_b)
        o_f = [add_intra(l, o) for l, o in zip(loc_f, o_f)]
        o_b = [add_intra(l, o) for l, o in zip(loc_b, o_b)]
        of_scr[rf, :] = jnp.concatenate(o_f, axis=0)
        ob_scr[rb, :] = jnp.concatenate(o_b, axis=0)
        return st_f, st_b

    st_f, st_b = lax.fori_loop(0, n, body, st0)
    if emit_state:
        st_ref[0] = st_f.T
        st_ref[1] = st_b.T

    fin_rows = math.gcd(seq_len, 256)
    onorm = on_ref[...]

    def finish(ri, _):
        r = pl.ds(pl.multiple_of(ri * fin_rows, fin_rows), fin_rows)
        o = of_scr[r, :] + ob_scr[r, :]
        y = o * lax.rsqrt(jnp.mean(o * o, axis=-1, keepdims=True) + EPS) * onorm
        o_ref[r, :] = (y * _silu(g_ref[r, :])).astype(o_ref.dtype)
        return 0

    lax.fori_loop(0, seq_len // fin_rows, finish, 0)


def _hgrn_scan(proj, lb_logits, onorm, s0, *, layer_j, n_seq, seq_len, row_off, heads, emit_state):
    dk = LANES
    d = heads * dk
    blk_off = row_off // seq_len
    n_lb = lb_logits.shape[1]

    def sec(k):
        return pl.BlockSpec((seq_len, dk), lambda b, h: (b + blk_off, k * heads + h))

    in_specs = [sec(0), sec(1), sec(2), sec(3), sec(4),
                pl.BlockSpec((2 * n_lb, dk), lambda b, h: (0, h)),
                pl.BlockSpec((1, dk), lambda b, h: (0, 0))]
    args = [proj] * 5 + [lb_logits.reshape(2 * n_lb, d), onorm.reshape(1, dk)]
    if s0 is not None:
        in_specs.append(pl.BlockSpec((None, None, 2, None, dk, dk),
                                     lambda b, h: (b, layer_j, 0, h, 0, 0)))
        args.append(s0)
    out_shape = [jax.ShapeDtypeStruct((n_seq * seq_len, d), BF16)]
    out_specs = [pl.BlockSpec((seq_len, dk), lambda b, h: (b, h))]
    if emit_state:
        out_shape.append(jax.ShapeDtypeStruct((n_seq, 2, heads, dk, dk), F32))
        out_specs.append(pl.BlockSpec((None, 2, None, dk, dk), lambda b, h: (b, 0, h, 0, 0)))
    kern = functools.partial(_hgrn_kernel, seq_len=seq_len, layer_j=layer_j, n_lb=n_lb,
                             has_s0=s0 is not None, emit_state=emit_state)
    return pl.pallas_call(
        kern,
        grid=(n_seq, heads),
        in_specs=in_specs,
        out_specs=out_specs,
        out_shape=out_shape,
        scratch_shapes=[pltpu.VMEM((seq_len, dk), F32), pltpu.VMEM((seq_len, dk), F32)],
        compiler_params=_params("arbitrary", "arbitrary"),
        name="hgrn_scan",
    )(*args)


def _attn_kernel(*refs, has_cache, lam_init, q_scale):
    q_ref, k_ref, v_ref = refs[:3]
    pos = 3
    kc_ref = vc_ref = None
    if has_cache:
        kc_ref, vc_ref = refs[3], refs[4]
        pos = 5
    lam_ref, sub_ref, o_ref = refs[pos:pos + 3]

    lp = lam_ref[...]
    lam = (jnp.exp(jnp.sum(lp[0:1] * lp[1:2], axis=-1, keepdims=True))
           - jnp.exp(jnp.sum(lp[2:3] * lp[3:4], axis=-1, keepdims=True)) + lam_init)

    hd = LANES
    v_new = v_ref[...].astype(BF16)
    v_old = vc_ref[...].astype(BF16) if has_cache else None
    tq = q_ref.shape[0]
    rb = min(tq, ATTN_ROW_BLOCK)
    chains = [(r, j) for r in range(0, tq, rb) for j in range(2)]
    scores = []
    for r, j in chains:
        cols = slice(j * hd, (j + 1) * hd)
        qj = q_ref[r:r + rb, cols]
        if q_scale != 1.0:
            qj = qj.astype(F32) * q_scale
        qj = qj.astype(BF16)
        s_new = lax.dot_general(qj, k_ref[:, cols].astype(BF16), NT_DIMS,
                                preferred_element_type=F32)
        s_old = None
        if has_cache:
            s_old = lax.dot_general(qj, kc_ref[:, cols].astype(BF16), NT_DIMS,
                                    preferred_element_type=F32)
        scores.append((s_new, s_old))
    outs = {}
    for (r, j), (s_new, s_old) in zip(chains, scores):
        mx = jnp.max(s_new, axis=-1, keepdims=True)
        if has_cache:
            mx = jnp.maximum(mx, jnp.max(s_old, axis=-1, keepdims=True))
        p_new = jnp.exp2(s_new - mx)
        den = jnp.sum(p_new, axis=-1, keepdims=True)
        acc = jnp.dot(p_new.astype(BF16), v_new, preferred_element_type=F32)
        if has_cache:
            p_old = jnp.exp2(s_old - mx)
            den += jnp.sum(p_old, axis=-1, keepdims=True)
            acc += jnp.dot(p_old.astype(BF16), v_old, preferred_element_type=F32)
        outs[r, j] = acc / den
    sub = sub_ref[...]
    for r in range(0, tq, rb):
        o = outs[r, 0] - lam * outs[r, 1]
        y = o * lax.rsqrt(jnp.mean(o * o, axis=-1, keepdims=True) + EPS) * sub
        o_ref[r:r + rb, :] = (y * (1.0 - lam_init)).astype(o_ref.dtype)


def _diff_attention(qkv, cache_k, cache_v, lam_p, subln, *, n_seq, seq_len, heads, lam_init,
                    q_scale, tq):
    hd2 = 2 * LANES
    d = heads * hd2
    qb = seq_len // tq
    in_specs = [
        pl.BlockSpec((tq, hd2), lambda b, h, i: (b * qb + i, h)),
        pl.BlockSpec((seq_len, hd2), lambda b, h, i: (b, heads + h)),
        pl.BlockSpec((seq_len, hd2), lambda b, h, i: (b, 2 * heads + h)),
    ]
    args = [qkv, qkv, qkv]
    if cache_k is not None:
        past = cache_k.shape[1]
        in_specs += [pl.BlockSpec((None, past, hd2), lambda b, h, i: (b, 0, h))] * 2
        args += [cache_k, cache_v]
    in_specs += [pl.BlockSpec(lam_p.shape, lambda b, h, i: (0, 0)),
                 pl.BlockSpec((1, hd2), lambda b, h, i: (0, 0))]
    args += [lam_p, subln.reshape(1, hd2)]
    kern = functools.partial(_attn_kernel, has_cache=cache_k is not None, lam_init=lam_init,
                             q_scale=q_scale)
    return pl.pallas_call(
        kern,
        grid=(n_seq, heads, qb),
        in_specs=in_specs,
        out_specs=pl.BlockSpec((tq, hd2), lambda b, h, i: (b * qb + i, h)),
        out_shape=jax.ShapeDtypeStruct((n_seq * seq_len, d), BF16),
        compiler_params=_params("arbitrary", "arbitrary", "arbitrary"),
        name="diff_attention",
    )(*args)


def _out_proj_kernel(*refs, seg_tiles):
    nseg = len(seg_tiles)
    a_refs = refs[:nseg]
    w_ref, x_ref, gate_ref, o_ref = refs[nseg:]
    i = pl.program_id(0)
    lo = 0
    for a_ref, cnt in zip(a_refs, seg_tiles):
        @pl.when((i >= lo) & (i < lo + cnt))
        def _(a_ref=a_ref):
            y = jnp.dot(a_ref[...], w_ref[...], preferred_element_type=F32)
            o_ref[...] = x_ref[...] + gate_ref[...] * y
        lo += cnt


def _out_project(a_segs, w, x, mod, gate_chunk, *, p_rows, s_len, tm=512, tn=1024):
    t, d = x.shape
    kdim = w.shape[0]
    tn = _tile(d, tn)
    seg_tiles = tuple(a.shape[0] // tm for a in a_segs)
    in_specs = []
    lo = 0
    for cnt in seg_tiles:
        in_specs.append(pl.BlockSpec(
            (tm, kdim), lambda i, j, lo=lo, cnt=cnt: (jnp.clip(i - lo, 0, cnt - 1), 0)))
        lo += cnt
    in_specs += [
        pl.BlockSpec((kdim, tn), lambda i, j: (0, j)),
        pl.BlockSpec((tm, tn), lambda i, j: (i, j)),
        pl.BlockSpec((None, 1, tn),
                     lambda i, j: (_seq_of_row(i * tm, p_rows, s_len), 0,
                                   gate_chunk * (d // tn) + j)),
    ]
    return pl.pallas_call(
        functools.partial(_out_proj_kernel, seg_tiles=seg_tiles),
        grid=(t // tm, d // tn),
        in_specs=in_specs,
        out_specs=pl.BlockSpec((tm, tn), lambda i, j: (i, j)),
        out_shape=jax.ShapeDtypeStruct((t, d), F32),
        compiler_params=_params("arbitrary", "arbitrary"),
        name="out_project_residual",
    )(*a_segs, w, x, mod)


def _ffn_kernel(x_ref, xp_ref, xn_ref, g_ref, sh_ref, sc_ref, gate_ref, wg_ref, wv_ref,
                conv_a, wd_a, conv_b, wd_b,
                o_ref, h_ref, ug_ref, uv_ref, acc_ref, *, tm, nj, p_rows, p_len, s_len):
    i = pl.program_id(0)
    j = pl.program_id(1)
    hs = xp_ref.shape[0]
    sub = FFN_TILE
    ext = tm + 2 * hs
    row0 = i * tm
    in_prompt = row0 < p_rows
    inner_edges = jnp.where(in_prompt, 1.0, 0.0)
    win = lax.broadcasted_iota(jnp.int32, (2 * SUBLANES, 1), 0)

    def up(cols):
        h = h_ref[...]
        return (jnp.dot(h, wg_ref[:, cols], preferred_element_type=F32),
                jnp.dot(h, wv_ref[:, cols], preferred_element_type=F32))

    def conv(u_all, cw, cb):
        u = u_all[:tm]
        prev = pltpu.roll(u_all, 1, 0)[:tm]
        nxt = pltpu.roll(u_all, ext - 1, 0)[:tm]
        y = prev * cw[0:1] + u * cw[1:2] + nxt * cw[2:3] + cb
        pieces, done = [], 0
        for edge in range(p_len, tm, p_len):
            lo, hi = edge - SUBLANES, edge + SUBLANES
            leak = (jnp.where(win == SUBLANES - 1, nxt[lo:hi] * cw[2:3], 0.0)
                    + jnp.where(win == SUBLANES, prev[lo:hi] * cw[0:1], 0.0))
            pieces += [y[done:lo], y[lo:hi] - inner_edges * leak]
            done = hi
        return jnp.concatenate(pieces + [y[done:]], axis=0) if pieces else y

    def activate(ug, uv, conv_ref):
        p = conv_ref[...]
        gate = conv(ug, p[0:3], p[3:4])
        val = conv(uv, p[4:7], p[7:8])
        return (_silu(gate) * val).astype(BF16)

    def activate_a(ug, uv):
        return activate(ug, uv, conv_a)

    def activate_b():
        return activate(ug_ref[...], uv_ref[...], conv_b)

    def project_down(act, wd):
        acc_ref[...] += jnp.dot(act, wd[...], preferred_element_type=F32)

    def park_b(ug, uv):
        ug_ref[...] = ug
        uv_ref[...] = uv

    cols_a, cols_b = slice(0, sub), slice(sub, 2 * sub)

    @pl.when(j == 0)
    def _():
        g, sh, sc = g_ref[...], sh_ref[...], sc_ref[...]
        h_ref[0:tm, :] = _norm_mod(x_ref[...], g, sh, sc).astype(BF16)
        end = row0 + tm
        at_seq_end = jnp.where(in_prompt, lax.rem(end, p_len) == 0,
                               lax.rem(end - p_rows, s_len) == 0)
        at_seq_start = jnp.where(in_prompt, lax.rem(row0, p_len) == 0,
                                 lax.rem(row0 - p_rows, s_len) == 0)
        after = _norm_mod(xn_ref[...], g, sh, sc) * jnp.where(at_seq_end, 0.0, 1.0)
        before = _norm_mod(xp_ref[...], g, sh, sc) * jnp.where(at_seq_start, 0.0, 1.0)
        h_ref[tm:tm + hs, :] = after.astype(BF16)
        h_ref[tm + hs:ext, :] = before.astype(BF16)
        acc_ref[...] = jnp.zeros_like(acc_ref)
        ua = up(cols_a)
        park_b(*up(cols_b))
        project_down(activate_a(*ua), wd_a)

    @pl.when((j > 0) & (j < nj))
    def _():
        act_b = activate_b()
        ua = up(cols_a)
        project_down(act_b, wd_b)
        ub = up(cols_b)
        act_a = activate_a(*ua)
        park_b(*ub)
        project_down(act_a, wd_a)

    @pl.when(j == nj)
    def _():
        project_down(activate_b(), wd_b)
        o_ref[...] = x_ref[...] + gate_ref[...] * acc_ref[...]


def _ffn(x, g, mod, shift_chunk, scale_chunk, gate_chunk, w_up, cw, cb, w_down, *,
         p_rows, p_len, s_len, tm=512):
    t, d = x.shape
    ff = w_up.shape[1] // 2
    sub = FFN_TILE
    assert ff % (2 * sub) == 0
    assert (tm % p_len == 0 or p_len % tm == 0) and p_rows % tm == 0 and s_len % tm == 0
    nj = ff // (2 * sub)
    nt = ff // sub
    hs = 16
    per = tm // hs
    last_blk = t // hs - 1

    def tile_a(i, j):
        return jnp.minimum(2 * j, nt - 2)

    def tile_b(i, j):
        return jnp.maximum(2 * j - 1, 1)

    def tile_specs(tile):
        return [
            pl.BlockSpec((None, 2 * (taps + 1), sub), lambda i, j: (tile(i, j), 0, 0)),
            pl.BlockSpec((sub, d), lambda i, j: (tile(i, j), 0)),
        ]

    taps = cw.shape[0]
    conv_p = jnp.concatenate([cw[:, :ff], cb[None, :ff], cw[:, ff:], cb[None, ff:]], axis=0)
    conv_p = conv_p.reshape(2 * (taps + 1), nt, sub).transpose(1, 0, 2)

    in_specs = [
        pl.BlockSpec((tm, d), lambda i, j: (i, 0)),
        pl.BlockSpec((hs, d), lambda i, j: (jnp.maximum(i * per - 1, 0), 0)),
        pl.BlockSpec((hs, d), lambda i, j: (jnp.minimum((i + 1) * per, last_blk), 0)),
        pl.BlockSpec((1, d), lambda i, j: (0, 0)),
        _mod_spec(shift_chunk, d, tm, p_rows, s_len),
        _mod_spec(scale_chunk, d, tm, p_rows, s_len),
        _mod_spec(gate_chunk, d, tm, p_rows, s_len),
        pl.BlockSpec((None, d, 2 * sub), lambda i, j: (jnp.minimum(j, nj - 1), 0, 0)),
        pl.BlockSpec((None, d, 2 * sub), lambda i, j: (jnp.minimum(j, nj - 1) + nj, 0, 0)),
    ] + tile_specs(tile_a) + tile_specs(tile_b)
    w_up = _column_tiles(w_up, 2 * sub)
    tile_args = [conv_p, w_down.astype(BF16)]
    kern = functools.partial(_ffn_kernel, tm=tm, nj=nj, p_rows=p_rows, p_len=p_len,
                             s_len=s_len)
    return pl.pallas_call(
        kern,
        grid=(t // tm, nj + 1),
        in_specs=in_specs,
        out_specs=pl.BlockSpec((tm, d), lambda i, j: (i, 0)),
        out_shape=jax.ShapeDtypeStruct((t, d), F32),
        scratch_shapes=[pltpu.VMEM((tm + 2 * hs, d), BF16),
                        pltpu.VMEM((tm + 2 * hs, sub), F32),
                        pltpu.VMEM((tm + 2 * hs, sub), F32),
                        pltpu.VMEM((tm, d), F32)],
        compiler_params=_params("arbitrary", "arbitrary"),
        name="conv_ffn",
    )(x, x, x, g.reshape(1, d), mod, mod, mod, w_up, w_up, *tile_args, *tile_args)


def _final_norm_kernel(x_ref, g_ref, o_ref):
    x = x_ref[...]
    o_ref[...] = x * lax.rsqrt(jnp.mean(x * x, axis=-1, keepdims=True) + EPS) * g_ref[...]


def _final_norm(x, g, *, row_off, rows, tm=512):
    d = x.shape[1]
    off = row_off // tm
    return pl.pallas_call(
        _final_norm_kernel,
        grid=(rows // tm,),
        in_specs=[pl.BlockSpec((tm, d), lambda i: (i + off, 0)),
                  pl.BlockSpec((1, d), lambda i: (0, 0))],
        out_specs=pl.BlockSpec((tm, d), lambda i: (i, 0)),
        out_shape=jax.ShapeDtypeStruct((rows, d), F32),
        compiler_params=_params("arbitrary"),
        name="final_norm",
    )(x, g.reshape(1, d))


def _rope_tables(seq_len, hd):
    pos = jnp.arange(seq_len)
    row = (pos // GRID_W).astype(F32)
    col = (pos % GRID_W).astype(F32)
    nf = hd // 4
    inv = ROPE_BASE ** (-jnp.arange(nf, dtype=F32) / nf)
    ar = row[:, None] * inv
    ac = col[:, None] * inv
    cos = jnp.concatenate([jnp.cos(ar), jnp.cos(ar), jnp.cos(ac), jnp.cos(ac)], axis=1)
    sin = jnp.concatenate([-jnp.sin(ar), jnp.sin(ar), -jnp.sin(ac), jnp.sin(ac)], axis=1)
    return cos, sin


def kernel(x_prompt, x_sample, c, cache_hgrn_state, cache_attn_k, cache_attn_v, c_ctx, w_mod, b_mod, norm_mix, norm_ffn, w_hgrn_in, hgrn_lb_logits, hgrn_onorm, w_hgrn_out, w_attn_in, attn_lambda, attn_subln, w_attn_out, w_ffn_up, ffn_conv_w, ffn_conv_b, w_ffn_down, norm_final):
    batch, p_len, d = x_prompt.shape
    dec_batch, s_len, _ = x_sample.shape
    depth = w_mod.shape[0]
    p_rows = batch * p_len
    s_rows = dec_batch * s_len
    hgrn_heads = d // LANES
    diff_heads = d // (2 * LANES)
    hd = LANES
    assert 1 + dec_batch <= MOD_ROWS and p_rows % s_len == 0

    x = jnp.concatenate([x_prompt.reshape(p_rows, d), x_sample.reshape(s_rows, d)], axis=0)
    cvec = jnp.concatenate(
        [c_ctx[None, :], c, jnp.zeros((MOD_ROWS - 1 - dec_batch, d), F32)], axis=0)
    mods = _modulation(cvec, w_mod, b_mod)
    seg = dict(p_rows=p_rows, s_len=s_len)

    hgrn_states, attn_ks, attn_vs = [], [], []
    for l in range(depth):
        mod = mods[l][:, None, :]
        j = l // N_MIXERS
        if l % N_MIXERS == 0:
            proj = _project(x, norm_mix[l], mod, 0, 1, w_hgrn_in[j],
                            row_off=0, rows=p_rows + s_rows, out_dtype=F32, tm=1024, tn=1024,
                            **seg)
            mix_p, st = _hgrn_scan(proj, hgrn_lb_logits, hgrn_onorm[j], None, layer_j=j,
                                   n_seq=batch, seq_len=p_len, row_off=0, heads=hgrn_heads,
                                   emit_state=True)
            hgrn_states.append(st)
            mix_s, = _hgrn_scan(proj, hgrn_lb_logits, hgrn_onorm[j], cache_hgrn_state,
                                layer_j=j, n_seq=dec_batch, seq_len=s_len, row_off=p_rows,
                                heads=hgrn_heads, emit_state=False)
            w_out = w_hgrn_out[j]
        else:
            lam_init = 0.8 - 0.6 * math.exp(-0.3 * l)
            q_scale = hd ** -0.5 * LOG2_E
            w_in = w_attn_in[j]
            qkv_p = _project(x, norm_mix[l], mod, 0, 1, w_in, row_off=0, rows=p_rows,
                             out_dtype=F32, tn=1024, **seg)
            attn_ks.append(qkv_p[:, d:2 * d].reshape(batch, p_len, 2 * diff_heads, hd))
            attn_vs.append(qkv_p[:, 2 * d:].reshape(batch, p_len, diff_heads, 2 * hd))
            mix_p = _diff_attention(qkv_p, None, None, attn_lambda[j], attn_subln[j],
                                    n_seq=batch, seq_len=p_len, heads=diff_heads,
                                    lam_init=lam_init, q_scale=q_scale, tq=p_len)
            qkv_s = _project(x, norm_mix[l], mod, 0, 1, w_in, row_off=p_rows, rows=s_rows,
                             out_dtype=BF16, rope_tables=_rope_tables(s_len, hd),
                             q_scale=q_scale, tn=1024, **seg)
            past = cache_attn_k.shape[2]
            mix_s = _diff_attention(qkv_s, cache_attn_k[:, j].reshape(dec_batch, past, d),
                                    cache_attn_v[:, j].reshape(dec_batch, past, d),
                                    attn_lambda[j], attn_subln[j], n_seq=dec_batch,
                                    seq_len=s_len, heads=diff_heads, lam_init=lam_init,
                                    q_scale=1.0, tq=512)
            w_out = w_attn_out[j]
        x = _out_project([mix_p, mix_s], w_out.astype(BF16), x, mod, 2, tn=d, **seg)
        x = _ffn(x, norm_ffn[l], mod, 3, 4, 5, w_ffn_up[l], ffn_conv_w[l], ffn_conv_b[l],
                 w_ffn_down[l], p_len=p_len, **seg)

    y_prompt = _final_norm(x, norm_final, row_off=0, rows=p_rows).reshape(batch, p_len, d)
    y_sample = _final_norm(x, norm_final, row_off=p_rows, rows=s_rows).reshape(dec_batch, s_len, d)
    new_hgrn_state = jnp.stack(hgrn_states, axis=1)
    new_attn_k = jnp.stack(attn_ks, axis=1)
    new_attn_v = jnp.stack(attn_vs, axis=1)
    return (y_prompt, y_sample, new_hgrn_state, new_attn_k, new_attn_v)
```

```python
import functools
import math

import jax
import jax.numpy as jnp
from jax import lax
from jax.experimental import pallas as pl
from jax.experimental.pallas import tpu as pltpu

F32 = jnp.float32
BF16 = jnp.bfloat16

EPS = 1e-6
GRID_W = 64
ROPE_BASE = 10000.0
N_MIXERS = 2

LANES = 128
SUBLANES = 8
MXU_COLS = 256
MOD_ROWS = 16
HGRN_CHUNK = 64
HGRN_SUB = 16
HGRN_GROUP = 8
ATTN_ROW_BLOCK = 128
FFN_TILE = 256
LOG2_E = 1.4426950408889634
VMEM_LIMIT = 56 * 1024 * 1024

NT_DIMS = (((1,), (1,)), ((), ()))
TN_DIMS = (((0,), (0,)), ((), ()))


def _params(*sem):
    return pltpu.CompilerParams(dimension_semantics=sem, vmem_limit_bytes=VMEM_LIMIT)


def _tile(n, want):
    best = LANES
    for cand in range(LANES, min(n, want) + 1, LANES):
        if n % cand == 0:
            best = cand
    assert n % best == 0
    return best


def _column_tiles(w, tn):
    k, n = w.shape
    return w.astype(BF16).reshape(k, n // tn, tn).transpose(1, 0, 2)


def _silu(x):
    return x * jax.nn.sigmoid(x)


def _split_bf16(x):
    hi = x.astype(BF16)
    lo = (x - hi.astype(F32)).astype(BF16)
    return hi, lo


def _mod_kernel(c_ref, w_ref, b_ref, o_ref):
    a_hi, a_lo = _split_bf16(_silu(c_ref[...]))
    w_hi, w_lo = _split_bf16(w_ref[...])
    acc = jnp.dot(a_hi, w_hi, preferred_element_type=F32)
    acc += jnp.dot(a_lo, w_hi, preferred_element_type=F32)
    acc += jnp.dot(a_hi, w_lo, preferred_element_type=F32)
    o_ref[...] = acc + b_ref[...]


def _modulation(cvec, w_mod, b_mod, tn=512):
    depth, d, n = w_mod.shape
    tn = _tile(n, tn)
    return pl.pallas_call(
        _mod_kernel,
        grid=(depth, n // tn),
        in_specs=[
            pl.BlockSpec((MOD_ROWS, d), lambda l, j: (0, 0)),
            pl.BlockSpec((None, d, tn), lambda l, j: (l, 0, j)),
            pl.BlockSpec((None, 1, tn), lambda l, j: (l, 0, j)),
        ],
        out_specs=pl.BlockSpec((None, MOD_ROWS, tn), lambda l, j: (l, 0, j)),
        out_shape=jax.ShapeDtypeStruct((depth, MOD_ROWS, n), F32),
        compiler_params=_params("arbitrary", "arbitrary"),
        name="modulation",
    )(cvec, w_mod, b_mod.reshape(depth, 1, n))


def _norm_mod(x, g, shift, scale):
    y = x * lax.rsqrt(jnp.mean(x * x, axis=-1, keepdims=True) + EPS) * g
    return y * (1.0 + scale) + shift


def _seq_of_row(row0, p_rows, s_len):
    return jnp.where(row0 < p_rows, 0, 1 + (row0 - p_rows) // s_len)


def _mod_spec(chunk, d, tm, p_rows, s_len, tile_off=0):
    return pl.BlockSpec(
        (None, 1, d),
        lambda i, j: (_seq_of_row((i + tile_off) * tm, p_rows, s_len), 0, chunk))


def _proj_kernel(*refs, x_bounds, tile_off, rope, q_tiles, qk_tiles, q_scale):
    x_refs = refs[:len(x_bounds)]
    g_ref, sh_ref, sc_ref, w_ref, *rest = refs[len(x_bounds):]
    if rope:
        cos_ref, sin_ref, o_ref, h_ref = rest
    else:
        o_ref, h_ref = rest
    tile = pl.program_id(0) + tile_off
    j = pl.program_id(1)

    for x_ref, (lo, cnt) in zip(x_refs, x_bounds):
        @pl.when((j == 0) & (tile >= lo) & (tile < lo + cnt))
        def _(x_ref=x_ref):
            h_ref[...] = _norm_mod(x_ref[...], g_ref[...], sh_ref[...],
                                   sc_ref[...]).astype(BF16)

    if not rope:
        o_ref[...] = jnp.dot(h_ref[...], w_ref[...],
                             preferred_element_type=F32).astype(o_ref.dtype)
        return

    @pl.when(j < qk_tiles)
    def _():
        tn = o_ref.shape[1]
        sub = min(tn, MXU_COLS)
        h = h_ref[...]
        accs = [jnp.dot(h, w_ref[:, c:c + sub], preferred_element_type=F32)
                for c in range(0, tn, sub)]
        scale = jnp.where(j < q_tiles, q_scale, 1.0)
        cos = cos_ref[...] * scale
        sin = sin_ref[...] * scale
        lane = lax.broadcasted_iota(jnp.int32, cos.shape, 1)
        first_of_pair = (lane // (LANES // 4)) % 2 == 0
        for ci, acc in enumerate(accs):
            for s in range(sub // LANES):
                xs = acc[:, s * LANES:(s + 1) * LANES]
                partner = jnp.where(first_of_pair,
                                    pltpu.roll(xs, LANES - LANES // 4, 1),
                                    pltpu.roll(xs, LANES // 4, 1))
                col = ci * sub + s * LANES
                o_ref[:, col:col + LANES] = (xs * cos + partner * sin).astype(o_ref.dtype)

    @pl.when(j >= qk_tiles)
    def _():
        o_ref[...] = jnp.dot(h_ref[...], w_ref[...],
                             preferred_element_type=F32).astype(o_ref.dtype)


def _row_segment_specs(segs, tm, tile_off=0):
    specs, bounds, lo = [], [], 0
    for a in segs:
        cnt = a.shape[0] // tm
        specs.append(pl.BlockSpec(
            (tm, a.shape[1]),
            lambda i, j, lo=lo, cnt=cnt: (jnp.clip(i + tile_off - lo, 0, cnt - 1), 0)))
        bounds.append((lo, cnt))
        lo += cnt
    return specs, tuple(bounds)


def _project(x_segs, g, mod, shift_chunk, scale_chunk, w, *, row_off, rows, out_dtype,
             p_rows, s_len, tm=512, tn=512, rope_tables=None, q_scale=1.0):
    d = x_segs[0].shape[1]
    n = w.shape[1]
    tm = math.gcd(tm, p_rows, s_len)
    tile_off = row_off // tm
    rope = rope_tables is not None
    tn = _tile(n // 3 if rope else n, tn)
    x_specs, x_bounds = _row_segment_specs(x_segs, tm, tile_off)
    in_specs = x_specs + [
        pl.BlockSpec((1, d), lambda i, j: (0, 0)),
        _mod_spec(shift_chunk, d, tm, p_rows, s_len, tile_off),
        _mod_spec(scale_chunk, d, tm, p_rows, s_len, tile_off),
        pl.BlockSpec((d, tn), lambda i, j: (0, j)),
    ]
    args = list(x_segs) + [g.reshape(1, d), mod, mod, w.astype(BF16)]
    q_tiles = qk_tiles = 0
    if rope:
        cos, sin = rope_tables
        tiles_per_seq = s_len // tm
        in_specs += [pl.BlockSpec((tm, LANES), lambda i, j: (i % tiles_per_seq, 0))] * 2
        args += [cos, sin]
        q_tiles = (n // 3) // tn
        qk_tiles = 2 * q_tiles
    kern = functools.partial(_proj_kernel, x_bounds=x_bounds, tile_off=tile_off, rope=rope,
                             q_tiles=q_tiles, qk_tiles=qk_tiles, q_scale=q_scale)
    return pl.pallas_call(
        kern,
        grid=(rows // tm, n // tn),
        in_specs=in_specs,
        out_specs=pl.BlockSpec((tm, tn), lambda i, j: (i, j)),
        out_shape=jax.ShapeDtypeStruct((rows, n), out_dtype),
        scratch_shapes=[pltpu.VMEM((tm, d), BF16)],
        compiler_params=_params("arbitrary", "arbitrary"),
        name="norm_mod_project",
    )(*args)


def _hgrn_group_local(groups):
    c, m = HGRN_CHUNK, HGRN_SUB
    nb = c // m
    t = lax.broadcasted_iota(jnp.int32, (c, c), 0)
    s = lax.broadcasted_iota(jnp.int32, (c, c), 1)

    jobs = []
    for q, z, v, lb, direction in groups:
        dk = q.shape[1]
        tri = (s <= t) if direction == 0 else (s >= t)
        cum = jnp.where(tri, 1.0, 0.0).astype(BF16)
        f = lb + (1.0 - lb) * jax.nn.sigmoid(z)
        g2 = jnp.log(f) * LOG2_E
        k = 1.0 - f
        g_hi = g2.astype(BF16)
        g_lo = (g2 - g_hi.astype(F32)).astype(BF16)
        v_bf = v.astype(BF16)
        for ci in range(q.shape[0] // c):
            rows = slice(ci * c, (ci + 1) * c)
            jobs.append(dict(direction=direction, tri=tri, cum=cum, q=q[rows], k=k[rows],
                             v=v_bf[rows], g=jnp.concatenate([g_hi[rows], g_lo[rows]], axis=1)))

    def padded(x, lo, hi):
        parts = [jnp.zeros((lo, dk), BF16)] if lo else []
        parts.append(x)
        if hi < c:
            parts.append(jnp.zeros((c - hi, dk), BF16))
        return jnp.concatenate(parts, axis=0) if len(parts) > 1 else x

    for job in jobs:
        job["sums"] = jnp.dot(job["cum"], job["g"], preferred_element_type=F32)

    for job in jobs:
        direction = job["direction"]
        sums = job["sums"]
        b = sums[:, :dk] + sums[:, dk:]
        anchors = []
        for i in range(nb):
            a_row = i * m + (m // 2 - 1 if direction == 0 else m // 2)
            anchors.append(b[a_row:a_row + 1, :])
        b_anchor = jnp.concatenate([jnp.broadcast_to(r, (m, dk)) for r in anchors], axis=0)
        q_rel = (job["q"] * jnp.exp2(b - b_anchor)).astype(BF16)
        full = nb - 1 if direction == 0 else 0
        q_cat, k_cat = [], []
        k_full = None
        for i in range(nb):
            lo, hi = (0, m * (i + 1)) if direction == 0 else (m * i, c)
            ki = job["k"][lo:hi] * jnp.exp2(anchors[i] - b[lo:hi])
            if i == full:
                k_full = ki
            k_cat.append(padded(ki.astype(BF16), lo, hi))
            q_cat.append(padded(q_rel[m * i:m * (i + 1)], m * i, m * (i + 1)))
        job["a"] = lax.dot_general(jnp.concatenate(q_cat, axis=1),
                                   jnp.concatenate(k_cat, axis=1), NT_DIMS,
                                   preferred_element_type=F32)
        edge = c - 1 if direction == 0 else 0
        total = b[edge:edge + 1, :]
        job["k_dec"] = (k_full * jnp.exp2(total - anchors[full])).astype(BF16)
        job["q_in"] = (job["q"] * jnp.exp2(b)).astype(BF16)
        job["decay"] = jnp.exp2(total)

    for job in jobs:
        job["u_t"] = lax.dot_general(job["v"], job["k_dec"], TN_DIMS,
                                     preferred_element_type=F32)

    out, pos = [], 0
    for q, *_ in groups:
        n = q.shape[0] // c
        out.append([(jnp.where(j["tri"], j["a"], 0.0).astype(BF16), j["v"], j["q_in"], j["u_t"],
                     j["decay"]) for j in jobs[pos:pos + n]])
        pos += n
    return out


def _hgrn_kernel(*refs, seq_len, layer_j, n_lb, has_s0, emit_state):
    q_ref, zf_ref, zb_ref, i_ref, g_ref, lg_ref, on_ref = refs[:7]
    pos = 7
    s0_ref = st_ref = None
    if has_s0:
        s0_ref = refs[pos]
        pos += 1
    o_ref = refs[pos]
    pos += 1
    if emit_state:
        st_ref = refs[pos]
        pos += 1
    of_scr, ob_scr = refs[pos], refs[pos + 1]

    grp = math.gcd(HGRN_GROUP, seq_len // HGRN_CHUNK)
    rows = grp * HGRN_CHUNK
    n = seq_len // rows
    dv = q_ref.shape[1]

    lbs = []
    for d in range(2):
        lg = lg_ref[d * n_lb:(d + 1) * n_lb, :]
        e = jnp.exp(lg - jnp.max(lg, axis=0, keepdims=True))
        lbs.append(jnp.sum(e[:layer_j + 1], axis=0, keepdims=True) / jnp.sum(e, axis=0, keepdims=True))

    if has_s0:
        st0 = (s0_ref[0].T, s0_ref[1].T)
    else:
        st0 = (jnp.zeros((dv, dv), F32), jnp.zeros((dv, dv), F32))

    def advance(local, st):
        _, _, q_in, u_t, decay = local
        o = lax.dot_general(q_in, st.astype(BF16), NT_DIMS, preferred_element_type=F32)
        return st * decay + u_t, o

    def add_intra(local, o_inter):
        a, v = local[:2]
        return o_inter + jnp.dot(a, v, preferred_element_type=F32)

    def body(gi, carry):
        st_f, st_b = carry
        rf = pl.ds(pl.multiple_of(gi * rows, rows), rows)
        rb = pl.ds(pl.multiple_of((n - 1 - gi) * rows, rows), rows)
        loc_f, loc_b = _hgrn_group_local([
            (q_ref[rf, :], zf_ref[rf, :], i_ref[rf, :], lbs[0], 0),
            (q_ref[rb, :], zb_ref[rb, :], i_ref[rb, :], lbs[1], 1)])
        o_f, o_b = [None] * grp, [None] * grp
        for ci in range(grp):
            st_f, o_f[ci] = advance(loc_f[ci], st_f)
            st_b, o_b[grp - 1 - ci] = advance(loc_b[grp - 1 - ci], st_b)
        o_f = [add_intra(l, o) for l, o in zip(loc_f, o_f)]
        o_b = [add_intra(l, o) for l, o in zip(loc_b, o_b)]
        of_scr[rf, :] = jnp.concatenate(o_f, axis=0)
        ob_scr[rb, :] = jnp.concatenate(o_b, axis=0)
        return st_f, st_b

    st_f, st_b = lax.fori_loop(0, n, body, st0)
    if emit_state:
        st_ref[0] = st_f.T
        st_ref[1] = st_b.T

    fin_rows = math.gcd(seq_len, 256)
    onorm = on_ref[...]

    def finish(ri, _):
        r = pl.ds(pl.multiple_of(ri * fin_rows, fin_rows), fin_rows)
        o = of_scr[r, :] + ob_scr[r, :]
        y = o * lax.rsqrt(jnp.mean(o * o, axis=-1, keepdims=True) + EPS) * onorm
        o_ref[r, :] = (y * _silu(g_ref[r, :])).astype(o_ref.dtype)
        return 0

    lax.fori_loop(0, seq_len // fin_rows, finish, 0)


def _hgrn_scan(proj, lb_logits, onorm, s0, *, layer_j, n_seq, seq_len, row_off, heads, emit_state):
    dk = LANES
    d = heads * dk
    blk_off = row_off // seq_len
    n_lb = lb_logits.shape[1]

    def sec(k):
        return pl.BlockSpec((seq_len, dk), lambda b, h: (b + blk_off, k * heads + h))

    in_specs = [sec(0), sec(1), sec(2), sec(3), sec(4),
                pl.BlockSpec((2 * n_lb, dk), lambda b, h: (0, h)),
                pl.BlockSpec((1, dk), lambda b, h: (0, 0))]
    args = [proj] * 5 + [lb_logits.reshape(2 * n_lb, d), onorm.reshape(1, dk)]
    if s0 is not None:
        in_specs.append(pl.BlockSpec((None, None, 2, None, dk, dk),
                                     lambda b, h: (b, layer_j, 0, h, 0, 0)))
        args.append(s0)
    out_shape = [jax.ShapeDtypeStruct((n_seq * seq_len, d), BF16)]
    out_specs = [pl.BlockSpec((seq_len, dk), lambda b, h: (b, h))]
    if emit_state:
        out_shape.append(jax.ShapeDtypeStruct((n_seq, 2, heads, dk, dk), F32))
        out_specs.append(pl.BlockSpec((None, 2, None, dk, dk), lambda b, h: (b, 0, h, 0, 0)))
    kern = functools.partial(_hgrn_kernel, seq_len=seq_len, layer_j=layer_j, n_lb=n_lb,
                             has_s0=s0 is not None, emit_state=emit_state)
    return pl.pallas_call(
        kern,
        grid=(n_seq, heads),
        in_specs=in_specs,
        out_specs=out_specs,
        out_shape=out_shape,
        scratch_shapes=[pltpu.VMEM((seq_len, dk), F32), pltpu.VMEM((seq_len, dk), F32)],
        compiler_params=_params("arbitrary", "arbitrary"),
        name="hgrn_scan",
    )(*args)


def _attn_kernel(*refs, has_cache, lam_init, q_scale):
    q_ref, k_ref, v_ref = refs[:3]
    pos = 3
    kc_ref = vc_ref = None
    if has_cache:
        kc_ref, vc_ref = refs[3], refs[4]
        pos = 5
    lam_ref, sub_ref, o_ref = refs[pos:pos + 3]

    lp = lam_ref[...]
    lam = (jnp.exp(jnp.sum(lp[0:1] * lp[1:2], axis=-1, keepdims=True))
           - jnp.exp(jnp.sum(lp[2:3] * lp[3:4], axis=-1, keepdims=True)) + lam_init)

    hd = LANES
    v_new = v_ref[...].astype(BF16)
    v_old = vc_ref[...].astype(BF16) if has_cache else None
    tq = q_ref.shape[0]
    rb = min(tq, ATTN_ROW_BLOCK)
    chains = [(r, j) for r in range(0, tq, rb) for j in range(2)]
    scores = []
    for r, j in chains:
        cols = slice(j * hd, (j + 1) * hd)
        qj = q_ref[r:r + rb, cols]
        if q_scale != 1.0:
            qj = qj.astype(F32) * q_scale
        qj = qj.astype(BF16)
        s_new = lax.dot_general(qj, k_ref[:, cols].astype(BF16), NT_DIMS,
                                preferred_element_type=F32)
        s_old = None
        if has_cache:
            s_old = lax.dot_general(qj, kc_ref[:, cols].astype(BF16), NT_DIMS,
                                    preferred_element_type=F32)
        scores.append((s_new, s_old))
    outs = {}
    for (r, j), (s_new, s_old) in zip(chains, scores):
        mx = jnp.max(s_new, axis=-1, keepdims=True)
        if has_cache:
            mx = jnp.maximum(mx, jnp.max(s_old, axis=-1, keepdims=True))
        p_new = jnp.exp2(s_new - mx)
        den = jnp.sum(p_new, axis=-1, keepdims=True)
        acc = jnp.dot(p_new.astype(BF16), v_new, preferred_element_type=F32)
        if has_cache:
            p_old = jnp.exp2(s_old - mx)
            den += jnp.sum(p_old, axis=-1, keepdims=True)
            acc += jnp.dot(p_old.astype(BF16), v_old, preferred_element_type=F32)
        outs[r, j] = acc / den
    sub = sub_ref[...]
    for r in range(0, tq, rb):
        o = outs[r, 0] - lam * outs[r, 1]
        y = o * lax.rsqrt(jnp.mean(o * o, axis=-1, keepdims=True) + EPS) * sub
        o_ref[r:r + rb, :] = (y * (1.0 - lam_init)).astype(o_ref.dtype)


def _diff_attention(qkv, cache_k, cache_v, lam_p, subln, *, n_seq, seq_len, heads, lam_init,
                    q_scale, tq):
    hd2 = 2 * LANES
    d = heads * hd2
    qb = seq_len // tq
    in_specs = [
        pl.BlockSpec((tq, hd2), lambda b, h, i: (b * qb + i, h)),
        pl.BlockSpec((seq_len, hd2), lambda b, h, i: (b, heads + h)),
        pl.BlockSpec((seq_len, hd2), lambda b, h, i: (b, 2 * heads + h)),
    ]
    args = [qkv, qkv, qkv]
    if cache_k is not None:
        past = cache_k.shape[1]
        in_specs += [pl.BlockSpec((None, past, hd2), lambda b, h, i: (b, 0, h))] * 2
        args += [cache_k, cache_v]
    in_specs += [pl.BlockSpec(lam_p.shape, lambda b, h, i: (0, 0)),
                 pl.BlockSpec((1, hd2), lambda b, h, i: (0, 0))]
    args += [lam_p, subln.reshape(1, hd2)]
    kern = functools.partial(_attn_kernel, has_cache=cache_k is not None, lam_init=lam_init,
                             q_scale=q_scale)
    return pl.pallas_call(
        kern,
        grid=(n_seq, heads, qb),
        in_specs=in_specs,
        out_specs=pl.BlockSpec((tq, hd2), lambda b, h, i: (b * qb + i, h)),
        out_shape=jax.ShapeDtypeStruct((n_seq * seq_len, d), BF16),
        compiler_params=_params("arbitrary", "arbitrary", "arbitrary"),
        name="diff_attention",
    )(*args)


def _out_proj_kernel(*refs, a_bounds, x_bounds):
    a_refs = refs[:len(a_bounds)]
    x_refs = refs[len(a_bounds):len(a_bounds) + len(x_bounds)]
    w_ref, gate_ref, o_ref = refs[len(a_bounds) + len(x_bounds):]
    i = pl.program_id(0)
    for a_ref, (a_lo, a_cnt) in zip(a_refs, a_bounds):
        for x_ref, (x_lo, x_cnt) in zip(x_refs, x_bounds):
            lo, hi = max(a_lo, x_lo), min(a_lo + a_cnt, x_lo + x_cnt)
            if lo >= hi:
                continue

            @pl.when((i >= lo) & (i < hi))
            def _(a_ref=a_ref, x_ref=x_ref):
                y = jnp.dot(a_ref[...], w_ref[...], preferred_element_type=F32)
                o_ref[...] = x_ref[...] + gate_ref[...] * y


def _out_project(a_segs, w, x_segs, mod, gate_chunk, *, p_rows, s_len, tm=512):
    d = x_segs[0].shape[1]
    t = sum(x.shape[0] for x in x_segs)
    kdim = w.shape[0]
    a_specs, a_bounds = _row_segment_specs(a_segs, tm)
    x_specs, x_bounds = _row_segment_specs(x_segs, tm)
    in_specs = a_specs + x_specs + [
        pl.BlockSpec((kdim, d), lambda i, j: (0, 0)),
        _mod_spec(gate_chunk, d, tm, p_rows, s_len),
    ]
    return pl.pallas_call(
        functools.partial(_out_proj_kernel, a_bounds=a_bounds, x_bounds=x_bounds),
        grid=(t // tm, 1),
        in_specs=in_specs,
        out_specs=pl.BlockSpec((tm, d), lambda i, j: (i, 0)),
        out_shape=jax.ShapeDtypeStruct((t, d), F32),
        compiler_params=_params("arbitrary", "arbitrary"),
        name="out_project_residual",
    )(*a_segs, *x_segs, w, mod)


def _ffn_kernel(x_ref, xp_ref, xn_ref, g_ref, sh_ref, sc_ref, gate_ref, wg_ref, wv_ref,
                conv_a, wd_a, conv_b, wd_b,
                o_ref, h_ref, act_ref, ug_scr, uv_scr, acc_ref,
                *, tm, nj, p_rows, p_len, s_len):
    i = pl.program_id(0)
    j = pl.program_id(1)
    hs = xp_ref.shape[0]
    sub = FFN_TILE
    ext = tm + 2 * hs
    row0 = i * tm
    in_prompt = row0 < p_rows
    inner_edges = jnp.where(in_prompt, 1.0, 0.0)
    win = lax.broadcasted_iota(jnp.int32, (2 * SUBLANES, 1), 0)

    def up(cols):
        h = h_ref[...]
        return (jnp.dot(h, wg_ref[:, cols], preferred_element_type=F32),
                jnp.dot(h, wv_ref[:, cols], preferred_element_type=F32))

    def conv(u_all, u_scr, cw, cb):
        u_scr[...] = u_all
        u = u_all[hs:hs + tm]
        prev = u_scr[hs - 1:hs - 1 + tm, :]
        nxt = u_scr[hs + 1:hs + 1 + tm, :]
        y = prev * cw[0:1] + u * cw[1:2] + nxt * cw[2:3] + cb
        pieces, done = [], 0
        for edge in range(p_len, tm, p_len):
            lo, hi = edge - SUBLANES, edge + SUBLANES
            leak = (jnp.where(win == SUBLANES - 1, nxt[lo:hi] * cw[2:3], 0.0)
                    + jnp.where(win == SUBLANES, prev[lo:hi] * cw[0:1], 0.0))
            pieces += [y[done:lo], y[lo:hi] - inner_edges * leak]
            done = hi
        return jnp.concatenate(pieces + [y[done:]], axis=0) if pieces else y

    def activate(ug, uv, conv_ref):
        p = conv_ref[...]
        gate = conv(ug, ug_scr, p[0:3], p[3:4])
        val = conv(uv, uv_scr, p[4:7], p[7:8])
        return (_silu(gate) * val).astype(BF16)

    def activate_a(ug, uv):
        return activate(ug, uv, conv_a)

    def project_down(act, wd):
        acc_ref[...] += jnp.dot(act, wd[...], preferred_element_type=F32)

    cols_a, cols_b = slice(0, sub), slice(sub, 2 * sub)

    @pl.when(j == 0)
    def _():
        g, sh, sc = g_ref[...], sh_ref[...], sc_ref[...]
        h_ref[hs:hs + tm, :] = _norm_mod(x_ref[...], g, sh, sc).astype(BF16)
        end = row0 + tm
        at_seq_end = jnp.where(in_prompt, lax.rem(end, p_len) == 0,
                               lax.rem(end - p_rows, s_len) == 0)
        at_seq_start = jnp.where(in_prompt, lax.rem(row0, p_len) == 0,
                                 lax.rem(row0 - p_rows, s_len) == 0)
        after = _norm_mod(xn_ref[...], g, sh, sc) * jnp.where(at_seq_end, 0.0, 1.0)
        before = _norm_mod(xp_ref[...], g, sh, sc) * jnp.where(at_seq_start, 0.0, 1.0)
        h_ref[0:hs, :] = before.astype(BF16)
        h_ref[hs + tm:ext, :] = after.astype(BF16)
        acc_ref[...] = jnp.zeros_like(acc_ref)
        ua = up(cols_a)
        act_a = activate_a(*ua)
        ub = up(cols_b)
        project_down(act_a, wd_a)
        act_ref[...] = activate(*ub, conv_b)

    @pl.when((j > 0) & (j < nj))
    def _():
        ua = up(cols_a)
        project_down(act_ref[...], wd_b)
        act_a = activate_a(*ua)
        ub = up(cols_b)
        project_down(act_a, wd_a)
        act_ref[...] = activate(*ub, conv_b)

    @pl.when(j == nj)
    def _():
        project_down(act_ref[...], wd_b)
        o_ref[...] = x_ref[...] + gate_ref[...] * acc_ref[...]


def _ffn(x, g, mod, shift_chunk, scale_chunk, gate_chunk, w_up, cw, cb, w_down, *,
         p_rows, p_len, s_len, tm=512):
    t, d = x.shape
    ff = w_up.shape[1] // 2
    sub = FFN_TILE
    assert ff % (2 * sub) == 0
    assert (tm % p_len == 0 or p_len % tm == 0) and p_rows % tm == 0 and s_len % tm == 0
    nj = ff // (2 * sub)
    nt = ff // sub
    hs = 16
    per = tm // hs
    last_blk = t // hs - 1

    def tile_a(i, j):
        return jnp.minimum(2 * j, nt - 2)

    def tile_b(i, j):
        return jnp.minimum(2 * j + 1, nt - 1)

    def parked_b(i, j):
        return jnp.maximum(2 * j - 1, 1)

    def tile_specs(conv_tile, down_tile):
        return [
            pl.BlockSpec((None, 2 * (taps + 1), sub), lambda i, j: (conv_tile(i, j), 0, 0)),
            pl.BlockSpec((sub, d), lambda i, j: (down_tile(i, j), 0)),
        ]

    taps = cw.shape[0]
    conv_p = jnp.concatenate([cw[:, :ff], cb[None, :ff], cw[:, ff:], cb[None, ff:]], axis=0)
    conv_p = conv_p.reshape(2 * (taps + 1), nt, sub).transpose(1, 0, 2)

    in_specs = [
        pl.BlockSpec((tm, d), lambda i, j: (i, 0)),
        pl.BlockSpec((hs, d), lambda i, j: (jnp.maximum(i * per - 1, 0), 0)),
        pl.BlockSpec((hs, d), lambda i, j: (jnp.minimum((i + 1) * per, last_blk), 0)),
        pl.BlockSpec((1, d), lambda i, j: (0, 0)),
        _mod_spec(shift_chunk, d, tm, p_rows, s_len),
        _mod_spec(scale_chunk, d, tm, p_rows, s_len),
        _mod_spec(gate_chunk, d, tm, p_rows, s_len),
        pl.BlockSpec((None, d, 2 * sub), lambda i, j: (jnp.minimum(j, nj - 1), 0, 0)),
        pl.BlockSpec((None, d, 2 * sub), lambda i, j: (jnp.minimum(j, nj - 1) + nj, 0, 0)),
    ] + tile_specs(tile_a, tile_a) + tile_specs(tile_b, parked_b)
    w_up = _column_tiles(w_up, 2 * sub)
    tile_args = [conv_p, w_down.astype(BF16)]
    kern = functools.partial(_ffn_kernel, tm=tm, nj=nj, p_rows=p_rows, p_len=p_len,
                             s_len=s_len)
    return pl.pallas_call(
        kern,
        grid=(t // tm, nj + 1),
        in_specs=in_specs,
        out_specs=pl.BlockSpec((tm, d), lambda i, j: (i, 0)),
        out_shape=jax.ShapeDtypeStruct((t, d), F32),
        scratch_shapes=[pltpu.VMEM((tm + 2 * hs, d), BF16),
                        pltpu.VMEM((tm, sub), BF16),
                        pltpu.VMEM((tm + 2 * hs, sub), F32),
                        pltpu.VMEM((tm + 2 * hs, sub), F32),
                        pltpu.VMEM((tm, d), F32)],
        compiler_params=_params("arbitrary", "arbitrary"),
        name="conv_ffn",
    )(x, x, x, g.reshape(1, d), mod, mod, mod, w_up, w_up, *tile_args, *tile_args)


def _final_norm_kernel(x_ref, g_ref, o_ref):
    x = x_ref[...]
    o_ref[...] = x * lax.rsqrt(jnp.mean(x * x, axis=-1, keepdims=True) + EPS) * g_ref[...]


def _final_norm(x, g, *, row_off, rows, tm=512):
    d = x.shape[1]
    off = row_off // tm
    return pl.pallas_call(
        _final_norm_kernel,
        grid=(rows // tm,),
        in_specs=[pl.BlockSpec((tm, d), lambda i: (i + off, 0)),
                  pl.BlockSpec((1, d), lambda i: (0, 0))],
        out_specs=pl.BlockSpec((tm, d), lambda i: (i, 0)),
        out_shape=jax.ShapeDtypeStruct((rows, d), F32),
        compiler_params=_params("arbitrary"),
        name="final_norm",
    )(x, g.reshape(1, d))


def _rope_tables(seq_len, hd):
    pos = jnp.arange(seq_len)
    row = (pos // GRID_W).astype(F32)
    col = (pos % GRID_W).astype(F32)
    nf = hd // 4
    inv = ROPE_BASE ** (-jnp.arange(nf, dtype=F32) / nf)
    ar = row[:, None] * inv
    ac = col[:, None] * inv
    cos = jnp.concatenate([jnp.cos(ar), jnp.cos(ar), jnp.cos(ac), jnp.cos(ac)], axis=1)
    sin = jnp.concatenate([-jnp.sin(ar), jnp.sin(ar), -jnp.sin(ac), jnp.sin(ac)], axis=1)
    return cos, sin


def kernel(x_prompt, x_sample, c, cache_hgrn_state, cache_attn_k, cache_attn_v, c_ctx, w_mod, b_mod, norm_mix, norm_ffn, w_hgrn_in, hgrn_lb_logits, hgrn_onorm, w_hgrn_out, w_attn_in, attn_lambda, attn_subln, w_attn_out, w_ffn_up, ffn_conv_w, ffn_conv_b, w_ffn_down, norm_final):
    batch, p_len, d = x_prompt.shape
    dec_batch, s_len, _ = x_sample.shape
    depth = w_mod.shape[0]
    p_rows = batch * p_len
    s_rows = dec_batch * s_len
    hgrn_heads = d // LANES
    diff_heads = d // (2 * LANES)
    hd = LANES
    assert 1 + dec_batch <= MOD_ROWS and p_rows % s_len == 0

    x = [x_prompt.reshape(p_rows, d), x_sample.reshape(s_rows, d)]
    cvec = jnp.concatenate(
        [c_ctx[None, :], c, jnp.zeros((MOD_ROWS - 1 - dec_batch, d), F32)], axis=0)
    mods = _modulation(cvec, w_mod, b_mod)
    seg = dict(p_rows=p_rows, s_len=s_len)

    hgrn_states, attn_ks, attn_vs = [], [], []
    for l in range(depth):
        mod = mods[l][:, None, :]
        j = l // N_MIXERS
        if l % N_MIXERS == 0:
            proj = _project(x, norm_mix[l], mod, 0, 1, w_hgrn_in[j],
                            row_off=0, rows=p_rows + s_rows, out_dtype=F32, tm=1024, tn=512,
                            **seg)
            mix_p, st = _hgrn_scan(proj, hgrn_lb_logits, hgrn_onorm[j], None, layer_j=j,
                                   n_seq=batch, seq_len=p_len, row_off=0, heads=hgrn_heads,
                                   emit_state=True)
            hgrn_states.append(st)
            mix_s, = _hgrn_scan(proj, hgrn_lb_logits, hgrn_onorm[j], cache_hgrn_state,
                                layer_j=j, n_seq=dec_batch, seq_len=s_len, row_off=p_rows,
                                heads=hgrn_heads, emit_state=False)
            w_out = w_hgrn_out[j]
        else:
            lam_init = 0.8 - 0.6 * math.exp(-0.3 * l)
            q_scale = hd ** -0.5 * LOG2_E
            w_in = w_attn_in[j]
            qkv_p = _project(x, norm_mix[l], mod, 0, 1, w_in, row_off=0, rows=p_rows,
                             out_dtype=F32, tn=1024, **seg)
            attn_ks.append(qkv_p[:, d:2 * d].reshape(batch, p_len, 2 * diff_heads, hd))
            attn_vs.append(qkv_p[:, 2 * d:].reshape(batch, p_len, diff_heads, 2 * hd))
            mix_p = _diff_attention(qkv_p, None, None, attn_lambda[j], attn_subln[j],
                                    n_seq=batch, seq_len=p_len, heads=diff_heads,
                                    lam_init=lam_init, q_scale=q_scale, tq=p_len)
            qkv_s = _project(x, norm_mix[l], mod, 0, 1, w_in, row_off=p_rows, rows=s_rows,
                             out_dtype=BF16, rope_tables=_rope_tables(s_len, hd),
                             q_scale=q_scale, tn=1024, **seg)
            past = cache_attn_k.shape[2]
            mix_s = _diff_attention(qkv_s, cache_attn_k[:, j].reshape(dec_batch, past, d),
                                    cache_attn_v[:, j].reshape(dec_batch, past, d),
                                    attn_lambda[j], attn_subln[j], n_seq=dec_batch,
                                    seq_len=s_len, heads=diff_heads, lam_init=lam_init,
                                    q_scale=1.0, tq=512)
            w_out = w_attn_out[j]
        x = _out_project([mix_p, mix_s], w_out.astype(BF16), x, mod, 2, **seg)
        x = [_ffn(x, norm_ffn[l], mod, 3, 4, 5, w_ffn_up[l], ffn_conv_w[l], ffn_conv_b[l],
                 w_ffn_down[l], p_len=p_len, **seg)]

    x, = x
    y_prompt = _final_norm(x, norm_final, row_off=0, rows=p_rows).reshape(batch, p_len, d)
    y_sample = _final_norm(x, norm_final, row_off=p_rows, rows=s_rows).reshape(dec_batch, s_len, d)
    new_hgrn_state = jnp.stack(hgrn_states, axis=1)
    new_attn_k = jnp.stack(attn_ks, axis=1)
    new_attn_v = jnp.stack(attn_vs, axis=1)
    return (y_prompt, y_sample, new_hgrn_state, new_attn_k, new_attn_v)
```

```python
import functools
import math

import jax
import jax.numpy as jnp
from jax import lax
from jax.experimental import pallas as pl
from jax.experimental.pallas import tpu as pltpu

F32 = jnp.float32
BF16 = jnp.bfloat16

EPS = 1e-6
GRID_W = 64
ROPE_BASE = 10000.0
N_MIXERS = 2

LANES = 128
SUBLANES = 8
MXU_COLS = 256
MOD_ROWS = 16
HGRN_CHUNK = 64
HGRN_SUB = 16
HGRN_GROUP = 16
ATTN_ROW_BLOCK = 128
FFN_TILE = 256
LOG2_E = 1.4426950408889634
VMEM_LIMIT = 56 * 1024 * 1024

NT_DIMS = (((1,), (1,)), ((), ()))
TN_DIMS = (((0,), (0,)), ((), ()))


def _params(*sem):
    return pltpu.CompilerParams(dimension_semantics=sem, vmem_limit_bytes=VMEM_LIMIT)


def _tile(n, want):
    best = LANES
    for cand in range(LANES, min(n, want) + 1, LANES):
        if n % cand == 0:
            best = cand
    assert n % best == 0
    return best


def _column_tiles(w, tn):
    k, n = w.shape
    return w.astype(BF16).reshape(k, n // tn, tn).transpose(1, 0, 2)


def _silu(x):
    return x * jax.nn.sigmoid(x)


def _split_bf16(x):
    hi = x.astype(BF16)
    lo = (x - hi.astype(F32)).astype(BF16)
    return hi, lo


def _mod_kernel(c_ref, w_ref, b_ref, o_ref):
    a_hi, a_lo = _split_bf16(_silu(c_ref[...]))
    w_hi, w_lo = _split_bf16(w_ref[...])
    acc = jnp.dot(a_hi, w_hi, preferred_element_type=F32)
    acc += jnp.dot(a_lo, w_hi, preferred_element_type=F32)
    acc += jnp.dot(a_hi, w_lo, preferred_element_type=F32)
    o_ref[...] = acc + b_ref[...]


def _modulation(cvec, w_mod, b_mod, tn=512):
    depth, d, n = w_mod.shape
    tn = _tile(n, tn)
    return pl.pallas_call(
        _mod_kernel,
        grid=(depth, n // tn),
        in_specs=[
            pl.BlockSpec((MOD_ROWS, d), lambda l, j: (0, 0)),
            pl.BlockSpec((None, d, tn), lambda l, j: (l, 0, j)),
            pl.BlockSpec((None, 1, tn), lambda l, j: (l, 0, j)),
        ],
        out_specs=pl.BlockSpec((None, MOD_ROWS, tn), lambda l, j: (l, 0, j)),
        out_shape=jax.ShapeDtypeStruct((depth, MOD_ROWS, n), F32),
        compiler_params=_params("arbitrary", "arbitrary"),
        name="modulation",
    )(cvec, w_mod, b_mod.reshape(depth, 1, n))


def _norm_mod(x, g, shift, scale):
    y = x * lax.rsqrt(jnp.mean(x * x, axis=-1, keepdims=True) + EPS) * g
    return y * (1.0 + scale) + shift


def _seq_of_row(row0, p_rows, s_len):
    return jnp.where(row0 < p_rows, 0, 1 + (row0 - p_rows) // s_len)


def _mod_spec(chunk, d, tm, p_rows, s_len, tile_off=0):
    return pl.BlockSpec(
        (None, 1, d),
        lambda i, j: (_seq_of_row((i + tile_off) * tm, p_rows, s_len), 0, chunk))


def _proj_kernel(*refs, x_bounds, tile_off, rope, q_tiles, qk_tiles, q_scale, split):
    x_refs = refs[:len(x_bounds)]
    g_ref, sh_ref, sc_ref, w_ref, *rest = refs[len(x_bounds):]
    if rope:
        cos_ref, sin_ref, o_ref, h_ref = rest
    elif split:
        *o_refs, h_ref = rest
    else:
        o_ref, h_ref = rest
    tile = pl.program_id(0) + tile_off
    j = pl.program_id(1)

    for x_ref, (lo, cnt) in zip(x_refs, x_bounds):
        @pl.when((j == 0) & (tile >= lo) & (tile < lo + cnt))
        def _(x_ref=x_ref):
            h_ref[...] = _norm_mod(x_ref[...], g_ref[...], sh_ref[...],
                                   sc_ref[...]).astype(BF16)

    if split:
        per = pl.num_programs(1) // split
        for s, out in enumerate(o_refs):
            @pl.when((j >= s * per) & (j < (s + 1) * per))
            def _(out=out):
                out[...] = jnp.dot(h_ref[...], w_ref[...],
                                   preferred_element_type=F32).astype(out.dtype)
        return

    if not rope:
        o_ref[...] = jnp.dot(h_ref[...], w_ref[...],
                             preferred_element_type=F32).astype(o_ref.dtype)
        return

    @pl.when(j < qk_tiles)
    def _():
        tn = o_ref.shape[1]
        sub = min(tn, MXU_COLS)
        h = h_ref[...]
        accs = [jnp.dot(h, w_ref[:, c:c + sub], preferred_element_type=F32)
                for c in range(0, tn, sub)]
        scale = jnp.where(j < q_tiles, q_scale, 1.0)
        cos = cos_ref[...] * scale
        sin = sin_ref[...] * scale
        lane = lax.broadcasted_iota(jnp.int32, cos.shape, 1)
        first_of_pair = (lane // (LANES // 4)) % 2 == 0
        for ci, acc in enumerate(accs):
            for s in range(sub // LANES):
                xs = acc[:, s * LANES:(s + 1) * LANES]
                partner = jnp.where(first_of_pair,
                                    pltpu.roll(xs, LANES - LANES // 4, 1),
                                    pltpu.roll(xs, LANES // 4, 1))
                col = ci * sub + s * LANES
                o_ref[:, col:col + LANES] = (xs * cos + partner * sin).astype(o_ref.dtype)

    @pl.when(j >= qk_tiles)
    def _():
        o_ref[...] = jnp.dot(h_ref[...], w_ref[...],
                             preferred_element_type=F32).astype(o_ref.dtype)


def _row_segment_specs(segs, tm, tile_off=0, buffers=None):
    specs, bounds, lo = [], [], 0
    mode = {} if buffers is None else dict(pipeline_mode=pl.Buffered(buffers))
    for a in segs:
        cnt = a.shape[0] // tm
        specs.append(pl.BlockSpec(
            (tm, a.shape[1]),
            lambda i, j, lo=lo, cnt=cnt: (jnp.clip(i + tile_off - lo, 0, cnt - 1), 0), **mode))
        bounds.append((lo, cnt))
        lo += cnt
    return specs, tuple(bounds)


def _project(x_segs, g, mod, shift_chunk, scale_chunk, w, *, row_off, rows, out_dtype,
             p_rows, s_len, tm=512, tn=512, rope_tables=None, q_scale=1.0, split=0):
    d = x_segs[0].shape[1]
    n = w.shape[1]
    tm = math.gcd(tm, p_rows, s_len)
    tile_off = row_off // tm
    rope = rope_tables is not None
    tn = _tile(n // 3 if rope else n // max(split, 1), tn)
    x_specs, x_bounds = _row_segment_specs(x_segs, tm, tile_off,
                                           buffers=1 if len(x_segs) > 1 else None)
    in_specs = x_specs + [
        pl.BlockSpec((1, d), lambda i, j: (0, 0)),
        _mod_spec(shift_chunk, d, tm, p_rows, s_len, tile_off),
        _mod_spec(scale_chunk, d, tm, p_rows, s_len, tile_off),
        pl.BlockSpec((d, tn), lambda i, j: (0, j)),
    ]
    args = list(x_segs) + [g.reshape(1, d), mod, mod, w.astype(BF16)]
    q_tiles = qk_tiles = 0
    if rope:
        cos, sin = rope_tables
        tiles_per_seq = s_len // tm
        in_specs += [pl.BlockSpec((tm, LANES), lambda i, j: (i % tiles_per_seq, 0))] * 2
        args += [cos, sin]
        q_tiles = (n // 3) // tn
        qk_tiles = 2 * q_tiles
    kern = functools.partial(_proj_kernel, x_bounds=x_bounds, tile_off=tile_off, rope=rope,
                             q_tiles=q_tiles, qk_tiles=qk_tiles, q_scale=q_scale, split=split)
    if split:
        per = n // tn // split
        out_specs = [pl.BlockSpec((tm, tn),
                                  lambda i, j, s=s: (i, jnp.clip(j - s * per, 0, per - 1)))
                     for s in range(split)]
        out_shape = [jax.ShapeDtypeStruct((rows, n // split), out_dtype)] * split
    else:
        out_specs = pl.BlockSpec((tm, tn), lambda i, j: (i, j))
        out_shape = jax.ShapeDtypeStruct((rows, n), out_dtype)
    return pl.pallas_call(
        kern,
        grid=(rows // tm, n // tn),
        in_specs=in_specs,
        out_specs=out_specs,
        out_shape=out_shape,
        scratch_shapes=[pltpu.VMEM((tm, d), BF16)],
        compiler_params=_params("arbitrary", "arbitrary"),
        name="norm_mod_project",
    )(*args)


def _hgrn_group_local(groups):
    c, m = HGRN_CHUNK, HGRN_SUB
    nb = c // m
    t = lax.broadcasted_iota(jnp.int32, (c, c), 0)
    s = lax.broadcasted_iota(jnp.int32, (c, c), 1)

    jobs = []
    for q, z, v, lb, direction in groups:
        dk = q.shape[1]
        tri = (s <= t) if direction == 0 else (s >= t)
        cum = jnp.where(tri, 1.0, 0.0).astype(BF16)
        f = lb + (1.0 - lb) * jax.nn.sigmoid(z)
        g2 = jnp.log(f) * LOG2_E
        k = 1.0 - f
        g_hi = g2.astype(BF16)
        g_lo = (g2 - g_hi.astype(F32)).astype(BF16)
        v_bf = v.astype(BF16)
        for ci in range(q.shape[0] // c):
            rows = slice(ci * c, (ci + 1) * c)
            jobs.append(dict(direction=direction, tri=tri, cum=cum, q=q[rows], k=k[rows],
                             v=v_bf[rows], g=jnp.concatenate([g_hi[rows], g_lo[rows]], axis=1)))

    def padded(x, lo, hi):
        parts = [jnp.zeros((lo, dk), BF16)] if lo else []
        parts.append(x)
        if hi < c:
            parts.append(jnp.zeros((c - hi, dk), BF16))
        return jnp.concatenate(parts, axis=0) if len(parts) > 1 else x

    for job in jobs:
        job["sums"] = jnp.dot(job["cum"], job["g"], preferred_element_type=F32)

    for job in jobs:
        direction = job["direction"]
        sums = job["sums"]
        b = sums[:, :dk] + sums[:, dk:]
        anchors = []
        for i in range(nb):
            a_row = i * m + (m // 2 - 1 if direction == 0 else m // 2)
            anchors.append(b[a_row:a_row + 1, :])
        b_anchor = jnp.concatenate([jnp.broadcast_to(r, (m, dk)) for r in anchors], axis=0)
        q_rel = (job["q"] * jnp.exp2(b - b_anchor)).astype(BF16)
        full = nb - 1 if direction == 0 else 0
        q_cat, k_cat = [], []
        k_full = None
        for i in range(nb):
            lo, hi = (0, m * (i + 1)) if direction == 0 else (m * i, c)
            ki = job["k"][lo:hi] * jnp.exp2(anchors[i] - b[lo:hi])
            if i == full:
                k_full = ki
            k_cat.append(padded(ki.astype(BF16), lo, hi))
            q_cat.append(padded(q_rel[m * i:m * (i + 1)], m * i, m * (i + 1)))
        job["a"] = lax.dot_general(jnp.concatenate(q_cat, axis=1),
                                   jnp.concatenate(k_cat, axis=1), NT_DIMS,
                                   preferred_element_type=F32)
        edge = c - 1 if direction == 0 else 0
        total = b[edge:edge + 1, :]
        job["k_dec"] = (k_full * jnp.exp2(total - anchors[full])).astype(BF16)
        job["q_in"] = (job["q"] * jnp.exp2(b)).astype(BF16)
        job["decay"] = jnp.exp2(total)

    for job in jobs:
        job["u_t"] = lax.dot_general(job["v"], job["k_dec"], TN_DIMS,
                                     preferred_element_type=F32)

    out, pos = [], 0
    for q, *_ in groups:
        n = q.shape[0] // c
        out.append([(jnp.where(j["tri"], j["a"], 0.0).astype(BF16), j["v"], j["q_in"], j["u_t"],
                     j["decay"]) for j in jobs[pos:pos + n]])
        pos += n
    return out


def _hgrn_kernel(*refs, seq_len, layer_j, n_lb, has_s0, emit_state):
    q_ref, zf_ref, zb_ref, i_ref, g_ref, lg_ref, on_ref = refs[:7]
    pos = 7
    s0_ref = st_ref = None
    if has_s0:
        s0_ref = refs[pos]
        pos += 1
    o_ref = refs[pos]
    pos += 1
    if emit_state:
        st_ref = refs[pos]
        pos += 1
    of_scr, ob_scr = refs[pos], refs[pos + 1]

    grp = math.gcd(HGRN_GROUP, seq_len // HGRN_CHUNK)
    rows = grp * HGRN_CHUNK
    n = seq_len // rows
    dv = q_ref.shape[1]

    lbs = []
    for d in range(2):
        lg = lg_ref[d * n_lb:(d + 1) * n_lb, :]
        e = jnp.exp(lg - jnp.max(lg, axis=0, keepdims=True))
        lbs.append(jnp.sum(e[:layer_j + 1], axis=0, keepdims=True) / jnp.sum(e, axis=0, keepdims=True))

    if has_s0:
        st0 = (s0_ref[0].T, s0_ref[1].T)
    else:
        st0 = (jnp.zeros((dv, dv), F32), jnp.zeros((dv, dv), F32))

    def advance(local, st):
        _, _, q_in, u_t, decay = local
        o = lax.dot_general(q_in, st.astype(BF16), NT_DIMS, preferred_element_type=F32)
        return st * decay + u_t, o

    def add_intra(local, o_inter):
        a, v = local[:2]
        return o_inter + jnp.dot(a, v, preferred_element_type=F32)

    def body(gi, carry):
        st_f, st_b = carry
        rf = pl.ds(pl.multiple_of(gi * rows, rows), rows)
        rb = pl.ds(pl.multiple_of((n - 1 - gi) * rows, rows), rows)
        loc_f, loc_b = _hgrn_group_local([
            (q_ref[rf, :], zf_ref[rf, :], i_ref[rf, :], lbs[0], 0),
            (q_ref[rb, :], zb_ref[rb, :], i_ref[rb, :], lbs[1], 1)])
        o_f, o_b = [None] * grp, [None] * grp
        for ci in range(grp):
            st_f, o_f[ci] = advance(loc_f[ci], st_f)
            st_b, o_b[grp - 1 - ci] = advance(loc_b[grp - 1 - ci], st_b)
        o_f = [add_intra(l, o) for l, o in zip(loc_f, o_f)]
        o_b = [add_intra(l, o) for l, o in zip(loc_b, o_b)]
        of_scr[rf, :] = jnp.concatenate(o_f, axis=0)
        ob_scr[rb, :] = jnp.concatenate(o_b, axis=0)
        return st_f, st_b

    st_f, st_b = lax.fori_loop(0, n, body, st0)
    if emit_state:
        st_ref[0] = st_f.T
        st_ref[1] = st_b.T

    fin_rows = math.gcd(seq_len, 256)
    onorm = on_ref[...]

    def finish(ri, _):
        r = pl.ds(pl.multiple_of(ri * fin_rows, fin_rows), fin_rows)
        o = of_scr[r, :] + ob_scr[r, :]
        y = o * lax.rsqrt(jnp.mean(o * o, axis=-1, keepdims=True) + EPS) * onorm
        o_ref[r, :] = (y * _silu(g_ref[r, :])).astype(o_ref.dtype)
        return 0

    lax.fori_loop(0, seq_len // fin_rows, finish, 0)


def _hgrn_scan(proj, lb_logits, onorm, s0, *, layer_j, n_seq, seq_len, row_off, heads, emit_state):
    dk = LANES
    d = heads * dk
    blk_off = row_off // seq_len
    n_lb = lb_logits.shape[1]

    def sec(k):
        return pl.BlockSpec((seq_len, dk), lambda b, h: (b + blk_off, k * heads + h))

    in_specs = [sec(0), sec(1), sec(2), sec(3), sec(4),
                pl.BlockSpec((2 * n_lb, dk), lambda b, h: (0, h)),
                pl.BlockSpec((1, dk), lambda b, h: (0, 0))]
    args = [proj] * 5 + [lb_logits.reshape(2 * n_lb, d), onorm.reshape(1, dk)]
    if s0 is not None:
        in_specs.append(pl.BlockSpec((None, None, 2, None, dk, dk),
                                     lambda b, h: (b, layer_j, 0, h, 0, 0)))
        args.append(s0)
    out_shape = [jax.ShapeDtypeStruct((n_seq * seq_len, d), BF16)]
    out_specs = [pl.BlockSpec((seq_len, dk), lambda b, h: (b, h))]
    if emit_state:
        out_shape.append(jax.ShapeDtypeStruct((n_seq, 2, heads, dk, dk), F32))
        out_specs.append(pl.BlockSpec((None, 2, None, dk, dk), lambda b, h: (b, 0, h, 0, 0)))
    kern = functools.partial(_hgrn_kernel, seq_len=seq_len, layer_j=layer_j, n_lb=n_lb,
                             has_s0=s0 is not None, emit_state=emit_state)
    return pl.pallas_call(
        kern,
        grid=(n_seq, heads),
        in_specs=in_specs,
        out_specs=out_specs,
        out_shape=out_shape,
        scratch_shapes=[pltpu.VMEM((seq_len, dk), F32), pltpu.VMEM((seq_len, dk), F32)],
        compiler_params=_params("arbitrary", "arbitrary"),
        name="hgrn_scan",
    )(*args)


def _attn_kernel(*refs, has_cache, lam_init, q_scale):
    q_ref, k_ref, v_ref = refs[:3]
    pos = 3
    kc_ref = vc_ref = None
    if has_cache:
        kc_ref, vc_ref = refs[3], refs[4]
        pos = 5
    lam_ref, sub_ref, o_ref = refs[pos:pos + 3]

    lp = lam_ref[...]
    lam = (jnp.exp(jnp.sum(lp[0:1] * lp[1:2], axis=-1, keepdims=True))
           - jnp.exp(jnp.sum(lp[2:3] * lp[3:4], axis=-1, keepdims=True)) + lam_init)

    hd = LANES
    v_new = v_ref[...].astype(BF16)
    v_old = vc_ref[...].astype(BF16) if has_cache else None
    tq = q_ref.shape[0]
    rb = min(tq, ATTN_ROW_BLOCK)
    chains = [(r, j) for r in range(0, tq, rb) for j in range(2)]
    scores = []
    for r, j in chains:
        cols = slice(j * hd, (j + 1) * hd)
        qj = q_ref[r:r + rb, cols]
        if q_scale != 1.0:
            qj = qj.astype(F32) * q_scale
        qj = qj.astype(BF16)
        s_new = lax.dot_general(qj, k_ref[:, cols].astype(BF16), NT_DIMS,
                                preferred_element_type=F32)
        s_old = None
        if has_cache:
            s_old = lax.dot_general(qj, kc_ref[:, cols].astype(BF16), NT_DIMS,
                                    preferred_element_type=F32)
        scores.append((s_new, s_old))
    outs = {}
    for (r, j), (s_new, s_old) in zip(chains, scores):
        mx = jnp.max(s_new, axis=-1, keepdims=True)
        if has_cache:
            mx = jnp.maximum(mx, jnp.max(s_old, axis=-1, keepdims=True))
        p_new = jnp.exp2(s_new - mx)
        den = jnp.sum(p_new, axis=-1, keepdims=True)
        acc = jnp.dot(p_new.astype(BF16), v_new, preferred_element_type=F32)
        if has_cache:
            p_old = jnp.exp2(s_old - mx)
            den += jnp.sum(p_old, axis=-1, keepdims=True)
            acc += jnp.dot(p_old.astype(BF16), v_old, preferred_element_type=F32)
        outs[r, j] = acc / den
    sub = sub_ref[...]
    for r in range(0, tq, rb):
        o = outs[r, 0] - lam * outs[r, 1]
        y = o * lax.rsqrt(jnp.mean(o * o, axis=-1, keepdims=True) + EPS) * sub
        o_ref[r:r + rb, :] = (y * (1.0 - lam_init)).astype(o_ref.dtype)


def _diff_attention(qkv, cache_k, cache_v, lam_p, subln, *, n_seq, seq_len, heads, lam_init,
                    q_scale, tq):
    hd2 = 2 * LANES
    d = heads * hd2
    qb = seq_len // tq
    fused = not isinstance(qkv, (list, tuple))
    k_off, v_off = (heads, 2 * heads) if fused else (0, 0)
    in_specs = [
        pl.BlockSpec((tq, hd2), lambda b, h, i: (b * qb + i, h)),
        pl.BlockSpec((seq_len, hd2), lambda b, h, i: (b, k_off + h)),
        pl.BlockSpec((seq_len, hd2), lambda b, h, i: (b, v_off + h)),
    ]
    args = [qkv, qkv, qkv] if fused else list(qkv)
    if cache_k is not None:
        past = cache_k.shape[1]
        in_specs += [pl.BlockSpec((None, past, hd2), lambda b, h, i: (b, 0, h))] * 2
        args += [cache_k, cache_v]
    in_specs += [pl.BlockSpec(lam_p.shape, lambda b, h, i: (0, 0)),
                 pl.BlockSpec((1, hd2), lambda b, h, i: (0, 0))]
    args += [lam_p, subln.reshape(1, hd2)]
    kern = functools.partial(_attn_kernel, has_cache=cache_k is not None, lam_init=lam_init,
                             q_scale=q_scale)
    return pl.pallas_call(
        kern,
        grid=(n_seq, heads, qb),
        in_specs=in_specs,
        out_specs=pl.BlockSpec((tq, hd2), lambda b, h, i: (b * qb + i, h)),
        out_shape=jax.ShapeDtypeStruct((n_seq * seq_len, d), BF16),
        compiler_params=_params("arbitrary", "arbitrary", "arbitrary"),
        name="diff_attention",
    )(*args)


def _out_proj_kernel(*refs, a_bounds, x_bounds):
    a_refs = refs[:len(a_bounds)]
    x_refs = refs[len(a_bounds):len(a_bounds) + len(x_bounds)]
    w_ref, gate_ref, o_ref = refs[len(a_bounds) + len(x_bounds):]
    i = pl.program_id(0)
    for a_ref, (a_lo, a_cnt) in zip(a_refs, a_bounds):
        for x_ref, (x_lo, x_cnt) in zip(x_refs, x_bounds):
            lo, hi = max(a_lo, x_lo), min(a_lo + a_cnt, x_lo + x_cnt)
            if lo >= hi:
                continue

            @pl.when((i >= lo) & (i < hi))
            def _(a_ref=a_ref, x_ref=x_ref):
                y = jnp.dot(a_ref[...], w_ref[...], preferred_element_type=F32)
                o_ref[...] = x_ref[...] + gate_ref[...] * y


def _out_project(a_segs, w, x_segs, mod, gate_chunk, *, p_rows, s_len, tm=512):
    d = x_segs[0].shape[1]
    t = sum(x.shape[0] for x in x_segs)
    kdim = w.shape[0]
    a_specs, a_bounds = _row_segment_specs(a_segs, tm)
    x_specs, x_bounds = _row_segment_specs(x_segs, tm)
    in_specs = a_specs + x_specs + [
        pl.BlockSpec((kdim, d), lambda i, j: (0, 0)),
        _mod_spec(gate_chunk, d, tm, p_rows, s_len),
    ]
    return pl.pallas_call(
        functools.partial(_out_proj_kernel, a_bounds=a_bounds, x_bounds=x_bounds),
        grid=(t // tm, 1),
        in_specs=in_specs,
        out_specs=pl.BlockSpec((tm, d), lambda i, j: (i, 0)),
        out_shape=jax.ShapeDtypeStruct((t, d), F32),
        compiler_params=_params("arbitrary", "arbitrary"),
        name="out_project_residual",
    )(*a_segs, *x_segs, w, mod)


def _ffn_kernel(x_ref, xp_ref, xn_ref, g_ref, sh_ref, sc_ref, gate_ref, wg_ref, wv_ref,
                conv_a, wd_a, conv_b, wd_b, *rest, tm, nj, p_rows, p_len, s_len, final):
    if final:
        gf_ref, yp_ref, ys_ref, h_ref, act_ref, ug_scr, uv_scr, acc_ref = rest
    else:
        o_ref, h_ref, act_ref, ug_scr, uv_scr, acc_ref = rest
    i = pl.program_id(0)
    j = pl.program_id(1)
    hs = xp_ref.shape[0]
    sub = FFN_TILE
    ext = tm + 2 * hs
    row0 = i * tm
    in_prompt = row0 < p_rows
    inner_edges = jnp.where(in_prompt, 1.0, 0.0)
    win = lax.broadcasted_iota(jnp.int32, (2 * SUBLANES, 1), 0)

    def up(cols):
        h = h_ref[...]
        return (jnp.dot(h, wg_ref[:, cols], preferred_element_type=F32),
                jnp.dot(h, wv_ref[:, cols], preferred_element_type=F32))

    def conv(u_all, u_scr, cw, cb):
        u_scr[...] = u_all
        u = u_all[hs:hs + tm]
        prev = u_scr[hs - 1:hs - 1 + tm, :]
        nxt = u_scr[hs + 1:hs + 1 + tm, :]
        y = prev * cw[0:1] + u * cw[1:2] + nxt * cw[2:3] + cb
        pieces, done = [], 0
        for edge in range(p_len, tm, p_len):
            lo, hi = edge - SUBLANES, edge + SUBLANES
            leak = (jnp.where(win == SUBLANES - 1, nxt[lo:hi] * cw[2:3], 0.0)
                    + jnp.where(win == SUBLANES, prev[lo:hi] * cw[0:1], 0.0))
            pieces += [y[done:lo], y[lo:hi] - inner_edges * leak]
            done = hi
        return jnp.concatenate(pieces + [y[done:]], axis=0) if pieces else y

    def activate(ug, uv, conv_ref):
        p = conv_ref[...]
        gate = conv(ug, ug_scr, p[0:3], p[3:4])
        val = conv(uv, uv_scr, p[4:7], p[7:8])
        return (_silu(gate) * val).astype(BF16)

    def activate_a(ug, uv):
        return activate(ug, uv, conv_a)

    def project_down(act, wd):
        acc_ref[...] += jnp.dot(act, wd[...], preferred_element_type=F32)

    cols_a, cols_b = slice(0, sub), slice(sub, 2 * sub)

    @pl.when(j == 0)
    def _():
        g, sh, sc = g_ref[...], sh_ref[...], sc_ref[...]
        h_ref[hs:hs + tm, :] = _norm_mod(x_ref[...], g, sh, sc).astype(BF16)
        end = row0 + tm
        at_seq_end = jnp.where(in_prompt, lax.rem(end, p_len) == 0,
                               lax.rem(end - p_rows, s_len) == 0)
        at_seq_start = jnp.where(in_prompt, lax.rem(row0, p_len) == 0,
                                 lax.rem(row0 - p_rows, s_len) == 0)
        after = _norm_mod(xn_ref[...], g, sh, sc) * jnp.where(at_seq_end, 0.0, 1.0)
        before = _norm_mod(xp_ref[...], g, sh, sc) * jnp.where(at_seq_start, 0.0, 1.0)
        h_ref[0:hs, :] = before.astype(BF16)
        h_ref[hs + tm:ext, :] = after.astype(BF16)
        acc_ref[...] = jnp.zeros_like(acc_ref)
        ua = up(cols_a)
        act_a = activate_a(*ua)
        ub = up(cols_b)
        project_down(act_a, wd_a)
        act_ref[...] = activate(*ub, conv_b)

    @pl.when((j > 0) & (j < nj))
    def _():
        ua = up(cols_a)
        project_down(act_ref[...], wd_b)
        act_a = activate_a(*ua)
        ub = up(cols_b)
        project_down(act_a, wd_a)
        act_ref[...] = activate(*ub, conv_b)

    @pl.when(j == nj)
    def _():
        project_down(act_ref[...], wd_b)
        res = x_ref[...] + gate_ref[...] * acc_ref[...]
        if not final:
            o_ref[...] = res
        else:
            y = res * lax.rsqrt(jnp.mean(res * res, axis=-1, keepdims=True) + EPS) * gf_ref[...]

            @pl.when(in_prompt)
            def _():
                yp_ref[...] = y

            @pl.when(jnp.logical_not(in_prompt))
            def _():
                ys_ref[...] = y


def _ffn(x, g, mod, shift_chunk, scale_chunk, gate_chunk, w_up, cw, cb, w_down, *,
         p_rows, p_len, s_len, tm=512, final_gain=None):
    t, d = x.shape
    ff = w_up.shape[1] // 2
    sub = FFN_TILE
    assert ff % (2 * sub) == 0
    assert (tm % p_len == 0 or p_len % tm == 0) and p_rows % tm == 0 and s_len % tm == 0
    nj = ff // (2 * sub)
    nt = ff // sub
    hs = 16
    per = tm // hs
    last_blk = t // hs - 1

    def tile_a(i, j):
        return jnp.minimum(2 * j, nt - 2)

    def tile_b(i, j):
        return jnp.minimum(2 * j + 1, nt - 1)

    def parked_b(i, j):
        return jnp.maximum(2 * j - 1, 1)

    def tile_specs(conv_tile, down_tile):
        return [
            pl.BlockSpec((None, 2 * (taps + 1), sub), lambda i, j: (conv_tile(i, j), 0, 0)),
            pl.BlockSpec((sub, d), lambda i, j: (down_tile(i, j), 0)),
        ]

    taps = cw.shape[0]
    conv_p = jnp.concatenate([cw[:, :ff], cb[None, :ff], cw[:, ff:], cb[None, ff:]], axis=0)
    conv_p = conv_p.reshape(2 * (taps + 1), nt, sub).transpose(1, 0, 2)

    in_specs = [
        pl.BlockSpec((tm, d), lambda i, j: (i, 0)),
        pl.BlockSpec((hs, d), lambda i, j: (jnp.maximum(i * per - 1, 0), 0)),
        pl.BlockSpec((hs, d), lambda i, j: (jnp.minimum((i + 1) * per, last_blk), 0)),
        pl.BlockSpec((1, d), lambda i, j: (0, 0)),
        _mod_spec(shift_chunk, d, tm, p_rows, s_len),
        _mod_spec(scale_chunk, d, tm, p_rows, s_len),
        _mod_spec(gate_chunk, d, tm, p_rows, s_len),
        pl.BlockSpec((None, d, 2 * sub), lambda i, j: (jnp.minimum(j, nj - 1), 0, 0)),
        pl.BlockSpec((None, d, 2 * sub), lambda i, j: (jnp.minimum(j, nj - 1) + nj, 0, 0)),
    ] + tile_specs(tile_a, tile_a) + tile_specs(tile_b, parked_b)
    w_up = _column_tiles(w_up, 2 * sub)
    tile_args = [conv_p, w_down.astype(BF16)]
    final = final_gain is not None
    kern = functools.partial(_ffn_kernel, tm=tm, nj=nj, p_rows=p_rows, p_len=p_len,
                             s_len=s_len, final=final)
    args = [x, x, x, g.reshape(1, d), mod, mod, mod, w_up, w_up, *tile_args, *tile_args]
    if final:
        pt = p_rows // tm
        in_specs.append(pl.BlockSpec((1, d), lambda i, j: (0, 0)))
        args.append(final_gain.reshape(1, d))
        out_specs = [pl.BlockSpec((tm, d), lambda i, j: (jnp.minimum(i, pt - 1), 0)),
                     pl.BlockSpec((tm, d), lambda i, j: (jnp.maximum(i - pt, 0), 0))]
        out_shape = [jax.ShapeDtypeStruct((p_rows, d), F32),
                     jax.ShapeDtypeStruct((t - p_rows, d), F32)]
    else:
        out_specs = pl.BlockSpec((tm, d), lambda i, j: (i, 0))
        out_shape = jax.ShapeDtypeStruct((t, d), F32)
    return pl.pallas_call(
        kern,
        grid=(t // tm, nj + 1),
        in_specs=in_specs,
        out_specs=out_specs,
        out_shape=out_shape,
        scratch_shapes=[pltpu.VMEM((tm + 2 * hs, d), BF16),
                        pltpu.VMEM((tm, sub), BF16),
                        pltpu.VMEM((tm + 2 * hs, sub), F32),
                        pltpu.VMEM((tm + 2 * hs, sub), F32),
                        pltpu.VMEM((tm, d), F32)],
        compiler_params=_params("arbitrary", "arbitrary"),
        name="conv_ffn",
    )(*args)


def _rope_tables(seq_len, hd):
    pos = jnp.arange(seq_len)
    row = (pos // GRID_W).astype(F32)
    col = (pos % GRID_W).astype(F32)
    nf = hd // 4
    inv = ROPE_BASE ** (-jnp.arange(nf, dtype=F32) / nf)
    ar = row[:, None] * inv
    ac = col[:, None] * inv
    cos = jnp.concatenate([jnp.cos(ar), jnp.cos(ar), jnp.cos(ac), jnp.cos(ac)], axis=1)
    sin = jnp.concatenate([-jnp.sin(ar), jnp.sin(ar), -jnp.sin(ac), jnp.sin(ac)], axis=1)
    return cos, sin


def kernel(x_prompt, x_sample, c, cache_hgrn_state, cache_attn_k, cache_attn_v, c_ctx, w_mod, b_mod, norm_mix, norm_ffn, w_hgrn_in, hgrn_lb_logits, hgrn_onorm, w_hgrn_out, w_attn_in, attn_lambda, attn_subln, w_attn_out, w_ffn_up, ffn_conv_w, ffn_conv_b, w_ffn_down, norm_final):
    batch, p_len, d = x_prompt.shape
    dec_batch, s_len, _ = x_sample.shape
    depth = w_mod.shape[0]
    p_rows = batch * p_len
    s_rows = dec_batch * s_len
    hgrn_heads = d // LANES
    diff_heads = d // (2 * LANES)
    hd = LANES
    assert 1 + dec_batch <= MOD_ROWS and p_rows % s_len == 0

    x = [x_prompt.reshape(p_rows, d), x_sample.reshape(s_rows, d)]
    cvec = jnp.concatenate(
        [c_ctx[None, :], c, jnp.zeros((MOD_ROWS - 1 - dec_batch, d), F32)], axis=0)
    mods = _modulation(cvec, w_mod, b_mod)
    seg = dict(p_rows=p_rows, s_len=s_len)

    hgrn_states, attn_ks, attn_vs = [], [], []
    for l in range(depth):
        mod = mods[l][:, None, :]
        j = l // N_MIXERS
        if l % N_MIXERS == 0:
            proj = _project(x, norm_mix[l], mod, 0, 1, w_hgrn_in[j],
                            row_off=0, rows=p_rows + s_rows, out_dtype=F32, tm=1024, tn=1024,
                            **seg)
            mix_p, st = _hgrn_scan(proj, hgrn_lb_logits, hgrn_onorm[j], None, layer_j=j,
                                   n_seq=batch, seq_len=p_len, row_off=0, heads=hgrn_heads,
                                   emit_state=True)
            hgrn_states.append(st)
            mix_s, = _hgrn_scan(proj, hgrn_lb_logits, hgrn_onorm[j], cache_hgrn_state,
                                layer_j=j, n_seq=dec_batch, seq_len=s_len, row_off=p_rows,
                                heads=hgrn_heads, emit_state=False)
            w_out = w_hgrn_out[j]
        else:
            lam_init = 0.8 - 0.6 * math.exp(-0.3 * l)
            q_scale = hd ** -0.5 * LOG2_E
            w_in = w_attn_in[j]
            qkv_p = _project(x, norm_mix[l], mod, 0, 1, w_in, row_off=0, rows=p_rows,
                             out_dtype=F32, tn=1024, split=3, **seg)
            attn_ks.append(qkv_p[1].reshape(batch, p_len, 2 * diff_heads, hd))
            attn_vs.append(qkv_p[2].reshape(batch, p_len, diff_heads, 2 * hd))
            mix_p = _diff_attention(qkv_p, None, None, attn_lambda[j], attn_subln[j],
                                    n_seq=batch, seq_len=p_len, heads=diff_heads,
                                    lam_init=lam_init, q_scale=q_scale, tq=p_len)
            qkv_s = _project(x, norm_mix[l], mod, 0, 1, w_in, row_off=p_rows, rows=s_rows,
                             out_dtype=BF16, rope_tables=_rope_tables(s_len, hd),
                             q_scale=q_scale, tn=1024, **seg)
            past = cache_attn_k.shape[2]
            mix_s = _diff_attention(qkv_s, cache_attn_k[:, j].reshape(dec_batch, past, d),
                                    cache_attn_v[:, j].reshape(dec_batch, past, d),
                                    attn_lambda[j], attn_subln[j], n_seq=dec_batch,
                                    seq_len=s_len, heads=diff_heads, lam_init=lam_init,
                                    q_scale=1.0, tq=512)
            w_out = w_attn_out[j]
        x = _out_project([mix_p, mix_s], w_out.astype(BF16), x, mod, 2, **seg)
        last = l == depth - 1
        x = _ffn(x, norm_ffn[l], mod, 3, 4, 5, w_ffn_up[l], ffn_conv_w[l], ffn_conv_b[l],
                 w_ffn_down[l], p_len=p_len, final_gain=norm_final if last else None, **seg)
        x = list(x) if last else [x]

    y_prompt = x[0].reshape(batch, p_len, d)
    y_sample = x[1].reshape(dec_batch, s_len, d)
    new_hgrn_state = jnp.stack(hgrn_states, axis=1)
    new_attn_k = jnp.stack(attn_ks, axis=1)
    new_attn_v = jnp.stack(attn_vs, axis=1)
    return (y_prompt, y_sample, new_hgrn_state, new_attn_k, new_attn_v)
```

```python
import functools
import math

import jax
import jax.numpy as jnp
from jax import lax
from jax.experimental import pallas as pl
from jax.experimental.pallas import tpu as pltpu

F32 = jnp.float32
BF16 = jnp.bfloat16

EPS = 1e-6
GRID_W = 64
ROPE_BASE = 10000.0
N_MIXERS = 2

LANES = 128
SUBLANES = 8
MXU_COLS = 256
MOD_ROWS = 16
HGRN_CHUNK = 64
HGRN_SUB = 16
HGRN_GROUP = 16
ATTN_ROW_BLOCK = 128
ATTN_LOOKAHEAD = 1
FFN_TILE = 256
LOG2_E = 1.4426950408889634
VMEM_LIMIT = 56 * 1024 * 1024

NT_DIMS = (((1,), (1,)), ((), ()))
TN_DIMS = (((0,), (0,)), ((), ()))


def _params(*sem):
    return pltpu.CompilerParams(dimension_semantics=sem, vmem_limit_bytes=VMEM_LIMIT)


def _tile(n, want):
    best = LANES
    for cand in range(LANES, min(n, want) + 1, LANES):
        if n % cand == 0:
            best = cand
    assert n % best == 0
    return best


def _column_tiles(w, tn):
    k, n = w.shape
    return w.astype(BF16).reshape(k, n // tn, tn).transpose(1, 0, 2)


def _silu(x):
    return x * jax.nn.sigmoid(x)


def _split_bf16(x):
    hi = x.astype(BF16)
    lo = (x - hi.astype(F32)).astype(BF16)
    return hi, lo


def _mod_kernel(c_ref, w_ref, b_ref, o_ref):
    a_hi, a_lo = _split_bf16(_silu(c_ref[...]))
    w_hi, w_lo = _split_bf16(w_ref[...])
    acc = jnp.dot(a_hi, w_hi, preferred_element_type=F32)
    acc += jnp.dot(a_lo, w_hi, preferred_element_type=F32)
    acc += jnp.dot(a_hi, w_lo, preferred_element_type=F32)
    o_ref[...] = acc + b_ref[...]


def _modulation(cvec, w_mod, b_mod, tn=1024):
    depth, d, n = w_mod.shape
    tn = _tile(n, tn)
    return pl.pallas_call(
        _mod_kernel,
        grid=(depth, n // tn),
        in_specs=[
            pl.BlockSpec((MOD_ROWS, d), lambda l, j: (0, 0)),
            pl.BlockSpec((None, d, tn), lambda l, j: (l, 0, j)),
            pl.BlockSpec((None, 1, tn), lambda l, j: (l, 0, j)),
        ],
        out_specs=pl.BlockSpec((None, MOD_ROWS, tn), lambda l, j: (l, 0, j)),
        out_shape=jax.ShapeDtypeStruct((depth, MOD_ROWS, n), F32),
        compiler_params=_params("arbitrary", "arbitrary"),
        name="modulation",
    )(cvec, w_mod, b_mod.reshape(depth, 1, n))


def _norm_mod(x, g, shift, scale):
    y = x * lax.rsqrt(jnp.mean(x * x, axis=-1, keepdims=True) + EPS) * g
    return y * (1.0 + scale) + shift


def _seq_of_row(row0, p_rows, s_len):
    return jnp.where(row0 < p_rows, 0, 1 + (row0 - p_rows) // s_len)


def _mod_spec(chunk, d, tm, p_rows, s_len, tile_off=0):
    return pl.BlockSpec(
        (None, 1, d),
        lambda i, j: (_seq_of_row((i + tile_off) * tm, p_rows, s_len), 0, chunk))


def _proj_kernel(*refs, x_bounds, tile_off, rope, q_tiles, qk_tiles, q_scale, split):
    x_refs = refs[:len(x_bounds)]
    g_ref, sh_ref, sc_ref, w_ref, *rest = refs[len(x_bounds):]
    if rope:
        cos_ref, sin_ref, o_ref, h_ref = rest
    elif split:
        *o_refs, h_ref = rest
    else:
        o_ref, h_ref = rest
    tile = pl.program_id(0) + tile_off
    j = pl.program_id(1)

    for x_ref, (lo, cnt) in zip(x_refs, x_bounds):
        @pl.when((j == 0) & (tile >= lo) & (tile < lo + cnt))
        def _(x_ref=x_ref):
            h_ref[...] = _norm_mod(x_ref[...], g_ref[...], sh_ref[...],
                                   sc_ref[...]).astype(BF16)

    if split:
        per = pl.num_programs(1) // split
        for s, out in enumerate(o_refs):
            @pl.when((j >= s * per) & (j < (s + 1) * per))
            def _(out=out):
                out[...] = jnp.dot(h_ref[...], w_ref[...],
                                   preferred_element_type=F32).astype(out.dtype)
        return

    if not rope:
        o_ref[...] = jnp.dot(h_ref[...], w_ref[...],
                             preferred_element_type=F32).astype(o_ref.dtype)
        return

    @pl.when(j < qk_tiles)
    def _():
        tn = o_ref.shape[1]
        sub = min(tn, MXU_COLS)
        h = h_ref[...]
        accs = [jnp.dot(h, w_ref[:, c:c + sub], preferred_element_type=F32)
                for c in range(0, tn, sub)]
        scale = jnp.where(j < q_tiles, q_scale, 1.0)
        cos = cos_ref[...] * scale
        sin = sin_ref[...] * scale
        lane = lax.broadcasted_iota(jnp.int32, cos.shape, 1)
        first_of_pair = (lane // (LANES // 4)) % 2 == 0
        for ci, acc in enumerate(accs):
            for s in range(sub // LANES):
                xs = acc[:, s * LANES:(s + 1) * LANES]
                partner = jnp.where(first_of_pair,
                                    pltpu.roll(xs, LANES - LANES // 4, 1),
                                    pltpu.roll(xs, LANES // 4, 1))
                col = ci * sub + s * LANES
                o_ref[:, col:col + LANES] = (xs * cos + partner * sin).astype(o_ref.dtype)

    @pl.when(j >= qk_tiles)
    def _():
        o_ref[...] = jnp.dot(h_ref[...], w_ref[...],
                             preferred_element_type=F32).astype(o_ref.dtype)


def _row_segment_specs(segs, tm, tile_off=0, buffers=None):
    specs, bounds, lo = [], [], 0
    for k, a in enumerate(segs):
        mode = {} if buffers is None or buffers[k] is None else dict(
            pipeline_mode=pl.Buffered(buffers[k]))
        cnt = a.shape[0] // tm
        specs.append(pl.BlockSpec(
            (tm, a.shape[1]),
            lambda i, j, lo=lo, cnt=cnt: (jnp.clip(i + tile_off - lo, 0, cnt - 1), 0), **mode))
        bounds.append((lo, cnt))
        lo += cnt
    return specs, tuple(bounds)


def _project(x_segs, g, mod, shift_chunk, scale_chunk, w, *, row_off, rows, out_dtype,
             p_rows, s_len, tm=512, tn=512, rope_tables=None, q_scale=1.0, split=0):
    d = x_segs[0].shape[1]
    n = w.shape[1]
    tm = math.gcd(tm, p_rows, s_len)
    tile_off = row_off // tm
    rope = rope_tables is not None
    tn = _tile(n // 3 if rope else n // max(split, 1), tn)
    biggest = max(range(len(x_segs)), key=lambda k: x_segs[k].shape[0])
    x_specs, x_bounds = _row_segment_specs(
        x_segs, tm, tile_off,
        buffers=[None if k == biggest else 1 for k in range(len(x_segs))])
    in_specs = x_specs + [
        pl.BlockSpec((1, d), lambda i, j: (0, 0)),
        _mod_spec(shift_chunk, d, tm, p_rows, s_len, tile_off),
        _mod_spec(scale_chunk, d, tm, p_rows, s_len, tile_off),
        pl.BlockSpec((d, tn), lambda i, j: (0, j)),
    ]
    args = list(x_segs) + [g.reshape(1, d), mod, mod, w.astype(BF16)]
    q_tiles = qk_tiles = 0
    if rope:
        cos, sin = rope_tables
        tiles_per_seq = s_len // tm
        in_specs += [pl.BlockSpec((tm, LANES), lambda i, j: (i % tiles_per_seq, 0))] * 2
        args += [cos, sin]
        q_tiles = (n // 3) // tn
        qk_tiles = 2 * q_tiles
    kern = functools.partial(_proj_kernel, x_bounds=x_bounds, tile_off=tile_off, rope=rope,
                             q_tiles=q_tiles, qk_tiles=qk_tiles, q_scale=q_scale, split=split)
    if split:
        per = n // tn // split
        out_specs = [pl.BlockSpec((tm, tn),
                                  lambda i, j, s=s: (i, jnp.clip(j - s * per, 0, per - 1)))
                     for s in range(split)]
        out_shape = [jax.ShapeDtypeStruct((rows, n // split), out_dtype)] * split
    else:
        out_specs = pl.BlockSpec((tm, tn), lambda i, j: (i, j))
        out_shape = jax.ShapeDtypeStruct((rows, n), out_dtype)
    return pl.pallas_call(
        kern,
        grid=(rows // tm, n // tn),
        in_specs=in_specs,
        out_specs=out_specs,
        out_shape=out_shape,
        scratch_shapes=[pltpu.VMEM((tm, d), BF16)],
        compiler_params=_params("arbitrary", "arbitrary"),
        name="norm_mod_project",
    )(*args)


def _hgrn_group_local(groups):
    c, m = HGRN_CHUNK, HGRN_SUB
    nb = c // m
    t = lax.broadcasted_iota(jnp.int32, (c, c), 0)
    s = lax.broadcasted_iota(jnp.int32, (c, c), 1)

    jobs = []
    for q, z, v, lb, direction in groups:
        dk = q.shape[1]
        tri = (s <= t) if direction == 0 else (s >= t)
        cum = jnp.where(tri, 1.0, 0.0).astype(BF16)
        f = lb + (1.0 - lb) * jax.nn.sigmoid(z)
        g2 = jnp.log(f) * LOG2_E
        k = 1.0 - f
        g_hi = g2.astype(BF16)
        g_lo = (g2 - g_hi.astype(F32)).astype(BF16)
        v_bf = v.astype(BF16)
        for ci in range(q.shape[0] // c):
            rows = slice(ci * c, (ci + 1) * c)
            jobs.append(dict(direction=direction, tri=tri, cum=cum, q=q[rows], k=k[rows],
                             v=v_bf[rows], g=jnp.concatenate([g_hi[rows], g_lo[rows]], axis=1)))

    def padded(x, lo, hi):
        parts = [jnp.zeros((lo, dk), BF16)] if lo else []
        parts.append(x)
        if hi < c:
            parts.append(jnp.zeros((c - hi, dk), BF16))
        return jnp.concatenate(parts, axis=0) if len(parts) > 1 else x

    for job in jobs:
        job["sums"] = jnp.dot(job["cum"], job["g"], preferred_element_type=F32)

    for job in jobs:
        direction = job["direction"]
        sums = job["sums"]
        b = sums[:, :dk] + sums[:, dk:]
        anchors = []
        for i in range(nb):
            a_row = i * m + (m // 2 - 1 if direction == 0 else m // 2)
            anchors.append(b[a_row:a_row + 1, :])
        b_anchor = jnp.concatenate([jnp.broadcast_to(r, (m, dk)) for r in anchors], axis=0)
        q_rel = (job["q"] * jnp.exp2(b - b_anchor)).astype(BF16)
        full = nb - 1 if direction == 0 else 0
        q_cat, k_cat = [], []
        k_full = None
        for i in range(nb):
            lo, hi = (0, m * (i + 1)) if direction == 0 else (m * i, c)
            ki = job["k"][lo:hi] * jnp.exp2(anchors[i] - b[lo:hi])
            if i == full:
                k_full = ki
            k_cat.append(padded(ki.astype(BF16), lo, hi))
            q_cat.append(padded(q_rel[m * i:m * (i + 1)], m * i, m * (i + 1)))
        job["a"] = lax.dot_general(jnp.concatenate(q_cat, axis=1),
                                   jnp.concatenate(k_cat, axis=1), NT_DIMS,
                                   preferred_element_type=F32)
        edge = c - 1 if direction == 0 else 0
        total = b[edge:edge + 1, :]
        job["k_dec"] = (k_full * jnp.exp2(total - anchors[full])).astype(BF16)
        job["q_in"] = (job["q"] * jnp.exp2(b)).astype(BF16)
        job["decay"] = jnp.exp2(total)

    for job in jobs:
        job["u_t"] = lax.dot_general(job["v"], job["k_dec"], TN_DIMS,
                                     preferred_element_type=F32)

    out, pos = [], 0
    for q, *_ in groups:
        n = q.shape[0] // c
        out.append([(jnp.where(j["tri"], j["a"], 0.0).astype(BF16), j["v"], j["q_in"], j["u_t"],
                     j["decay"]) for j in jobs[pos:pos + n]])
        pos += n
    return out


def _hgrn_kernel(*refs, seq_len, layer_j, n_lb, has_s0, emit_state):
    q_ref, zf_ref, zb_ref, i_ref, g_ref, lg_ref, on_ref = refs[:7]
    pos = 7
    s0_ref = st_ref = None
    if has_s0:
        s0_ref = refs[pos]
        pos += 1
    o_ref = refs[pos]
    pos += 1
    if emit_state:
        st_ref = refs[pos]
        pos += 1
    of_scr, ob_scr = refs[pos], refs[pos + 1]

    grp = math.gcd(HGRN_GROUP, seq_len // HGRN_CHUNK)
    rows = grp * HGRN_CHUNK
    n = seq_len // rows
    dv = q_ref.shape[1]

    lbs = []
    for d in range(2):
        lg = lg_ref[d * n_lb:(d + 1) * n_lb, :]
        e = jnp.exp(lg - jnp.max(lg, axis=0, keepdims=True))
        lbs.append(jnp.sum(e[:layer_j + 1], axis=0, keepdims=True) / jnp.sum(e, axis=0, keepdims=True))

    if has_s0:
        st0 = (s0_ref[0].T, s0_ref[1].T)
    else:
        st0 = (jnp.zeros((dv, dv), F32), jnp.zeros((dv, dv), F32))

    def advance(local, st):
        _, _, q_in, u_t, decay = local
        o = lax.dot_general(q_in, st.astype(BF16), NT_DIMS, preferred_element_type=F32)
        return st * decay + u_t, o

    def add_intra(local, o_inter):
        a, v = local[:2]
        return o_inter + jnp.dot(a, v, preferred_element_type=F32)

    def body(gi, carry):
        st_f, st_b = carry
        rf = pl.ds(pl.multiple_of(gi * rows, rows), rows)
        rb = pl.ds(pl.multiple_of((n - 1 - gi) * rows, rows), rows)
        loc_f, loc_b = _hgrn_group_local([
            (q_ref[rf, :], zf_ref[rf, :], i_ref[rf, :], lbs[0], 0),
            (q_ref[rb, :], zb_ref[rb, :], i_ref[rb, :], lbs[1], 1)])
        o_f, o_b = [None] * grp, [None] * grp
        for ci in range(grp):
            st_f, o_f[ci] = advance(loc_f[ci], st_f)
            st_b, o_b[grp - 1 - ci] = advance(loc_b[grp - 1 - ci], st_b)
        o_f = [add_intra(l, o) for l, o in zip(loc_f, o_f)]
        o_b = [add_intra(l, o) for l, o in zip(loc_b, o_b)]
        of_scr[rf, :] = jnp.concatenate(o_f, axis=0)
        ob_scr[rb, :] = jnp.concatenate(o_b, axis=0)
        return st_f, st_b

    st_f, st_b = lax.fori_loop(0, n, body, st0)
    if emit_state:
        st_ref[0] = st_f.T
        st_ref[1] = st_b.T

    fin_rows = math.gcd(seq_len, 256)
    onorm = on_ref[...]

    def finish(ri, _):
        r = pl.ds(pl.multiple_of(ri * fin_rows, fin_rows), fin_rows)
        o = of_scr[r, :] + ob_scr[r, :]
        y = o * lax.rsqrt(jnp.mean(o * o, axis=-1, keepdims=True) + EPS) * onorm
        o_ref[r, :] = (y * _silu(g_ref[r, :])).astype(o_ref.dtype)
        return 0

    lax.fori_loop(0, seq_len // fin_rows, finish, 0)


def _hgrn_scan(proj, lb_logits, onorm, s0, *, layer_j, n_seq, seq_len, row_off, heads, emit_state):
    dk = LANES
    d = heads * dk
    blk_off = row_off // seq_len
    n_lb = lb_logits.shape[1]

    def sec(k):
        return pl.BlockSpec((seq_len, dk), lambda b, h: (b + blk_off, k * heads + h))

    in_specs = [sec(0), sec(1), sec(2), sec(3), sec(4),
                pl.BlockSpec((2 * n_lb, dk), lambda b, h: (0, h)),
                pl.BlockSpec((1, dk), lambda b, h: (0, 0))]
    args = [proj] * 5 + [lb_logits.reshape(2 * n_lb, d), onorm.reshape(1, dk)]
    if s0 is not None:
        in_specs.append(pl.BlockSpec((None, None, 2, None, dk, dk),
                                     lambda b, h: (b, layer_j, 0, h, 0, 0)))
        args.append(s0)
    out_shape = [jax.ShapeDtypeStruct((n_seq * seq_len, d), BF16)]
    out_specs = [pl.BlockSpec((seq_len, dk), lambda b, h: (b, h))]
    if emit_state:
        out_shape.append(jax.ShapeDtypeStruct((n_seq, 2, heads, dk, dk), F32))
        out_specs.append(pl.BlockSpec((None, 2, None, dk, dk), lambda b, h: (b, 0, h, 0, 0)))
    kern = functools.partial(_hgrn_kernel, seq_len=seq_len, layer_j=layer_j, n_lb=n_lb,
                             has_s0=s0 is not None, emit_state=emit_state)
    return pl.pallas_call(
        kern,
        grid=(n_seq, heads),
        in_specs=in_specs,
        out_specs=out_specs,
        out_shape=out_shape,
        scratch_shapes=[pltpu.VMEM((seq_len, dk), F32), pltpu.VMEM((seq_len, dk), F32)],
        compiler_params=_params("arbitrary", "arbitrary"),
        name="hgrn_scan",
    )(*args)


def _attn_kernel(*refs, has_cache, lam_init, q_scale):
    q_ref, k_ref, v_ref = refs[:3]
    pos = 3
    kc_ref = vc_ref = None
    if has_cache:
        kc_ref, vc_ref = refs[3], refs[4]
        pos = 5
    lam_ref, sub_ref, o_ref = refs[pos:pos + 3]

    lp = lam_ref[...]
    lam = (jnp.exp(jnp.sum(lp[0:1] * lp[1:2], axis=-1, keepdims=True))
           - jnp.exp(jnp.sum(lp[2:3] * lp[3:4], axis=-1, keepdims=True)) + lam_init)

    hd = LANES
    v_new = v_ref[...].astype(BF16)
    v_old = vc_ref[...].astype(BF16) if has_cache else None
    tq = q_ref.shape[0]
    rb = min(tq, ATTN_ROW_BLOCK)
    chains = [(r, j) for r in range(0, tq, rb) for j in range(2)]
    def score(r, j):
        cols = slice(j * hd, (j + 1) * hd)
        qj = q_ref[r:r + rb, cols]
        if q_scale != 1.0:
            qj = qj.astype(F32) * q_scale
        qj = qj.astype(BF16)
        s_new = lax.dot_general(qj, k_ref[:, cols].astype(BF16), NT_DIMS,
                                preferred_element_type=F32)
        s_old = None
        if has_cache:
            s_old = lax.dot_general(qj, kc_ref[:, cols].astype(BF16), NT_DIMS,
                                    preferred_element_type=F32)
        return s_new, s_old

    ahead = min(ATTN_LOOKAHEAD, len(chains))
    scores = [score(*c) for c in chains[:ahead]]
    outs = {}
    for ci, (r, j) in enumerate(chains):
        if ci + ahead < len(chains):
            scores.append(score(*chains[ci + ahead]))
        s_new, s_old = scores[ci]
        mx = jnp.max(s_new, axis=-1, keepdims=True)
        if has_cache:
            mx = jnp.maximum(mx, jnp.max(s_old, axis=-1, keepdims=True))
        p_new = jnp.exp2(s_new - mx)
        den = jnp.sum(p_new, axis=-1, keepdims=True)
        acc = jnp.dot(p_new.astype(BF16), v_new, preferred_element_type=F32)
        if has_cache:
            p_old = jnp.exp2(s_old - mx)
            den += jnp.sum(p_old, axis=-1, keepdims=True)
            acc += jnp.dot(p_old.astype(BF16), v_old, preferred_element_type=F32)
        outs[r, j] = acc / den
    sub = sub_ref[...]
    for r in range(0, tq, rb):
        o = outs[r, 0] - lam * outs[r, 1]
        y = o * lax.rsqrt(jnp.mean(o * o, axis=-1, keepdims=True) + EPS) * sub
        o_ref[r:r + rb, :] = (y * (1.0 - lam_init)).astype(o_ref.dtype)


def _diff_attention(qkv, cache_k, cache_v, lam_p, subln, *, n_seq, seq_len, heads, lam_init,
                    q_scale, tq, cache_layer=0):
    hd2 = 2 * LANES
    d = heads * hd2
    qb = seq_len // tq
    fused = not isinstance(qkv, (list, tuple))
    k_off, v_off = (heads, 2 * heads) if fused else (0, 0)
    in_specs = [
        pl.BlockSpec((tq, hd2), lambda b, h, i: (b * qb + i, h)),
        pl.BlockSpec((seq_len, hd2), lambda b, h, i: (b, k_off + h)),
        pl.BlockSpec((seq_len, hd2), lambda b, h, i: (b, v_off + h)),
    ]
    args = [qkv, qkv, qkv] if fused else list(qkv)
    if cache_k is not None:
        past = cache_k.shape[2]
        in_specs += [pl.BlockSpec((None, None, past, hd2),
                                  lambda b, h, i: (b, cache_layer, 0, h))] * 2
        args += [cache_k, cache_v]
    in_specs += [pl.BlockSpec(lam_p.shape, lambda b, h, i: (0, 0)),
                 pl.BlockSpec((1, hd2), lambda b, h, i: (0, 0))]
    args += [lam_p, subln.reshape(1, hd2)]
    kern = functools.partial(_attn_kernel, has_cache=cache_k is not None, lam_init=lam_init,
                             q_scale=q_scale)
    return pl.pallas_call(
        kern,
        grid=(n_seq, heads, qb),
        in_specs=in_specs,
        out_specs=pl.BlockSpec((tq, hd2), lambda b, h, i: (b * qb + i, h)),
        out_shape=jax.ShapeDtypeStruct((n_seq * seq_len, d), BF16),
        compiler_params=_params("arbitrary", "arbitrary", "arbitrary"),
        name="diff_attention",
    )(*args)


def _out_proj_kernel(*refs, a_bounds, x_bounds):
    a_refs = refs[:len(a_bounds)]
    x_refs = refs[len(a_bounds):len(a_bounds) + len(x_bounds)]
    w_ref, gate_ref, o_ref = refs[len(a_bounds) + len(x_bounds):]
    i = pl.program_id(0)
    for a_ref, (a_lo, a_cnt) in zip(a_refs, a_bounds):
        for x_ref, (x_lo, x_cnt) in zip(x_refs, x_bounds):
            lo, hi = max(a_lo, x_lo), min(a_lo + a_cnt, x_lo + x_cnt)
            if lo >= hi:
                continue

            @pl.when((i >= lo) & (i < hi))
            def _(a_ref=a_ref, x_ref=x_ref):
                y = jnp.dot(a_ref[...], w_ref[...], preferred_element_type=F32)
                o_ref[...] = x_ref[...] + gate_ref[...] * y


def _out_project(a_segs, w, x_segs, mod, gate_chunk, *, p_rows, s_len, tm=512):
    d = x_segs[0].shape[1]
    t = sum(x.shape[0] for x in x_segs)
    kdim = w.shape[0]
    a_specs, a_bounds = _row_segment_specs(a_segs, tm)
    x_specs, x_bounds = _row_segment_specs(x_segs, tm)
    in_specs = a_specs + x_specs + [
        pl.BlockSpec((kdim, d), lambda i, j: (0, 0)),
        _mod_spec(gate_chunk, d, tm, p_rows, s_len),
    ]
    return pl.pallas_call(
        functools.partial(_out_proj_kernel, a_bounds=a_bounds, x_bounds=x_bounds),
        grid=(t // tm, 1),
        in_specs=in_specs,
        out_specs=pl.BlockSpec((tm, d), lambda i, j: (i, 0)),
        out_shape=jax.ShapeDtypeStruct((t, d), F32),
        compiler_params=_params("arbitrary", "arbitrary"),
        name="out_project_residual",
    )(*a_segs, *x_segs, w, mod)


def _ffn_kernel(x_ref, xp_ref, xn_ref, g_ref, sh_ref, sc_ref, gate_ref, wg_ref, wv_ref,
                conv_a, wd_a, conv_b, wd_b, *rest, tm, nj, p_rows, p_len, s_len, final):
    if final:
        gf_ref, yp_ref, ys_ref, h_ref, act_ref, ug_scr, uv_scr, acc_ref = rest
    else:
        o_ref, h_ref, act_ref, ug_scr, uv_scr, acc_ref = rest
    i = pl.program_id(0)
    j = pl.program_id(1)
    hs = xp_ref.shape[0]
    sub = FFN_TILE
    ext = tm + 2 * hs
    row0 = i * tm
    in_prompt = row0 < p_rows
    inner_edges = jnp.where(in_prompt, 1.0, 0.0)
    win = lax.broadcasted_iota(jnp.int32, (2 * SUBLANES, 1), 0)

    def up(cols):
        h = h_ref[...]
        return (jnp.dot(h, wg_ref[:, cols], preferred_element_type=F32),
                jnp.dot(h, wv_ref[:, cols], preferred_element_type=F32))

    def conv(u_all, u_scr, cw, cb):
        u_scr[...] = u_all
        u = u_all[hs:hs + tm]
        prev = u_scr[hs - 1:hs - 1 + tm, :]
        nxt = u_scr[hs + 1:hs + 1 + tm, :]
        y = prev * cw[0:1] + u * cw[1:2] + nxt * cw[2:3] + cb
        pieces, done = [], 0
        for edge in range(p_len, tm, p_len):
            lo, hi = edge - SUBLANES, edge + SUBLANES
            leak = (jnp.where(win == SUBLANES - 1, nxt[lo:hi] * cw[2:3], 0.0)
                    + jnp.where(win == SUBLANES, prev[lo:hi] * cw[0:1], 0.0))
            pieces += [y[done:lo], y[lo:hi] - inner_edges * leak]
            done = hi
        return jnp.concatenate(pieces + [y[done:]], axis=0) if pieces else y

    def activate(ug, uv, conv_ref):
        p = conv_ref[...]
        gate = conv(ug, ug_scr, p[0:3], p[3:4])
        val = conv(uv, uv_scr, p[4:7], p[7:8])
        return (_silu(gate) * val).astype(BF16)

    def activate_a(ug, uv):
        return activate(ug, uv, conv_a)

    def project_down(act, wd):
        acc_ref[...] += jnp.dot(act, wd[...], preferred_element_type=F32)

    cols_a, cols_b = slice(0, sub), slice(sub, 2 * sub)

    @pl.when(j == 0)
    def _():
        g, sh, sc = g_ref[...], sh_ref[...], sc_ref[...]
        h_ref[hs:hs + tm, :] = _norm_mod(x_ref[...], g, sh, sc).astype(BF16)
        end = row0 + tm
        at_seq_end = jnp.where(in_prompt, lax.rem(end, p_len) == 0,
                               lax.rem(end - p_rows, s_len) == 0)
        at_seq_start = jnp.where(in_prompt, lax.rem(row0, p_len) == 0,
                                 lax.rem(row0 - p_rows, s_len) == 0)
        after = _norm_mod(xn_ref[...], g, sh, sc) * jnp.where(at_seq_end, 0.0, 1.0)
        before = _norm_mod(xp_ref[...], g, sh, sc) * jnp.where(at_seq_start, 0.0, 1.0)
        h_ref[0:hs, :] = before.astype(BF16)
        h_ref[hs + tm:ext, :] = after.astype(BF16)
        acc_ref[...] = jnp.zeros_like(acc_ref)
        ua = up(cols_a)
        act_a = activate_a(*ua)
        ub = up(cols_b)
        project_down(act_a, wd_a)
        act_ref[...] = activate(*ub, conv_b)

    @pl.when((j > 0) & (j < nj))
    def _():
        ua = up(cols_a)
        project_down(act_ref[...], wd_b)
        act_a = activate_a(*ua)
        ub = up(cols_b)
        project_down(act_a, wd_a)
        act_ref[...] = activate(*ub, conv_b)

    @pl.when(j == nj)
    def _():
        project_down(act_ref[...], wd_b)
        res = x_ref[...] + gate_ref[...] * acc_ref[...]
        if not final:
            o_ref[...] = res
        else:
            y = res * lax.rsqrt(jnp.mean(res * res, axis=-1, keepdims=True) + EPS) * gf_ref[...]

            @pl.when(in_prompt)
            def _():
                yp_ref[...] = y

            @pl.when(jnp.logical_not(in_prompt))
            def _():
                ys_ref[...] = y


def _ffn(x, g, mod, shift_chunk, scale_chunk, gate_chunk, w_up, cw, cb, w_down, *,
         p_rows, p_len, s_len, tm=512, final_gain=None):
    t, d = x.shape
    ff = w_up.shape[1] // 2
    sub = FFN_TILE
    assert ff % (2 * sub) == 0
    assert (tm % p_len == 0 or p_len % tm == 0) and p_rows % tm == 0 and s_len % tm == 0
    nj = ff // (2 * sub)
    nt = ff // sub
    hs = 16
    per = tm // hs
    last_blk = t // hs - 1

    def tile_a(i, j):
        return jnp.minimum(2 * j, nt - 2)

    def tile_b(i, j):
        return jnp.minimum(2 * j + 1, nt - 1)

    def parked_b(i, j):
        return jnp.maximum(2 * j - 1, 1)

    def tile_specs(conv_tile, down_tile):
        return [
            pl.BlockSpec((None, 2 * (taps + 1), sub), lambda i, j: (conv_tile(i, j), 0, 0)),
            pl.BlockSpec((sub, d), lambda i, j: (down_tile(i, j), 0)),
        ]

    taps = cw.shape[0]
    conv_p = jnp.concatenate([cw[:, :ff], cb[None, :ff], cw[:, ff:], cb[None, ff:]], axis=0)
    conv_p = conv_p.reshape(2 * (taps + 1), nt, sub).transpose(1, 0, 2)

    in_specs = [
        pl.BlockSpec((tm, d), lambda i, j: (i, 0)),
        pl.BlockSpec((hs, d), lambda i, j: (jnp.maximum(i * per - 1, 0), 0)),
        pl.BlockSpec((hs, d), lambda i, j: (jnp.minimum((i + 1) * per, last_blk), 0)),
        pl.BlockSpec((1, d), lambda i, j: (0, 0)),
        _mod_spec(shift_chunk, d, tm, p_rows, s_len),
        _mod_spec(scale_chunk, d, tm, p_rows, s_len),
        _mod_spec(gate_chunk, d, tm, p_rows, s_len),
        pl.BlockSpec((None, d, 2 * sub), lambda i, j: (jnp.minimum(j, nj - 1), 0, 0)),
        pl.BlockSpec((None, d, 2 * sub), lambda i, j: (jnp.minimum(j, nj - 1) + nj, 0, 0)),
    ] + tile_specs(tile_a, tile_a) + tile_specs(tile_b, parked_b)
    w_up = _column_tiles(w_up, 2 * sub)
    tile_args = [conv_p, w_down.astype(BF16)]
    final = final_gain is not None
    kern = functools.partial(_ffn_kernel, tm=tm, nj=nj, p_rows=p_rows, p_len=p_len,
                             s_len=s_len, final=final)
    args = [x, x, x, g.reshape(1, d), mod, mod, mod, w_up, w_up, *tile_args, *tile_args]
    if final:
        pt = p_rows // tm
        in_specs.append(pl.BlockSpec((1, d), lambda i, j: (0, 0)))
        args.append(final_gain.reshape(1, d))
        out_specs = [pl.BlockSpec((tm, d), lambda i, j: (jnp.minimum(i, pt - 1), 0)),
                     pl.BlockSpec((tm, d), lambda i, j: (jnp.maximum(i - pt, 0), 0))]
        out_shape = [jax.ShapeDtypeStruct((p_rows, d), F32),
                     jax.ShapeDtypeStruct((t - p_rows, d), F32)]
    else:
        out_specs = pl.BlockSpec((tm, d), lambda i, j: (i, 0))
        out_shape = jax.ShapeDtypeStruct((t, d), F32)
    return pl.pallas_call(
        kern,
        grid=(t // tm, nj + 1),
        in_specs=in_specs,
        out_specs=out_specs,
        out_shape=out_shape,
        scratch_shapes=[pltpu.VMEM((tm + 2 * hs, d), BF16),
                        pltpu.VMEM((tm, sub), BF16),
                        pltpu.VMEM((tm + 2 * hs, sub), F32),
                        pltpu.VMEM((tm + 2 * hs, sub), F32),
                        pltpu.VMEM((tm, d), F32)],
        compiler_params=_params("arbitrary", "arbitrary"),
        name="conv_ffn",
    )(*args)


def _rope_tables(seq_len, hd):
    pos = jnp.arange(seq_len)
    row = (pos // GRID_W).astype(F32)
    col = (pos % GRID_W).astype(F32)
    nf = hd // 4
    inv = ROPE_BASE ** (-jnp.arange(nf, dtype=F32) / nf)
    ar = row[:, None] * inv
    ac = col[:, None] * inv
    cos = jnp.concatenate([jnp.cos(ar), jnp.cos(ar), jnp.cos(ac), jnp.cos(ac)], axis=1)
    sin = jnp.concatenate([-jnp.sin(ar), jnp.sin(ar), -jnp.sin(ac), jnp.sin(ac)], axis=1)
    return cos, sin


def kernel(x_prompt, x_sample, c, cache_hgrn_state, cache_attn_k, cache_attn_v, c_ctx, w_mod, b_mod, norm_mix, norm_ffn, w_hgrn_in, hgrn_lb_logits, hgrn_onorm, w_hgrn_out, w_attn_in, attn_lambda, attn_subln, w_attn_out, w_ffn_up, ffn_conv_w, ffn_conv_b, w_ffn_down, norm_final):
    batch, p_len, d = x_prompt.shape
    dec_batch, s_len, _ = x_sample.shape
    depth = w_mod.shape[0]
    p_rows = batch * p_len
    s_rows = dec_batch * s_len
    hgrn_heads = d // LANES
    diff_heads = d // (2 * LANES)
    hd = LANES
    assert 1 + dec_batch <= MOD_ROWS and p_rows % s_len == 0

    x = [x_prompt.reshape(p_rows, d), x_sample.reshape(s_rows, d)]
    cvec = jnp.concatenate(
        [c_ctx[None, :], c, jnp.zeros((MOD_ROWS - 1 - dec_batch, d), F32)], axis=0)
    mods = _modulation(cvec, w_mod, b_mod)
    seg = dict(p_rows=p_rows, s_len=s_len)

    hgrn_states, attn_ks, attn_vs = [], [], []
    for l in range(depth):
        mod = mods[l][:, None, :]
        j = l // N_MIXERS
        if l % N_MIXERS == 0:
            proj = _project(x, norm_mix[l], mod, 0, 1, w_hgrn_in[j],
                            row_off=0, rows=p_rows + s_rows, out_dtype=F32, tm=1024, tn=1024,
                            **seg)
            mix_p, st = _hgrn_scan(proj, hgrn_lb_logits, hgrn_onorm[j], None, layer_j=j,
                                   n_seq=batch, seq_len=p_len, row_off=0, heads=hgrn_heads,
                                   emit_state=True)
            hgrn_states.append(st)
            mix_s, = _hgrn_scan(proj, hgrn_lb_logits, hgrn_onorm[j], cache_hgrn_state,
                                layer_j=j, n_seq=dec_batch, seq_len=s_len, row_off=p_rows,
                                heads=hgrn_heads, emit_state=False)
            w_out = w_hgrn_out[j]
        else:
            lam_init = 0.8 - 0.6 * math.exp(-0.3 * l)
            q_scale = hd ** -0.5 * LOG2_E
            w_in = w_attn_in[j]
            qkv_p = _project(x, norm_mix[l], mod, 0, 1, w_in, row_off=0, rows=p_rows,
                             out_dtype=F32, tn=1024, split=3, **seg)
            attn_ks.append(qkv_p[1].reshape(batch, p_len, 2 * diff_heads, hd))
            attn_vs.append(qkv_p[2].reshape(batch, p_len, diff_heads, 2 * hd))
            mix_p = _diff_attention(qkv_p, None, None, attn_lambda[j], attn_subln[j],
                                    n_seq=batch, seq_len=p_len, heads=diff_heads,
                                    lam_init=lam_init, q_scale=q_scale, tq=p_len)
            qkv_s = _project(x, norm_mix[l], mod, 0, 1, w_in, row_off=p_rows, rows=s_rows,
                             out_dtype=BF16, rope_tables=_rope_tables(s_len, hd),
                             q_scale=q_scale, tm=1024, tn=1024, **seg)
            n_attn, past = cache_attn_k.shape[1:3]
            mix_s = _diff_attention(qkv_s, cache_attn_k.reshape(dec_batch, n_attn, past, d),
                                    cache_attn_v.reshape(dec_batch, n_attn, past, d),
                                    attn_lambda[j], attn_subln[j], n_seq=dec_batch,
                                    seq_len=s_len, heads=diff_heads, lam_init=lam_init,
                                    q_scale=1.0, tq=512, cache_layer=j)
            w_out = w_attn_out[j]
        x = _out_project([mix_p, mix_s], w_out.astype(BF16), x, mod, 2, **seg)
        last = l == depth - 1
        x = _ffn(x, norm_ffn[l], mod, 3, 4, 5, w_ffn_up[l], ffn_conv_w[l], ffn_conv_b[l],
                 w_ffn_down[l], p_len=p_len, final_gain=norm_final if last else None, **seg)
        x = list(x) if last else [x]

    y_prompt = x[0].reshape(batch, p_len, d)
    y_sample = x[1].reshape(dec_batch, s_len, d)
    new_hgrn_state = jnp.stack(hgrn_states, axis=1)
    new_attn_k = jnp.stack(attn_ks, axis=1)
    new_attn_v = jnp.stack(attn_vs, axis=1)
    return (y_prompt, y_sample, new_hgrn_state, new_attn_k, new_attn_v)
```

```python
import functools
import math

import jax
import jax.numpy as jnp
from jax import lax
from jax.experimental import pallas as pl
from jax.experimental.pallas import tpu as pltpu

F32 = jnp.float32
BF16 = jnp.bfloat16

EPS = 1e-6
GRID_W = 64
ROPE_BASE = 10000.0
N_MIXERS = 2

LANES = 128
SUBLANES = 8
MXU_COLS = 256
MOD_ROWS = 16
HGRN_CHUNK = 64
HGRN_SUB = 16
HGRN_GROUP = 16
ATTN_ROW_BLOCK = 128
ATTN_LOOKAHEAD = 1
FFN_TILE = 256
LOG2_E = 1.4426950408889634
VMEM_LIMIT = 56 * 1024 * 1024

NT_DIMS = (((1,), (1,)), ((), ()))
TN_DIMS = (((0,), (0,)), ((), ()))


def _params(*sem):
    return pltpu.CompilerParams(dimension_semantics=sem, vmem_limit_bytes=VMEM_LIMIT)


def _tile(n, want):
    best = LANES
    for cand in range(LANES, min(n, want) + 1, LANES):
        if n % cand == 0:
            best = cand
    assert n % best == 0
    return best


def _column_tiles(w, tn):
    k, n = w.shape
    return w.astype(BF16).reshape(k, n // tn, tn).transpose(1, 0, 2)


def _silu(x):
    return x * jax.nn.sigmoid(x)


def _split_bf16(x):
    hi = x.astype(BF16)
    lo = (x - hi.astype(F32)).astype(BF16)
    return hi, lo


def _mod_kernel(c_ref, w_ref, b_ref, o_ref):
    a_hi, a_lo = _split_bf16(_silu(c_ref[...]))
    w_hi, w_lo = _split_bf16(w_ref[...])
    rows = a_hi.shape[0]
    both = jnp.dot(jnp.concatenate([a_hi, a_lo], axis=0), w_hi, preferred_element_type=F32)
    acc = both[:rows] + both[rows:] + jnp.dot(a_hi, w_lo, preferred_element_type=F32)
    o_ref[...] = acc + b_ref[...]


def _modulation(cvec, w_mod, b_mod, tn=1024):
    depth, d, n = w_mod.shape
    tn = _tile(n, tn)
    return pl.pallas_call(
        _mod_kernel,
        grid=(depth, n // tn),
        in_specs=[
            pl.BlockSpec((MOD_ROWS, d), lambda l, j: (0, 0)),
            pl.BlockSpec((None, d, tn), lambda l, j: (l, 0, j)),
            pl.BlockSpec((None, 1, tn), lambda l, j: (l, 0, j)),
        ],
        out_specs=pl.BlockSpec((None, MOD_ROWS, tn), lambda l, j: (l, 0, j)),
        out_shape=jax.ShapeDtypeStruct((depth, MOD_ROWS, n), F32),
        compiler_params=_params("arbitrary", "arbitrary"),
        name="modulation",
    )(cvec, w_mod, b_mod.reshape(depth, 1, n))


def _norm_mod(x, g, shift, scale):
    y = x * lax.rsqrt(jnp.mean(x * x, axis=-1, keepdims=True) + EPS) * g
    return y * (1.0 + scale) + shift


def _seq_of_row(row0, p_rows, s_len):
    return jnp.where(row0 < p_rows, 0, 1 + (row0 - p_rows) // s_len)


def _mod_spec(chunk, d, tm, p_rows, s_len, tile_off=0):
    return pl.BlockSpec(
        (None, 1, d),
        lambda i, j: (_seq_of_row((i + tile_off) * tm, p_rows, s_len), 0, chunk))


def _proj_kernel(*refs, x_bounds, tile_off, rope, q_tiles, qk_tiles, q_scale, split):
    x_refs = refs[:len(x_bounds)]
    g_ref, sh_ref, sc_ref, w_ref, *rest = refs[len(x_bounds):]
    if rope:
        cos_ref, sin_ref, o_ref, h_ref = rest
    elif split:
        *o_refs, h_ref = rest
    else:
        o_ref, h_ref = rest
    tile = pl.program_id(0) + tile_off
    j = pl.program_id(1)

    for x_ref, (lo, cnt) in zip(x_refs, x_bounds):
        @pl.when((j == 0) & (tile >= lo) & (tile < lo + cnt))
        def _(x_ref=x_ref):
            h_ref[...] = _norm_mod(x_ref[...], g_ref[...], sh_ref[...],
                                   sc_ref[...]).astype(BF16)

    if split:
        lo = 0
        for cnt, out in zip(split, o_refs):
            @pl.when((j >= lo) & (j < lo + cnt))
            def _(out=out):
                out[...] = jnp.dot(h_ref[...], w_ref[...],
                                   preferred_element_type=F32).astype(out.dtype)
            lo += cnt
        return

    if not rope:
        o_ref[...] = jnp.dot(h_ref[...], w_ref[...],
                             preferred_element_type=F32).astype(o_ref.dtype)
        return

    @pl.when(j < qk_tiles)
    def _():
        tn = o_ref.shape[1]
        sub = min(tn, MXU_COLS)
        h = h_ref[...]
        accs = [jnp.dot(h, w_ref[:, c:c + sub], preferred_element_type=F32)
                for c in range(0, tn, sub)]
        scale = jnp.where(j < q_tiles, q_scale, 1.0)
        cos = cos_ref[...] * scale
        sin = sin_ref[...] * scale
        lane = lax.broadcasted_iota(jnp.int32, cos.shape, 1)
        first_of_pair = (lane // (LANES // 4)) % 2 == 0
        for ci, acc in enumerate(accs):
            for s in range(sub // LANES):
                xs = acc[:, s * LANES:(s + 1) * LANES]
                partner = jnp.where(first_of_pair,
                                    pltpu.roll(xs, LANES - LANES // 4, 1),
                                    pltpu.roll(xs, LANES // 4, 1))
                col = ci * sub + s * LANES
                o_ref[:, col:col + LANES] = (xs * cos + partner * sin).astype(o_ref.dtype)

    @pl.when(j >= qk_tiles)
    def _():
        o_ref[...] = jnp.dot(h_ref[...], w_ref[...],
                             preferred_element_type=F32).astype(o_ref.dtype)


def _row_segment_specs(segs, tm, tile_off=0, buffers=None):
    specs, bounds, lo = [], [], 0
    for k, a in enumerate(segs):
        mode = {} if buffers is None or buffers[k] is None else dict(
            pipeline_mode=pl.Buffered(buffers[k]))
        cnt = a.shape[0] // tm
        specs.append(pl.BlockSpec(
            (tm, a.shape[1]),
            lambda i, j, lo=lo, cnt=cnt: (jnp.clip(i + tile_off - lo, 0, cnt - 1), 0), **mode))
        bounds.append((lo, cnt))
        lo += cnt
    return specs, tuple(bounds)


def _project(x_segs, g, mod, shift_chunk, scale_chunk, w, *, row_off, rows, out_dtype,
             p_rows, s_len, tm=512, tn=512, rope_tables=None, q_scale=1.0, sections=None):
    d = x_segs[0].shape[1]
    n = w.shape[1]
    tm = math.gcd(tm, p_rows, s_len)
    tile_off = row_off // tm
    rope = rope_tables is not None
    tn = _tile(n // 3 if rope else math.gcd(n, *(sections or [n])), tn)
    split = tuple(c // tn for c in sections) if sections else ()
    biggest = max(range(len(x_segs)), key=lambda k: x_segs[k].shape[0])
    x_specs, x_bounds = _row_segment_specs(
        x_segs, tm, tile_off,
        buffers=[None if k == biggest else 1 for k in range(len(x_segs))])
    in_specs = x_specs + [
        pl.BlockSpec((1, d), lambda i, j: (0, 0)),
        _mod_spec(shift_chunk, d, tm, p_rows, s_len, tile_off),
        _mod_spec(scale_chunk, d, tm, p_rows, s_len, tile_off),
        pl.BlockSpec((d, tn), lambda i, j: (0, j)),
    ]
    args = list(x_segs) + [g.reshape(1, d), mod, mod, w.astype(BF16)]
    q_tiles = qk_tiles = 0
    if rope:
        cos, sin = rope_tables
        tiles_per_seq = s_len // tm
        in_specs += [pl.BlockSpec((tm, LANES), lambda i, j: (i % tiles_per_seq, 0))] * 2
        args += [cos, sin]
        q_tiles = (n // 3) // tn
        qk_tiles = 2 * q_tiles
    kern = functools.partial(_proj_kernel, x_bounds=x_bounds, tile_off=tile_off, rope=rope,
                             q_tiles=q_tiles, qk_tiles=qk_tiles, q_scale=q_scale, split=split)
    if split:
        starts = [sum(split[:s]) for s in range(len(split))]
        out_specs = [pl.BlockSpec((tm, tn),
                                  lambda i, j, lo=lo, cnt=cnt: (i, jnp.clip(j - lo, 0, cnt - 1)))
                     for lo, cnt in zip(starts, split)]
        dtypes = out_dtype if isinstance(out_dtype, (list, tuple)) else [out_dtype] * len(split)
        out_shape = [jax.ShapeDtypeStruct((rows, c), dt) for c, dt in zip(sections, dtypes)]
    else:
        out_specs = pl.BlockSpec((tm, tn), lambda i, j: (i, j))
        out_shape = jax.ShapeDtypeStruct((rows, n), out_dtype)
    return pl.pallas_call(
        kern,
        grid=(rows // tm, n // tn),
        in_specs=in_specs,
        out_specs=out_specs,
        out_shape=out_shape,
        scratch_shapes=[pltpu.VMEM((tm, d), BF16)],
        compiler_params=_params("arbitrary", "arbitrary"),
        name="norm_mod_project",
    )(*args)


def _hgrn_group_local(groups):
    c, m = HGRN_CHUNK, HGRN_SUB
    nb = c // m
    t = lax.broadcasted_iota(jnp.int32, (c, c), 0)
    s = lax.broadcasted_iota(jnp.int32, (c, c), 1)

    jobs = []
    for q, z, v, lb, direction in groups:
        dk = q.shape[1]
        tri = (s <= t) if direction == 0 else (s >= t)
        cum = jnp.where(tri, 1.0, 0.0).astype(BF16)
        f = lb + (1.0 - lb) * jax.nn.sigmoid(z)
        g2 = jnp.log(f) * LOG2_E
        k = 1.0 - f
        g_hi = g2.astype(BF16)
        g_lo = (g2 - g_hi.astype(F32)).astype(BF16)
        v_bf = v.astype(BF16)
        for ci in range(q.shape[0] // c):
            rows = slice(ci * c, (ci + 1) * c)
            jobs.append(dict(direction=direction, tri=tri, cum=cum, q=q[rows], k=k[rows],
                             v=v_bf[rows], g=jnp.concatenate([g_hi[rows], g_lo[rows]], axis=1)))

    def padded(x, lo, hi):
        parts = [jnp.zeros((lo, dk), BF16)] if lo else []
        parts.append(x)
        if hi < c:
            parts.append(jnp.zeros((c - hi, dk), BF16))
        return jnp.concatenate(parts, axis=0) if len(parts) > 1 else x

    for job in jobs:
        job["sums"] = jnp.dot(job["cum"], job["g"], preferred_element_type=F32)

    for job in jobs:
        direction = job["direction"]
        sums = job["sums"]
        b = sums[:, :dk] + sums[:, dk:]
        anchors = []
        for i in range(nb):
            a_row = i * m + (m // 2 - 1 if direction == 0 else m // 2)
            anchors.append(b[a_row:a_row + 1, :])
        b_anchor = jnp.concatenate([jnp.broadcast_to(r, (m, dk)) for r in anchors], axis=0)
        q_rel = (job["q"] * jnp.exp2(b - b_anchor)).astype(BF16)
        full = nb - 1 if direction == 0 else 0
        q_cat, k_cat = [], []
        k_full = None
        for i in range(nb):
            lo, hi = (0, m * (i + 1)) if direction == 0 else (m * i, c)
            ki = job["k"][lo:hi] * jnp.exp2(anchors[i] - b[lo:hi])
            if i == full:
                k_full = ki
            k_cat.append(padded(ki.astype(BF16), lo, hi))
            q_cat.append(padded(q_rel[m * i:m * (i + 1)], m * i, m * (i + 1)))
        job["a"] = lax.dot_general(jnp.concatenate(q_cat, axis=1),
                                   jnp.concatenate(k_cat, axis=1), NT_DIMS,
                                   preferred_element_type=F32)
        edge = c - 1 if direction == 0 else 0
        total = b[edge:edge + 1, :]
        job["k_dec"] = (k_full * jnp.exp2(total - anchors[full])).astype(BF16)
        job["q_in"] = (job["q"] * jnp.exp2(b)).astype(BF16)
        job["decay"] = jnp.exp2(total)

    for job in jobs:
        job["u_t"] = lax.dot_general(job["v"], job["k_dec"], TN_DIMS,
                                     preferred_element_type=F32)

    out, pos = [], 0
    for q, *_ in groups:
        n = q.shape[0] // c
        out.append([(jnp.where(j["tri"], j["a"], 0.0).astype(BF16), j["v"], j["q_in"], j["u_t"],
                     j["decay"]) for j in jobs[pos:pos + n]])
        pos += n
    return out


def _hgrn_kernel(*refs, seq_len, layer_j, n_lb, has_s0, emit_state):
    q_ref, zf_ref, zb_ref, i_ref, g_ref, lg_ref, on_ref = refs[:7]
    pos = 7
    s0_ref = st_ref = None
    if has_s0:
        s0_ref = refs[pos]
        pos += 1
    o_ref = refs[pos]
    pos += 1
    if emit_state:
        st_ref = refs[pos]
        pos += 1
    of_scr, ob_scr = refs[pos], refs[pos + 1]

    grp = math.gcd(HGRN_GROUP, seq_len // HGRN_CHUNK)
    rows = grp * HGRN_CHUNK
    n = seq_len // rows
    dv = q_ref.shape[1]

    lbs = []
    for d in range(2):
        lg = lg_ref[d * n_lb:(d + 1) * n_lb, :]
        e = jnp.exp(lg - jnp.max(lg, axis=0, keepdims=True))
        lbs.append(jnp.sum(e[:layer_j + 1], axis=0, keepdims=True) / jnp.sum(e, axis=0, keepdims=True))

    if has_s0:
        st0 = (s0_ref[0].T, s0_ref[1].T)
    else:
        st0 = (jnp.zeros((dv, dv), F32), jnp.zeros((dv, dv), F32))

    def advance(local, st):
        _, _, q_in, u_t, decay = local
        o = lax.dot_general(q_in, st.astype(BF16), NT_DIMS, preferred_element_type=F32)
        return st * decay + u_t, o

    def add_intra(local, o_inter):
        a, v = local[:2]
        return o_inter + jnp.dot(a, v, preferred_element_type=F32)

    def body(gi, carry):
        st_f, st_b = carry
        rf = pl.ds(pl.multiple_of(gi * rows, rows), rows)
        rb = pl.ds(pl.multiple_of((n - 1 - gi) * rows, rows), rows)
        loc_f, loc_b = _hgrn_group_local([
            (q_ref[rf, :], zf_ref[rf, :], i_ref[rf, :], lbs[0], 0),
            (q_ref[rb, :], zb_ref[rb, :], i_ref[rb, :], lbs[1], 1)])
        o_f, o_b = [None] * grp, [None] * grp
        for ci in range(grp):
            st_f, o_f[ci] = advance(loc_f[ci], st_f)
            st_b, o_b[grp - 1 - ci] = advance(loc_b[grp - 1 - ci], st_b)
        o_f = [add_intra(l, o) for l, o in zip(loc_f, o_f)]
        o_b = [add_intra(l, o) for l, o in zip(loc_b, o_b)]
        of_scr[rf, :] = jnp.concatenate(o_f, axis=0)
        ob_scr[rb, :] = jnp.concatenate(o_b, axis=0)
        return st_f, st_b

    st_f, st_b = lax.fori_loop(0, n, body, st0)
    if emit_state:
        st_ref[0] = st_f.T
        st_ref[1] = st_b.T

    fin_rows = math.gcd(seq_len, 256)
    onorm = on_ref[...]

    def finish(ri, _):
        r = pl.ds(pl.multiple_of(ri * fin_rows, fin_rows), fin_rows)
        o = of_scr[r, :] + ob_scr[r, :]
        y = o * lax.rsqrt(jnp.mean(o * o, axis=-1, keepdims=True) + EPS) * onorm
        o_ref[r, :] = (y * _silu(g_ref[r, :].astype(F32))).astype(o_ref.dtype)
        return 0

    lax.fori_loop(0, seq_len // fin_rows, finish, 0)


def _hgrn_scan(proj, lb_logits, onorm, s0, *, layer_j, n_seq, seq_len, row_off, heads, emit_state):
    dk = LANES
    d = heads * dk
    blk_off = row_off // seq_len
    n_lb = lb_logits.shape[1]

    def sec(k):
        return pl.BlockSpec((seq_len, dk), lambda b, h: (b + blk_off, k * heads + h))

    qig, z = proj
    in_specs = [sec(0), sec(0), sec(1), sec(1), sec(2),
                pl.BlockSpec((2 * n_lb, dk), lambda b, h: (0, h)),
                pl.BlockSpec((1, dk), lambda b, h: (0, 0))]
    args = [qig, z, z, qig, qig, lb_logits.reshape(2 * n_lb, d), onorm.reshape(1, dk)]
    if s0 is not None:
        in_specs.append(pl.BlockSpec((None, None, 2, None, dk, dk),
                                     lambda b, h: (b, layer_j, 0, h, 0, 0)))
        args.append(s0)
    out_shape = [jax.ShapeDtypeStruct((n_seq * seq_len, d), BF16)]
    out_specs = [pl.BlockSpec((seq_len, dk), lambda b, h: (b, h))]
    if emit_state:
        out_shape.append(jax.ShapeDtypeStruct((n_seq, 2, heads, dk, dk), F32))
        out_specs.append(pl.BlockSpec((None, 2, None, dk, dk), lambda b, h: (b, 0, h, 0, 0)))
    kern = functools.partial(_hgrn_kernel, seq_len=seq_len, layer_j=layer_j, n_lb=n_lb,
                             has_s0=s0 is not None, emit_state=emit_state)
    return pl.pallas_call(
        kern,
        grid=(n_seq, heads),
        in_specs=in_specs,
        out_specs=out_specs,
        out_shape=out_shape,
        scratch_shapes=[pltpu.VMEM((seq_len, dk), F32), pltpu.VMEM((seq_len, dk), F32)],
        compiler_params=_params("arbitrary", "arbitrary"),
        name="hgrn_scan",
    )(*args)


def _attn_kernel(*refs, has_cache, lam_init, q_scale):
    q_ref, k_ref, v_ref = refs[:3]
    pos = 3
    kc_ref = vc_ref = None
    if has_cache:
        kc_ref, vc_ref = refs[3], refs[4]
        pos = 5
    lam_ref, sub_ref, o_ref = refs[pos:pos + 3]

    lp = lam_ref[...]
    lam = (jnp.exp(jnp.sum(lp[0:1] * lp[1:2], axis=-1, keepdims=True))
           - jnp.exp(jnp.sum(lp[2:3] * lp[3:4], axis=-1, keepdims=True)) + lam_init)

    hd = LANES
    v_new = v_ref[...].astype(BF16)
    v_old = vc_ref[...].astype(BF16) if has_cache else None
    tq = q_ref.shape[0]
    rb = min(tq, ATTN_ROW_BLOCK)
    chains = [(r, j) for r in range(0, tq, rb) for j in range(2)]
    def score(r, j):
        cols = slice(j * hd, (j + 1) * hd)
        qj = q_ref[r:r + rb, cols]
        if q_scale != 1.0:
            qj = qj.astype(F32) * q_scale
        qj = qj.astype(BF16)
        s_new = lax.dot_general(qj, k_ref[:, cols].astype(BF16), NT_DIMS,
                                preferred_element_type=F32)
        s_old = None
        if has_cache:
            s_old = lax.dot_general(qj, kc_ref[:, cols].astype(BF16), NT_DIMS,
                                    preferred_element_type=F32)
        return s_new, s_old

    ahead = min(ATTN_LOOKAHEAD, len(chains))
    scores = [score(*c) for c in chains[:ahead]]
    outs = {}
    for ci, (r, j) in enumerate(chains):
        if ci + ahead < len(chains):
            scores.append(score(*chains[ci + ahead]))
        s_new, s_old = scores[ci]
        mx = jnp.max(s_new, axis=-1, keepdims=True)
        if has_cache:
            mx = jnp.maximum(mx, jnp.max(s_old, axis=-1, keepdims=True))
        p_new = jnp.exp2(s_new - mx)
        den = jnp.sum(p_new, axis=-1, keepdims=True)
        acc = jnp.dot(p_new.astype(BF16), v_new, preferred_element_type=F32)
        if has_cache:
            p_old = jnp.exp2(s_old - mx)
            den += jnp.sum(p_old, axis=-1, keepdims=True)
            acc += jnp.dot(p_old.astype(BF16), v_old, preferred_element_type=F32)
        outs[r, j] = acc / den
    sub = sub_ref[...]
    for r in range(0, tq, rb):
        o = outs[r, 0] - lam * outs[r, 1]
        y = o * lax.rsqrt(jnp.mean(o * o, axis=-1, keepdims=True) + EPS) * sub
        o_ref[r:r + rb, :] = (y * (1.0 - lam_init)).astype(o_ref.dtype)


def _diff_attention(qkv, cache_k, cache_v, lam_p, subln, *, n_seq, seq_len, heads, lam_init,
                    q_scale, tq, cache_layer=0):
    hd2 = 2 * LANES
    d = heads * hd2
    qb = seq_len // tq
    fused = not isinstance(qkv, (list, tuple))
    k_off, v_off = (heads, 2 * heads) if fused else (0, 0)
    in_specs = [
        pl.BlockSpec((tq, hd2), lambda b, h, i: (b * qb + i, h)),
        pl.BlockSpec((seq_len, hd2), lambda b, h, i: (b, k_off + h)),
        pl.BlockSpec((seq_len, hd2), lambda b, h, i: (b, v_off + h)),
    ]
    args = [qkv, qkv, qkv] if fused else list(qkv)
    if cache_k is not None:
        past = cache_k.shape[2]
        in_specs += [pl.BlockSpec((None, None, past, hd2),
                                  lambda b, h, i: (b, cache_layer, 0, h))] * 2
        args += [cache_k, cache_v]
    in_specs += [pl.BlockSpec(lam_p.shape, lambda b, h, i: (0, 0)),
                 pl.BlockSpec((1, hd2), lambda b, h, i: (0, 0))]
    args += [lam_p, subln.reshape(1, hd2)]
    kern = functools.partial(_attn_kernel, has_cache=cache_k is not None, lam_init=lam_init,
                             q_scale=q_scale)
    return pl.pallas_call(
        kern,
        grid=(n_seq, heads, qb),
        in_specs=in_specs,
        out_specs=pl.BlockSpec((tq, hd2), lambda b, h, i: (b * qb + i, h)),
        out_shape=jax.ShapeDtypeStruct((n_seq * seq_len, d), BF16),
        compiler_params=_params("arbitrary", "arbitrary", "arbitrary"),
        name="diff_attention",
    )(*args)


def _out_proj_kernel(*refs, a_bounds, x_bounds):
    a_refs = refs[:len(a_bounds)]
    x_refs = refs[len(a_bounds):len(a_bounds) + len(x_bounds)]
    w_ref, gate_ref, o_ref = refs[len(a_bounds) + len(x_bounds):]
    i = pl.program_id(0)
    for a_ref, (a_lo, a_cnt) in zip(a_refs, a_bounds):
        for x_ref, (x_lo, x_cnt) in zip(x_refs, x_bounds):
            lo, hi = max(a_lo, x_lo), min(a_lo + a_cnt, x_lo + x_cnt)
            if lo >= hi:
                continue

            @pl.when((i >= lo) & (i < hi))
            def _(a_ref=a_ref, x_ref=x_ref):
                y = jnp.dot(a_ref[...], w_ref[...], preferred_element_type=F32)
                o_ref[...] = x_ref[...] + gate_ref[...] * y


def _out_project(a_segs, w, x_segs, mod, gate_chunk, *, p_rows, s_len, tm=512):
    d = x_segs[0].shape[1]
    t = sum(x.shape[0] for x in x_segs)
    kdim = w.shape[0]
    a_specs, a_bounds = _row_segment_specs(a_segs, tm)
    x_specs, x_bounds = _row_segment_specs(x_segs, tm)
    in_specs = a_specs + x_specs + [
        pl.BlockSpec((kdim, d), lambda i, j: (0, 0)),
        _mod_spec(gate_chunk, d, tm, p_rows, s_len),
    ]
    return pl.pallas_call(
        functools.partial(_out_proj_kernel, a_bounds=a_bounds, x_bounds=x_bounds),
        grid=(t // tm, 1),
        in_specs=in_specs,
        out_specs=pl.BlockSpec((tm, d), lambda i, j: (i, 0)),
        out_shape=jax.ShapeDtypeStruct((t, d), F32),
        compiler_params=_params("arbitrary", "arbitrary"),
        name="out_project_residual",
    )(*a_segs, *x_segs, w, mod)


def _ffn_kernel(x_ref, xp_ref, xn_ref, g_ref, sh_ref, sc_ref, gate_ref, wg_ref, wv_ref,
                conv_a, wd_a, conv_b, wd_b, *rest, tm, nj, p_rows, p_len, s_len, final):
    if final:
        gf_ref, yp_ref, ys_ref, h_ref, act_ref, ug_scr, uv_scr, acc_ref = rest
    else:
        o_ref, h_ref, act_ref, ug_scr, uv_scr, acc_ref = rest
    i = pl.program_id(0)
    j = pl.program_id(1)
    hs = xp_ref.shape[0]
    sub = FFN_TILE
    ext = tm + 2 * hs
    row0 = i * tm
    in_prompt = row0 < p_rows
    inner_edges = jnp.where(in_prompt, 1.0, 0.0)
    win = lax.broadcasted_iota(jnp.int32, (2 * SUBLANES, 1), 0)

    def up(cols):
        h = h_ref[...]
        return (jnp.dot(h, wg_ref[:, cols], preferred_element_type=F32),
                jnp.dot(h, wv_ref[:, cols], preferred_element_type=F32))

    def conv(u_all, u_scr, cw, cb):
        u_scr[...] = u_all
        u = u_all[hs:hs + tm]
        prev = u_scr[hs - 1:hs - 1 + tm, :]
        nxt = u_scr[hs + 1:hs + 1 + tm, :]
        y = prev * cw[0:1] + u * cw[1:2] + nxt * cw[2:3] + cb
        pieces, done = [], 0
        for edge in range(p_len, tm, p_len):
            lo, hi = edge - SUBLANES, edge + SUBLANES
            leak = (jnp.where(win == SUBLANES - 1, nxt[lo:hi] * cw[2:3], 0.0)
                    + jnp.where(win == SUBLANES, prev[lo:hi] * cw[0:1], 0.0))
            pieces += [y[done:lo], y[lo:hi] - inner_edges * leak]
            done = hi
        return jnp.concatenate(pieces + [y[done:]], axis=0) if pieces else y

    def activate(ug, uv, conv_ref):
        p = conv_ref[...]
        gate = conv(ug, ug_scr, p[0:3], p[3:4])
        val = conv(uv, uv_scr, p[4:7], p[7:8])
        return (_silu(gate) * val).astype(BF16)

    def activate_a(ug, uv):
        return activate(ug, uv, conv_a)

    def project_down(act, wd):
        acc_ref[...] += jnp.dot(act, wd[...], preferred_element_type=F32)

    cols_a, cols_b = slice(0, sub), slice(sub, 2 * sub)

    @pl.when(j == 0)
    def _():
        g, sh, sc = g_ref[...], sh_ref[...], sc_ref[...]
        h_ref[hs:hs + tm, :] = _norm_mod(x_ref[...], g, sh, sc).astype(BF16)
        end = row0 + tm
        at_seq_end = jnp.where(in_prompt, lax.rem(end, p_len) == 0,
                               lax.rem(end - p_rows, s_len) == 0)
        at_seq_start = jnp.where(in_prompt, lax.rem(row0, p_len) == 0,
                                 lax.rem(row0 - p_rows, s_len) == 0)
        after = _norm_mod(xn_ref[...], g, sh, sc) * jnp.where(at_seq_end, 0.0, 1.0)
        before = _norm_mod(xp_ref[...], g, sh, sc) * jnp.where(at_seq_start, 0.0, 1.0)
        h_ref[0:hs, :] = before.astype(BF16)
        h_ref[hs + tm:ext, :] = after.astype(BF16)
        acc_ref[...] = jnp.zeros_like(acc_ref)
        ua = up(cols_a)
        act_a = activate_a(*ua)
        ub = up(cols_b)
        project_down(act_a, wd_a)
        act_ref[...] = activate(*ub, conv_b)

    @pl.when((j > 0) & (j < nj))
    def _():
        ua = up(cols_a)
        project_down(act_ref[...], wd_b)
        act_a = activate_a(*ua)
        ub = up(cols_b)
        project_down(act_a, wd_a)
        act_ref[...] = activate(*ub, conv_b)

    @pl.when(j == nj)
    def _():
        project_down(act_ref[...], wd_b)
        res = x_ref[...] + gate_ref[...] * acc_ref[...]
        if not final:
            o_ref[...] = res
        else:
            y = res * lax.rsqrt(jnp.mean(res * res, axis=-1, keepdims=True) + EPS) * gf_ref[...]

            @pl.when(in_prompt)
            def _():
                yp_ref[...] = y

            @pl.when(jnp.logical_not(in_prompt))
            def _():
                ys_ref[...] = y


def _ffn(x, g, mod, shift_chunk, scale_chunk, gate_chunk, w_up, cw, cb, w_down, *,
         p_rows, p_len, s_len, tm=512, final_gain=None):
    t, d = x.shape
    ff = w_up.shape[1] // 2
    sub = FFN_TILE
    assert ff % (2 * sub) == 0
    tm = math.gcd(tm, p_rows, s_len)
    assert (tm % p_len == 0 or p_len % tm == 0) and p_rows % tm == 0 and s_len % tm == 0
    nj = ff // (2 * sub)
    nt = ff // sub
    hs = 16
    per = tm // hs
    last_blk = t // hs - 1

    def tile_a(i, j):
        return jnp.minimum(2 * j, nt - 2)

    def tile_b(i, j):
        return jnp.minimum(2 * j + 1, nt - 1)

    def parked_b(i, j):
        return jnp.maximum(2 * j - 1, 1)

    def tile_specs(conv_tile, down_tile):
        return [
            pl.BlockSpec((None, 2 * (taps + 1), sub), lambda i, j: (conv_tile(i, j), 0, 0)),
            pl.BlockSpec((sub, d), lambda i, j: (down_tile(i, j), 0)),
        ]

    taps = cw.shape[0]
    conv_p = jnp.concatenate([cw[:, :ff], cb[None, :ff], cw[:, ff:], cb[None, ff:]], axis=0)
    conv_p = conv_p.reshape(2 * (taps + 1), nt, sub).transpose(1, 0, 2)

    in_specs = [
        pl.BlockSpec((tm, d), lambda i, j: (i, 0)),
        pl.BlockSpec((hs, d), lambda i, j: (jnp.maximum(i * per - 1, 0), 0)),
        pl.BlockSpec((hs, d), lambda i, j: (jnp.minimum((i + 1) * per, last_blk), 0)),
        pl.BlockSpec((1, d), lambda i, j: (0, 0)),
        _mod_spec(shift_chunk, d, tm, p_rows, s_len),
        _mod_spec(scale_chunk, d, tm, p_rows, s_len),
        _mod_spec(gate_chunk, d, tm, p_rows, s_len),
        pl.BlockSpec((None, d, 2 * sub), lambda i, j: (jnp.minimum(j, nj - 1), 0, 0)),
        pl.BlockSpec((None, d, 2 * sub), lambda i, j: (jnp.minimum(j, nj - 1) + nj, 0, 0)),
    ] + tile_specs(tile_a, tile_a) + tile_specs(tile_b, parked_b)
    w_up = _column_tiles(w_up, 2 * sub)
    tile_args = [conv_p, w_down.astype(BF16)]
    final = final_gain is not None
    kern = functools.partial(_ffn_kernel, tm=tm, nj=nj, p_rows=p_rows, p_len=p_len,
                             s_len=s_len, final=final)
    args = [x, x, x, g.reshape(1, d), mod, mod, mod, w_up, w_up, *tile_args, *tile_args]
    if final:
        pt = p_rows // tm
        in_specs.append(pl.BlockSpec((1, d), lambda i, j: (0, 0)))
        args.append(final_gain.reshape(1, d))
        out_specs = [pl.BlockSpec((tm, d), lambda i, j: (jnp.minimum(i, pt - 1), 0)),
                     pl.BlockSpec((tm, d), lambda i, j: (jnp.maximum(i - pt, 0), 0))]
        out_shape = [jax.ShapeDtypeStruct((p_rows, d), F32),
                     jax.ShapeDtypeStruct((t - p_rows, d), F32)]
    else:
        out_specs = pl.BlockSpec((tm, d), lambda i, j: (i, 0))
        out_shape = jax.ShapeDtypeStruct((t, d), F32)
    return pl.pallas_call(
        kern,
        grid=(t // tm, nj + 1),
        in_specs=in_specs,
        out_specs=out_specs,
        out_shape=out_shape,
        scratch_shapes=[pltpu.VMEM((tm + 2 * hs, d), BF16),
                        pltpu.VMEM((tm, sub), BF16),
                        pltpu.VMEM((tm + 2 * hs, sub), F32),
                        pltpu.VMEM((tm + 2 * hs, sub), F32),
                        pltpu.VMEM((tm, d), F32)],
        compiler_params=_params("arbitrary", "arbitrary"),
        name="conv_ffn",
    )(*args)


def _rope_tables(seq_len, hd):
    pos = jnp.arange(seq_len)
    row = (pos // GRID_W).astype(F32)
    col = (pos % GRID_W).astype(F32)
    nf = hd // 4
    inv = ROPE_BASE ** (-jnp.arange(nf, dtype=F32) / nf)
    ar = row[:, None] * inv
    ac = col[:, None] * inv
    cos = jnp.concatenate([jnp.cos(ar), jnp.cos(ar), jnp.cos(ac), jnp.cos(ac)], axis=1)
    sin = jnp.concatenate([-jnp.sin(ar), jnp.sin(ar), -jnp.sin(ac), jnp.sin(ac)], axis=1)
    return cos, sin


def kernel(x_prompt, x_sample, c, cache_hgrn_state, cache_attn_k, cache_attn_v, c_ctx, w_mod, b_mod, norm_mix, norm_ffn, w_hgrn_in, hgrn_lb_logits, hgrn_onorm, w_hgrn_out, w_attn_in, attn_lambda, attn_subln, w_attn_out, w_ffn_up, ffn_conv_w, ffn_conv_b, w_ffn_down, norm_final):
    batch, p_len, d = x_prompt.shape
    dec_batch, s_len, _ = x_sample.shape
    depth = w_mod.shape[0]
    p_rows = batch * p_len
    s_rows = dec_batch * s_len
    hgrn_heads = d // LANES
    diff_heads = d // (2 * LANES)
    hd = LANES
    assert 1 + dec_batch <= MOD_ROWS and p_rows % s_len == 0

    x = [x_prompt.reshape(p_rows, d), x_sample.reshape(s_rows, d)]
    cvec = jnp.concatenate(
        [c_ctx[None, :], c, jnp.zeros((MOD_ROWS - 1 - dec_batch, d), F32)], axis=0)
    mods = _modulation(cvec, w_mod, b_mod)
    seg = dict(p_rows=p_rows, s_len=s_len)

    hgrn_states, attn_ks, attn_vs = [], [], []
    for l in range(depth):
        mod = mods[l][:, None, :]
        j = l // N_MIXERS
        if l % N_MIXERS == 0:
            wq, wzf, wzb, wi, wg = jnp.split(w_hgrn_in[j], 5, axis=1)
            proj = _project(x, norm_mix[l], mod, 0, 1,
                            jnp.concatenate([wq, wi, wg, wzf, wzb], axis=1),
                            row_off=0, rows=p_rows + s_rows, sections=[3 * d, 2 * d],
                            out_dtype=[BF16, F32], tm=1024, tn=1024, **seg)
            mix_p, st = _hgrn_scan(proj, hgrn_lb_logits, hgrn_onorm[j], None, layer_j=j,
                                   n_seq=batch, seq_len=p_len, row_off=0, heads=hgrn_heads,
                                   emit_state=True)
            hgrn_states.append(st)
            mix_s, = _hgrn_scan(proj, hgrn_lb_logits, hgrn_onorm[j], cache_hgrn_state,
                                layer_j=j, n_seq=dec_batch, seq_len=s_len, row_off=p_rows,
                                heads=hgrn_heads, emit_state=False)
            w_out = w_hgrn_out[j]
        else:
            lam_init = 0.8 - 0.6 * math.exp(-0.3 * l)
            q_scale = hd ** -0.5 * LOG2_E
            w_in = w_attn_in[j]
            qkv_p = _project(x, norm_mix[l], mod, 0, 1, w_in, row_off=0, rows=p_rows,
                             out_dtype=F32, tn=1024, sections=[d, d, d], **seg)
            attn_ks.append(qkv_p[1].reshape(batch, p_len, 2 * diff_heads, hd))
            attn_vs.append(qkv_p[2].reshape(batch, p_len, diff_heads, 2 * hd))
            mix_p = _diff_attention(qkv_p, None, None, attn_lambda[j], attn_subln[j],
                                    n_seq=batch, seq_len=p_len, heads=diff_heads,
                                    lam_init=lam_init, q_scale=q_scale, tq=p_len)
            qkv_s = _project(x, norm_mix[l], mod, 0, 1, w_in, row_off=p_rows, rows=s_rows,
                             out_dtype=BF16, rope_tables=_rope_tables(s_len, hd),
                             q_scale=q_scale, tm=1024, tn=1024, **seg)
            n_attn, past = cache_attn_k.shape[1:3]
            mix_s = _diff_attention(qkv_s, cache_attn_k.reshape(dec_batch, n_attn, past, d),
                                    cache_attn_v.reshape(dec_batch, n_attn, past, d),
                                    attn_lambda[j], attn_subln[j], n_seq=dec_batch,
                                    seq_len=s_len, heads=diff_heads, lam_init=lam_init,
                                    q_scale=1.0, tq=512, cache_layer=j)
            w_out = w_attn_out[j]
        x = _out_project([mix_p, mix_s], w_out.astype(BF16), x, mod, 2, **seg)
        last = l == depth - 1
        x = _ffn(x, norm_ffn[l], mod, 3, 4, 5, w_ffn_up[l], ffn_conv_w[l], ffn_conv_b[l],
                 w_ffn_down[l], p_len=p_len, final_gain=norm_final if last else None, **seg)
        x = list(x) if last else [x]

    y_prompt = x[0].reshape(batch, p_len, d)
    y_sample = x[1].reshape(dec_batch, s_len, d)
    new_hgrn_state = jnp.stack(hgrn_states, axis=1)
    new_attn_k = jnp.stack(attn_ks, axis=1)
    new_attn_v = jnp.stack(attn_vs, axis=1)
    return (y_prompt, y_sample, new_hgrn_state, new_attn_k, new_attn_v)
```

```python
import functools
import math

import jax
import jax.numpy as jnp
from jax import lax
from jax.experimental import pallas as pl
from jax.experimental.pallas import tpu as pltpu

F32 = jnp.float32
BF16 = jnp.bfloat16

EPS = 1e-6
GRID_W = 64
ROPE_BASE = 10000.0
N_MIXERS = 2

LANES = 128
SUBLANES = 8
MXU_COLS = 256
MOD_ROWS = 16
HGRN_CHUNK = 64
HGRN_SUB = 16
HGRN_GROUP = 16
ATTN_ROW_BLOCK = 128
ATTN_LOOKAHEAD = 1
FFN_TILE = 256
LOG2_E = 1.4426950408889634
VMEM_LIMIT = 56 * 1024 * 1024

NT_DIMS = (((1,), (1,)), ((), ()))
TN_DIMS = (((0,), (0,)), ((), ()))


def _params(*sem):
    return pltpu.CompilerParams(dimension_semantics=sem, vmem_limit_bytes=VMEM_LIMIT)


def _tile(n, want):
    best = LANES
    for cand in range(LANES, min(n, want) + 1, LANES):
        if n % cand == 0:
            best = cand
    assert n % best == 0
    return best


def _column_tiles(w, tn):
    k, n = w.shape
    return w.astype(BF16).reshape(k, n // tn, tn).transpose(1, 0, 2)


def _silu(x):
    return x * jax.nn.sigmoid(x)


def _split_bf16(x):
    hi = x.astype(BF16)
    lo = (x - hi.astype(F32)).astype(BF16)
    return hi, lo


def _mod_kernel(c_ref, w_ref, b_ref, o_ref):
    a_hi, a_lo = _split_bf16(_silu(c_ref[...]))
    w_hi, w_lo = _split_bf16(w_ref[...])
    rows = a_hi.shape[0]
    both = jnp.dot(jnp.concatenate([a_hi, a_lo], axis=0), w_hi, preferred_element_type=F32)
    acc = both[:rows] + both[rows:] + jnp.dot(a_hi, w_lo, preferred_element_type=F32)
    o_ref[...] = acc + b_ref[...]


def _modulation(cvec, w_mod, b_mod, tn=1024):
    depth, d, n = w_mod.shape
    tn = _tile(n, tn)
    return pl.pallas_call(
        _mod_kernel,
        grid=(depth, n // tn),
        in_specs=[
            pl.BlockSpec((MOD_ROWS, d), lambda l, j: (0, 0)),
            pl.BlockSpec((None, d, tn), lambda l, j: (l, 0, j)),
            pl.BlockSpec((None, 1, tn), lambda l, j: (l, 0, j)),
        ],
        out_specs=pl.BlockSpec((None, MOD_ROWS, tn), lambda l, j: (l, 0, j)),
        out_shape=jax.ShapeDtypeStruct((depth, MOD_ROWS, n), F32),
        compiler_params=_params("arbitrary", "arbitrary"),
        name="modulation",
    )(cvec, w_mod, b_mod.reshape(depth, 1, n))


def _norm_mod(x, g, shift, scale):
    y = x * lax.rsqrt(jnp.mean(x * x, axis=-1, keepdims=True) + EPS) * g
    return y * (1.0 + scale) + shift


def _seq_of_row(row0, p_rows, s_len):
    return jnp.where(row0 < p_rows, 0, 1 + (row0 - p_rows) // s_len)


def _mod_spec(chunk, d, tm, p_rows, s_len, tile_off=0):
    return pl.BlockSpec(
        (None, 1, d),
        lambda i, j: (_seq_of_row((i + tile_off) * tm, p_rows, s_len), 0, chunk))


def _proj_kernel(*refs, x_bounds, tile_off, rope, q_tiles, qk_tiles, q_scale, split):
    x_refs = refs[:len(x_bounds)]
    g_ref, sh_ref, sc_ref, w_ref, *rest = refs[len(x_bounds):]
    if rope:
        cos_ref, sin_ref, o_ref, h_ref = rest
    elif split:
        *o_refs, h_ref = rest
    else:
        o_ref, h_ref = rest
    tile = pl.program_id(0) + tile_off
    j = pl.program_id(1)

    for x_ref, (lo, cnt) in zip(x_refs, x_bounds):
        @pl.when((j == 0) & (tile >= lo) & (tile < lo + cnt))
        def _(x_ref=x_ref):
            h_ref[...] = _norm_mod(x_ref[...], g_ref[...], sh_ref[...],
                                   sc_ref[...]).astype(BF16)

    if split:
        lo = 0
        for cnt, out in zip(split, o_refs):
            @pl.when((j >= lo) & (j < lo + cnt))
            def _(out=out):
                out[...] = jnp.dot(h_ref[...], w_ref[...],
                                   preferred_element_type=F32).astype(out.dtype)
            lo += cnt
        return

    if not rope:
        o_ref[...] = jnp.dot(h_ref[...], w_ref[...],
                             preferred_element_type=F32).astype(o_ref.dtype)
        return

    @pl.when(j < qk_tiles)
    def _():
        tn = o_ref.shape[1]
        sub = min(tn, MXU_COLS)
        h = h_ref[...]
        accs = [jnp.dot(h, w_ref[:, c:c + sub], preferred_element_type=F32)
                for c in range(0, tn, sub)]
        scale = jnp.where(j < q_tiles, q_scale, 1.0)
        cos = cos_ref[...] * scale
        sin = sin_ref[...] * scale
        lane = lax.broadcasted_iota(jnp.int32, cos.shape, 1)
        first_of_pair = (lane // (LANES // 4)) % 2 == 0
        for ci, acc in enumerate(accs):
            for s in range(sub // LANES):
                xs = acc[:, s * LANES:(s + 1) * LANES]
                partner = jnp.where(first_of_pair,
                                    pltpu.roll(xs, LANES - LANES // 4, 1),
                                    pltpu.roll(xs, LANES // 4, 1))
                col = ci * sub + s * LANES
                o_ref[:, col:col + LANES] = (xs * cos + partner * sin).astype(o_ref.dtype)

    @pl.when(j >= qk_tiles)
    def _():
        o_ref[...] = jnp.dot(h_ref[...], w_ref[...],
                             preferred_element_type=F32).astype(o_ref.dtype)


def _row_segment_specs(segs, tm, tile_off=0, buffers=None):
    specs, bounds, lo = [], [], 0
    for k, a in enumerate(segs):
        mode = {} if buffers is None or buffers[k] is None else dict(
            pipeline_mode=pl.Buffered(buffers[k]))
        cnt = a.shape[0] // tm
        specs.append(pl.BlockSpec(
            (tm, a.shape[1]),
            lambda i, j, lo=lo, cnt=cnt: (jnp.clip(i + tile_off - lo, 0, cnt - 1), 0), **mode))
        bounds.append((lo, cnt))
        lo += cnt
    return specs, tuple(bounds)


def _project(x_segs, g, mod, shift_chunk, scale_chunk, w, *, row_off, rows, out_dtype,
             p_rows, s_len, tm=512, tn=512, rope_tables=None, q_scale=1.0, sections=None):
    d = x_segs[0].shape[1]
    n = w.shape[1]
    tm = math.gcd(tm, p_rows, s_len)
    tile_off = row_off // tm
    rope = rope_tables is not None
    tn = _tile(n // 3 if rope else math.gcd(n, *(sections or [n])), tn)
    split = tuple(c // tn for c in sections) if sections else ()
    biggest = max(range(len(x_segs)), key=lambda k: x_segs[k].shape[0])
    x_specs, x_bounds = _row_segment_specs(
        x_segs, tm, tile_off,
        buffers=[None if k == biggest else 1 for k in range(len(x_segs))])
    in_specs = x_specs + [
        pl.BlockSpec((1, d), lambda i, j: (0, 0)),
        _mod_spec(shift_chunk, d, tm, p_rows, s_len, tile_off),
        _mod_spec(scale_chunk, d, tm, p_rows, s_len, tile_off),
        pl.BlockSpec((d, tn), lambda i, j: (0, j)),
    ]
    args = list(x_segs) + [g.reshape(1, d), mod, mod, w.astype(BF16)]
    q_tiles = qk_tiles = 0
    if rope:
        cos, sin = rope_tables
        tiles_per_seq = s_len // tm
        in_specs += [pl.BlockSpec((tm, LANES), lambda i, j: (i % tiles_per_seq, 0))] * 2
        args += [cos, sin]
        q_tiles = (n // 3) // tn
        qk_tiles = 2 * q_tiles
    kern = functools.partial(_proj_kernel, x_bounds=x_bounds, tile_off=tile_off, rope=rope,
                             q_tiles=q_tiles, qk_tiles=qk_tiles, q_scale=q_scale, split=split)
    if split:
        starts = [sum(split[:s]) for s in range(len(split))]
        out_specs = [pl.BlockSpec((tm, tn),
                                  lambda i, j, lo=lo, cnt=cnt: (i, jnp.clip(j - lo, 0, cnt - 1)))
                     for lo, cnt in zip(starts, split)]
        dtypes = out_dtype if isinstance(out_dtype, (list, tuple)) else [out_dtype] * len(split)
        out_shape = [jax.ShapeDtypeStruct((rows, c), dt) for c, dt in zip(sections, dtypes)]
    else:
        out_specs = pl.BlockSpec((tm, tn), lambda i, j: (i, j))
        out_shape = jax.ShapeDtypeStruct((rows, n), out_dtype)
    return pl.pallas_call(
        kern,
        grid=(rows // tm, n // tn),
        in_specs=in_specs,
        out_specs=out_specs,
        out_shape=out_shape,
        scratch_shapes=[pltpu.VMEM((tm, d), BF16)],
        compiler_params=_params("arbitrary", "arbitrary"),
        name="norm_mod_project",
    )(*args)


def _hgrn_group_local(groups):
    c, m = HGRN_CHUNK, HGRN_SUB
    nb = c // m
    t = lax.broadcasted_iota(jnp.int32, (c, c), 0)
    s = lax.broadcasted_iota(jnp.int32, (c, c), 1)

    jobs = []
    for q, z, v, lb, direction in groups:
        dk = q.shape[1]
        tri = (s <= t) if direction == 0 else (s >= t)
        cum = jnp.where(tri, 1.0, 0.0).astype(BF16)
        f = lb + (1.0 - lb) * jax.nn.sigmoid(z)
        g2 = jnp.log(f) * LOG2_E
        k = 1.0 - f
        g_hi = g2.astype(BF16)
        g_lo = (g2 - g_hi.astype(F32)).astype(BF16)
        v_bf = v.astype(BF16)
        for ci in range(q.shape[0] // c):
            rows = slice(ci * c, (ci + 1) * c)
            jobs.append(dict(direction=direction, tri=tri, cum=cum, q=q[rows], k=k[rows],
                             v=v_bf[rows], g=jnp.concatenate([g_hi[rows], g_lo[rows]], axis=1)))

    def padded(x, lo, hi):
        parts = [jnp.zeros((lo, dk), BF16)] if lo else []
        parts.append(x)
        if hi < c:
            parts.append(jnp.zeros((c - hi, dk), BF16))
        return jnp.concatenate(parts, axis=0) if len(parts) > 1 else x

    for job in jobs:
        job["sums"] = jnp.dot(job["cum"], job["g"], preferred_element_type=F32)

    for job in jobs:
        direction = job["direction"]
        sums = job["sums"]
        b = sums[:, :dk] + sums[:, dk:]
        anchors = []
        for i in range(nb):
            a_row = i * m + (m // 2 - 1 if direction == 0 else m // 2)
            anchors.append(b[a_row:a_row + 1, :])
        b_anchor = jnp.concatenate([jnp.broadcast_to(r, (m, dk)) for r in anchors], axis=0)
        q_rel = (job["q"] * jnp.exp2(b - b_anchor)).astype(BF16)
        full = nb - 1 if direction == 0 else 0
        q_cat, k_cat = [], []
        k_full = None
        for i in range(nb):
            lo, hi = (0, m * (i + 1)) if direction == 0 else (m * i, c)
            ki = job["k"][lo:hi] * jnp.exp2(anchors[i] - b[lo:hi])
            if i == full:
                k_full = ki
            k_cat.append(padded(ki.astype(BF16), lo, hi))
            q_cat.append(padded(q_rel[m * i:m * (i + 1)], m * i, m * (i + 1)))
        job["a"] = lax.dot_general(jnp.concatenate(q_cat, axis=1),
                                   jnp.concatenate(k_cat, axis=1), NT_DIMS,
                                   preferred_element_type=F32)
        edge = c - 1 if direction == 0 else 0
        total = b[edge:edge + 1, :]
        job["k_dec"] = (k_full * jnp.exp2(total - anchors[full])).astype(BF16)
        job["q_in"] = (job["q"] * jnp.exp2(b)).astype(BF16)
        job["decay"] = jnp.exp2(total)

    for job in jobs:
        job["u_t"] = lax.dot_general(job["v"], job["k_dec"], TN_DIMS,
                                     preferred_element_type=F32)

    out, pos = [], 0
    for q, *_ in groups:
        n = q.shape[0] // c
        out.append([(jnp.where(j["tri"], j["a"], 0.0).astype(BF16), j["v"], j["q_in"], j["u_t"],
                     j["decay"]) for j in jobs[pos:pos + n]])
        pos += n
    return out


def _hgrn_kernel(*refs, seq_len, layer_j, n_lb, has_s0, emit_state):
    q_ref, zf_ref, zb_ref, i_ref, g_ref, lg_ref, on_ref = refs[:7]
    pos = 7
    s0_ref = st_ref = None
    if has_s0:
        s0_ref = refs[pos]
        pos += 1
    o_ref = refs[pos]
    pos += 1
    if emit_state:
        st_ref = refs[pos]
        pos += 1
    of_scr, ob_scr = refs[pos], refs[pos + 1]

    grp = math.gcd(HGRN_GROUP, seq_len // HGRN_CHUNK)
    rows = grp * HGRN_CHUNK
    n = seq_len // rows
    dv = q_ref.shape[1]

    lbs = []
    for d in range(2):
        lg = lg_ref[d * n_lb:(d + 1) * n_lb, :]
        e = jnp.exp(lg - jnp.max(lg, axis=0, keepdims=True))
        lbs.append(jnp.sum(e[:layer_j + 1], axis=0, keepdims=True) / jnp.sum(e, axis=0, keepdims=True))

    if has_s0:
        st0 = (s0_ref[0].T, s0_ref[1].T)
    else:
        st0 = (jnp.zeros((dv, dv), F32), jnp.zeros((dv, dv), F32))

    def advance(local, st):
        _, _, q_in, u_t, decay = local
        o = lax.dot_general(q_in, st.astype(BF16), NT_DIMS, preferred_element_type=F32)
        return st * decay + u_t, o

    def add_intra(local, o_inter):
        a, v = local[:2]
        return o_inter + jnp.dot(a, v, preferred_element_type=F32)

    def body(gi, carry):
        st_f, st_b = carry
        rf = pl.ds(pl.multiple_of(gi * rows, rows), rows)
        rb = pl.ds(pl.multiple_of((n - 1 - gi) * rows, rows), rows)
        loc_f, loc_b = _hgrn_group_local([
            (q_ref[rf, :], zf_ref[rf, :], i_ref[rf, :], lbs[0], 0),
            (q_ref[rb, :], zb_ref[rb, :], i_ref[rb, :], lbs[1], 1)])
        o_f, o_b = [None] * grp, [None] * grp
        for ci in range(grp):
            st_f, o_f[ci] = advance(loc_f[ci], st_f)
            st_b, o_b[grp - 1 - ci] = advance(loc_b[grp - 1 - ci], st_b)
        o_f = [add_intra(l, o) for l, o in zip(loc_f, o_f)]
        o_b = [add_intra(l, o) for l, o in zip(loc_b, o_b)]
        of_scr[rf, :] = jnp.concatenate(o_f, axis=0)
        ob_scr[rb, :] = jnp.concatenate(o_b, axis=0)
        return st_f, st_b

    st_f, st_b = lax.fori_loop(0, n, body, st0)
    if emit_state:
        st_ref[0] = st_f.T
        st_ref[1] = st_b.T

    fin_rows = math.gcd(seq_len, 256)
    onorm = on_ref[...]

    def finish(ri, _):
        r = pl.ds(pl.multiple_of(ri * fin_rows, fin_rows), fin_rows)
        o = of_scr[r, :] + ob_scr[r, :]
        y = o * lax.rsqrt(jnp.mean(o * o, axis=-1, keepdims=True) + EPS) * onorm
        o_ref[r, :] = (y * _silu(g_ref[r, :].astype(F32))).astype(o_ref.dtype)
        return 0

    lax.fori_loop(0, seq_len // fin_rows, finish, 0)


def _hgrn_scan(proj, lb_logits, onorm, s0, *, layer_j, n_seq, seq_len, row_off, heads, emit_state):
    dk = LANES
    d = heads * dk
    blk_off = row_off // seq_len
    n_lb = lb_logits.shape[1]

    def sec(k):
        return pl.BlockSpec((seq_len, dk), lambda b, h: (b + blk_off, k * heads + h))

    in_specs = [sec(0), sec(1), sec(2), sec(3), sec(4),
                pl.BlockSpec((2 * n_lb, dk), lambda b, h: (0, h)),
                pl.BlockSpec((1, dk), lambda b, h: (0, 0))]
    args = [proj] * 5 + [lb_logits.reshape(2 * n_lb, d), onorm.reshape(1, dk)]
    if s0 is not None:
        in_specs.append(pl.BlockSpec((None, None, 2, None, dk, dk),
                                     lambda b, h: (b, layer_j, 0, h, 0, 0)))
        args.append(s0)
    out_shape = [jax.ShapeDtypeStruct((n_seq * seq_len, d), BF16)]
    out_specs = [pl.BlockSpec((seq_len, dk), lambda b, h: (b, h))]
    if emit_state:
        out_shape.append(jax.ShapeDtypeStruct((n_seq, 2, heads, dk, dk), F32))
        out_specs.append(pl.BlockSpec((None, 2, None, dk, dk), lambda b, h: (b, 0, h, 0, 0)))
    kern = functools.partial(_hgrn_kernel, seq_len=seq_len, layer_j=layer_j, n_lb=n_lb,
                             has_s0=s0 is not None, emit_state=emit_state)
    return pl.pallas_call(
        kern,
        grid=(n_seq, heads),
        in_specs=in_specs,
        out_specs=out_specs,
        out_shape=out_shape,
        scratch_shapes=[pltpu.VMEM((seq_len, dk), F32), pltpu.VMEM((seq_len, dk), F32)],
        compiler_params=_params("arbitrary", "arbitrary"),
        name="hgrn_scan",
    )(*args)


def _attn_kernel(*refs, has_cache, lam_init, q_scale):
    q_ref, k_ref, v_ref = refs[:3]
    pos = 3
    kc_ref = vc_ref = None
    if has_cache:
        kc_ref, vc_ref = refs[3], refs[4]
        pos = 5
    lam_ref, sub_ref, o_ref = refs[pos:pos + 3]

    lp = lam_ref[...]
    lam = (jnp.exp(jnp.sum(lp[0:1] * lp[1:2], axis=-1, keepdims=True))
           - jnp.exp(jnp.sum(lp[2:3] * lp[3:4], axis=-1, keepdims=True)) + lam_init)

    hd = LANES
    v_new = v_ref[...].astype(BF16)
    v_old = vc_ref[...].astype(BF16) if has_cache else None
    tq = q_ref.shape[0]
    rb = min(tq, ATTN_ROW_BLOCK)
    chains = [(r, j) for r in range(0, tq, rb) for j in range(2)]
    def score(r, j):
        cols = slice(j * hd, (j + 1) * hd)
        qj = q_ref[r:r + rb, cols]
        if q_scale != 1.0:
            qj = qj.astype(F32) * q_scale
        qj = qj.astype(BF16)
        s_new = lax.dot_general(qj, k_ref[:, cols].astype(BF16), NT_DIMS,
                                preferred_element_type=F32)
        s_old = None
        if has_cache:
            s_old = lax.dot_general(qj, kc_ref[:, cols].astype(BF16), NT_DIMS,
                                    preferred_element_type=F32)
        return s_new, s_old

    ahead = min(ATTN_LOOKAHEAD, len(chains))
    scores = [score(*c) for c in chains[:ahead]]
    outs = {}
    for ci, (r, j) in enumerate(chains):
        if ci + ahead < len(chains):
            scores.append(score(*chains[ci + ahead]))
        s_new, s_old = scores[ci]
        mx = jnp.max(s_new, axis=-1, keepdims=True)
        if has_cache:
            mx = jnp.maximum(mx, jnp.max(s_old, axis=-1, keepdims=True))
        p_new = jnp.exp2(s_new - mx)
        den = jnp.sum(p_new, axis=-1, keepdims=True)
        acc = jnp.dot(p_new.astype(BF16), v_new, preferred_element_type=F32)
        if has_cache:
            p_old = jnp.exp2(s_old - mx)
            den += jnp.sum(p_old, axis=-1, keepdims=True)
            acc += jnp.dot(p_old.astype(BF16), v_old, preferred_element_type=F32)
        outs[r, j] = acc / den
    sub = sub_ref[...]
    for r in range(0, tq, rb):
        o = outs[r, 0] - lam * outs[r, 1]
        y = o * lax.rsqrt(jnp.mean(o * o, axis=-1, keepdims=True) + EPS) * sub
        o_ref[r:r + rb, :] = (y * (1.0 - lam_init)).astype(o_ref.dtype)


def _diff_attention(qkv, cache_k, cache_v, lam_p, subln, *, n_seq, seq_len, heads, lam_init,
                    q_scale, tq, cache_layer=0):
    hd2 = 2 * LANES
    d = heads * hd2
    qb = seq_len // tq
    fused = not isinstance(qkv, (list, tuple))
    k_off, v_off = (heads, 2 * heads) if fused else (0, 0)
    in_specs = [
        pl.BlockSpec((tq, hd2), lambda b, h, i: (b * qb + i, h)),
        pl.BlockSpec((seq_len, hd2), lambda b, h, i: (b, k_off + h)),
        pl.BlockSpec((seq_len, hd2), lambda b, h, i: (b, v_off + h)),
    ]
    args = [qkv, qkv, qkv] if fused else list(qkv)
    if cache_k is not None:
        past = cache_k.shape[2]
        in_specs += [pl.BlockSpec((None, None, past, hd2),
                                  lambda b, h, i: (b, cache_layer, 0, h))] * 2
        args += [cache_k, cache_v]
    in_specs += [pl.BlockSpec(lam_p.shape, lambda b, h, i: (0, 0)),
                 pl.BlockSpec((1, hd2), lambda b, h, i: (0, 0))]
    args += [lam_p, subln.reshape(1, hd2)]
    kern = functools.partial(_attn_kernel, has_cache=cache_k is not None, lam_init=lam_init,
                             q_scale=q_scale)
    return pl.pallas_call(
        kern,
        grid=(n_seq, heads, qb),
        in_specs=in_specs,
        out_specs=pl.BlockSpec((tq, hd2), lambda b, h, i: (b * qb + i, h)),
        out_shape=jax.ShapeDtypeStruct((n_seq * seq_len, d), BF16),
        compiler_params=_params("arbitrary", "arbitrary", "arbitrary"),
        name="diff_attention",
    )(*args)


def _out_proj_kernel(*refs, a_bounds, x_bounds):
    a_refs = refs[:len(a_bounds)]
    x_refs = refs[len(a_bounds):len(a_bounds) + len(x_bounds)]
    w_ref, gate_ref, o_ref = refs[len(a_bounds) + len(x_bounds):]
    i = pl.program_id(0)
    for a_ref, (a_lo, a_cnt) in zip(a_refs, a_bounds):
        for x_ref, (x_lo, x_cnt) in zip(x_refs, x_bounds):
            lo, hi = max(a_lo, x_lo), min(a_lo + a_cnt, x_lo + x_cnt)
            if lo >= hi:
                continue

            @pl.when((i >= lo) & (i < hi))
            def _(a_ref=a_ref, x_ref=x_ref):
                y = jnp.dot(a_ref[...], w_ref[...], preferred_element_type=F32)
                o_ref[...] = x_ref[...] + gate_ref[...] * y


def _out_project(a_segs, w, x_segs, mod, gate_chunk, *, p_rows, s_len, tm=512):
    d = x_segs[0].shape[1]
    t = sum(x.shape[0] for x in x_segs)
    kdim = w.shape[0]
    a_specs, a_bounds = _row_segment_specs(a_segs, tm)
    x_specs, x_bounds = _row_segment_specs(x_segs, tm)
    in_specs = a_specs + x_specs + [
        pl.BlockSpec((kdim, d), lambda i, j: (0, 0)),
        _mod_spec(gate_chunk, d, tm, p_rows, s_len),
    ]
    return pl.pallas_call(
        functools.partial(_out_proj_kernel, a_bounds=a_bounds, x_bounds=x_bounds),
        grid=(t // tm, 1),
        in_specs=in_specs,
        out_specs=pl.BlockSpec((tm, d), lambda i, j: (i, 0)),
        out_shape=jax.ShapeDtypeStruct((t, d), F32),
        compiler_params=_params("arbitrary", "arbitrary"),
        name="out_project_residual",
    )(*a_segs, *x_segs, w, mod)


def _ffn_kernel(x_ref, xp_ref, xn_ref, g_ref, sh_ref, sc_ref, gate_ref, wg_ref, wv_ref,
                conv_a, wd_a, conv_b, wd_b, *rest, tm, nj, p_rows, p_len, s_len, final):
    if final:
        gf_ref, yp_ref, ys_ref, h_ref, act_ref, ug_scr, uv_scr, acc_ref = rest
    else:
        o_ref, h_ref, act_ref, ug_scr, uv_scr, acc_ref = rest
    i = pl.program_id(0)
    j = pl.program_id(1)
    hs = xp_ref.shape[0]
    sub = FFN_TILE
    ext = hs + tm
    row0 = i * tm
    in_prompt = row0 < p_rows
    inner_edges = jnp.where(in_prompt, 1.0, 0.0)
    win = lax.broadcasted_iota(jnp.int32, (2 * SUBLANES, 1), 0)

    def up(cols):
        h = h_ref[...]
        return (jnp.dot(h, wg_ref[:, cols], preferred_element_type=F32),
                jnp.dot(h, wv_ref[:, cols], preferred_element_type=F32))

    def conv(u_all, u_scr, cw, cb):
        u_scr[0:ext, :] = u_all
        u_scr[ext:ext + 1, :] = u_all[hs - 2:hs - 1]
        u = u_all[hs:hs + tm]
        prev = u_scr[hs - 1:hs - 1 + tm, :]
        nxt = u_scr[hs + 1:hs + 1 + tm, :]
        y = prev * cw[0:1] + u * cw[1:2] + nxt * cw[2:3] + cb
        pieces, done = [], 0
        for edge in range(p_len, tm, p_len):
            lo, hi = edge - SUBLANES, edge + SUBLANES
            leak = (jnp.where(win == SUBLANES - 1, nxt[lo:hi] * cw[2:3], 0.0)
                    + jnp.where(win == SUBLANES, prev[lo:hi] * cw[0:1], 0.0))
            pieces += [y[done:lo], y[lo:hi] - inner_edges * leak]
            done = hi
        return jnp.concatenate(pieces + [y[done:]], axis=0) if pieces else y

    def activate(ug, uv, conv_ref):
        p = conv_ref[...]
        gate = conv(ug, ug_scr, p[0:3], p[3:4])
        val = conv(uv, uv_scr, p[4:7], p[7:8])
        return (_silu(gate) * val).astype(BF16)

    def activate_a(ug, uv):
        return activate(ug, uv, conv_a)

    def project_down(act, wd):
        acc_ref[...] += jnp.dot(act, wd[...], preferred_element_type=F32)

    cols_a, cols_b = slice(0, sub), slice(sub, 2 * sub)

    @pl.when(j == 0)
    def _():
        g, sh, sc = g_ref[...], sh_ref[...], sc_ref[...]
        h_ref[hs:hs + tm, :] = _norm_mod(x_ref[...], g, sh, sc).astype(BF16)
        end = row0 + tm
        at_seq_end = jnp.where(in_prompt, lax.rem(end, p_len) == 0,
                               lax.rem(end - p_rows, s_len) == 0)
        at_seq_start = jnp.where(in_prompt, lax.rem(row0, p_len) == 0,
                                 lax.rem(row0 - p_rows, s_len) == 0)
        after = _norm_mod(xn_ref[...], g, sh, sc) * jnp.where(at_seq_end, 0.0, 1.0)
        before = _norm_mod(xp_ref[...], g, sh, sc) * jnp.where(at_seq_start, 0.0, 1.0)
        r = lax.broadcasted_iota(jnp.int32, (hs, 1), 0)
        halo = jnp.where(r == hs - 1, before,
                         jnp.where(r == hs - 2, pltpu.roll(after, hs - 2, 0), 0.0))
        h_ref[0:hs, :] = halo.astype(BF16)
        acc_ref[...] = jnp.zeros_like(acc_ref)
        ua = up(cols_a)
        act_a = activate_a(*ua)
        ub = up(cols_b)
        project_down(act_a, wd_a)
        act_ref[...] = activate(*ub, conv_b)

    @pl.when((j > 0) & (j < nj))
    def _():
        ua = up(cols_a)
        project_down(act_ref[...], wd_b)
        act_a = activate_a(*ua)
        ub = up(cols_b)
        project_down(act_a, wd_a)
        act_ref[...] = activate(*ub, conv_b)

    @pl.when(j == nj)
    def _():
        project_down(act_ref[...], wd_b)
        res = x_ref[...] + gate_ref[...] * acc_ref[...]
        if not final:
            o_ref[...] = res
        else:
            y = res * lax.rsqrt(jnp.mean(res * res, axis=-1, keepdims=True) + EPS) * gf_ref[...]

            @pl.when(in_prompt)
            def _():
                yp_ref[...] = y

            @pl.when(jnp.logical_not(in_prompt))
            def _():
                ys_ref[...] = y


def _ffn(x, g, mod, shift_chunk, scale_chunk, gate_chunk, w_up, cw, cb, w_down, *,
         p_rows, p_len, s_len, tm=512, final_gain=None):
    t, d = x.shape
    ff = w_up.shape[1] // 2
    sub = FFN_TILE
    assert ff % (2 * sub) == 0
    tm = math.gcd(tm, p_rows, s_len)
    assert (tm % p_len == 0 or p_len % tm == 0) and p_rows % tm == 0 and s_len % tm == 0
    nj = ff // (2 * sub)
    nt = ff // sub
    hs = 16
    per = tm // hs
    last_blk = t // hs - 1

    def tile_a(i, j):
        return jnp.minimum(2 * j, nt - 2)

    def tile_b(i, j):
        return jnp.minimum(2 * j + 1, nt - 1)

    def parked_b(i, j):
        return jnp.maximum(2 * j - 1, 1)

    def tile_specs(conv_tile, down_tile):
        return [
            pl.BlockSpec((None, 2 * (taps + 1), sub), lambda i, j: (conv_tile(i, j), 0, 0)),
            pl.BlockSpec((sub, d), lambda i, j: (down_tile(i, j), 0)),
        ]

    taps = cw.shape[0]
    conv_p = jnp.concatenate([cw[:, :ff], cb[None, :ff], cw[:, ff:], cb[None, ff:]], axis=0)
    conv_p = conv_p.reshape(2 * (taps + 1), nt, sub).transpose(1, 0, 2)

    in_specs = [
        pl.BlockSpec((tm, d), lambda i, j: (i, 0)),
        pl.BlockSpec((hs, d), lambda i, j: (jnp.maximum(i * per - 1, 0), 0)),
        pl.BlockSpec((hs, d), lambda i, j: (jnp.minimum((i + 1) * per, last_blk), 0)),
        pl.BlockSpec((1, d), lambda i, j: (0, 0)),
        _mod_spec(shift_chunk, d, tm, p_rows, s_len),
        _mod_spec(scale_chunk, d, tm, p_rows, s_len),
        _mod_spec(gate_chunk, d, tm, p_rows, s_len),
        pl.BlockSpec((None, d, 2 * sub), lambda i, j: (jnp.minimum(j, nj - 1), 0, 0)),
        pl.BlockSpec((None, d, 2 * sub), lambda i, j: (jnp.minimum(j, nj - 1) + nj, 0, 0)),
    ] + tile_specs(tile_a, tile_a) + tile_specs(tile_b, parked_b)
    w_up = _column_tiles(w_up, 2 * sub)
    tile_args = [conv_p, w_down.astype(BF16)]
    final = final_gain is not None
    kern = functools.partial(_ffn_kernel, tm=tm, nj=nj, p_rows=p_rows, p_len=p_len,
                             s_len=s_len, final=final)
    args = [x, x, x, g.reshape(1, d), mod, mod, mod, w_up, w_up, *tile_args, *tile_args]
    if final:
        pt = p_rows // tm
        in_specs.append(pl.BlockSpec((1, d), lambda i, j: (0, 0)))
        args.append(final_gain.reshape(1, d))
        out_specs = [pl.BlockSpec((tm, d), lambda i, j: (jnp.minimum(i, pt - 1), 0)),
                     pl.BlockSpec((tm, d), lambda i, j: (jnp.maximum(i - pt, 0), 0))]
        out_shape = [jax.ShapeDtypeStruct((p_rows, d), F32),
                     jax.ShapeDtypeStruct((t - p_rows, d), F32)]
    else:
        out_specs = pl.BlockSpec((tm, d), lambda i, j: (i, 0))
        out_shape = jax.ShapeDtypeStruct((t, d), F32)
    return pl.pallas_call(
        kern,
        grid=(t // tm, nj + 1),
        in_specs=in_specs,
        out_specs=out_specs,
        out_shape=out_shape,
        scratch_shapes=[pltpu.VMEM((tm + hs, d), BF16),
                        pltpu.VMEM((tm, sub), BF16),
                        pltpu.VMEM((tm + hs + SUBLANES, sub), F32),
                        pltpu.VMEM((tm + hs + SUBLANES, sub), F32),
                        pltpu.VMEM((tm, d), F32)],
        compiler_params=_params("arbitrary", "arbitrary"),
        name="conv_ffn",
    )(*args)


def _rope_tables(seq_len, hd):
    pos = jnp.arange(seq_len)
    row = (pos // GRID_W).astype(F32)
    col = (pos % GRID_W).astype(F32)
    nf = hd // 4
    inv = ROPE_BASE ** (-jnp.arange(nf, dtype=F32) / nf)
    ar = row[:, None] * inv
    ac = col[:, None] * inv
    cos = jnp.concatenate([jnp.cos(ar), jnp.cos(ar), jnp.cos(ac), jnp.cos(ac)], axis=1)
    sin = jnp.concatenate([-jnp.sin(ar), jnp.sin(ar), -jnp.sin(ac), jnp.sin(ac)], axis=1)
    return cos, sin


def kernel(x_prompt, x_sample, c, cache_hgrn_state, cache_attn_k, cache_attn_v, c_ctx, w_mod, b_mod, norm_mix, norm_ffn, w_hgrn_in, hgrn_lb_logits, hgrn_onorm, w_hgrn_out, w_attn_in, attn_lambda, attn_subln, w_attn_out, w_ffn_up, ffn_conv_w, ffn_conv_b, w_ffn_down, norm_final):
    batch, p_len, d = x_prompt.shape
    dec_batch, s_len, _ = x_sample.shape
    depth = w_mod.shape[0]
    p_rows = batch * p_len
    s_rows = dec_batch * s_len
    hgrn_heads = d // LANES
    diff_heads = d // (2 * LANES)
    hd = LANES
    assert 1 + dec_batch <= MOD_ROWS and p_rows % s_len == 0

    x = [x_prompt.reshape(p_rows, d), x_sample.reshape(s_rows, d)]
    cvec = jnp.concatenate(
        [c_ctx[None, :], c, jnp.zeros((MOD_ROWS - 1 - dec_batch, d), F32)], axis=0)
    mods = _modulation(cvec, w_mod, b_mod)
    seg = dict(p_rows=p_rows, s_len=s_len)

    hgrn_states, attn_ks, attn_vs = [], [], []
    for l in range(depth):
        mod = mods[l][:, None, :]
        j = l // N_MIXERS
        if l % N_MIXERS == 0:
            proj = _project(x, norm_mix[l], mod, 0, 1, w_hgrn_in[j], row_off=0,
                            rows=p_rows + s_rows, out_dtype=F32, tm=1024, tn=1024, **seg)
            mix_p, st = _hgrn_scan(proj, hgrn_lb_logits, hgrn_onorm[j], None, layer_j=j,
                                   n_seq=batch, seq_len=p_len, row_off=0, heads=hgrn_heads,
                                   emit_state=True)
            hgrn_states.append(st)
            mix_s, = _hgrn_scan(proj, hgrn_lb_logits, hgrn_onorm[j], cache_hgrn_state,
                                layer_j=j, n_seq=dec_batch, seq_len=s_len, row_off=p_rows,
                                heads=hgrn_heads, emit_state=False)
            w_out = w_hgrn_out[j]
        else:
            lam_init = 0.8 - 0.6 * math.exp(-0.3 * l)
            q_scale = hd ** -0.5 * LOG2_E
            w_in = w_attn_in[j]
            qkv_p = _project(x, norm_mix[l], mod, 0, 1, w_in, row_off=0, rows=p_rows,
                             out_dtype=F32, tn=1024, sections=[d, d, d], **seg)
            attn_ks.append(qkv_p[1].reshape(batch, p_len, 2 * diff_heads, hd))
            attn_vs.append(qkv_p[2].reshape(batch, p_len, diff_heads, 2 * hd))
            mix_p = _diff_attention(qkv_p, None, None, attn_lambda[j], attn_subln[j],
                                    n_seq=batch, seq_len=p_len, heads=diff_heads,
                                    lam_init=lam_init, q_scale=q_scale, tq=p_len)
            qkv_s = _project(x, norm_mix[l], mod, 0, 1, w_in, row_off=p_rows, rows=s_rows,
                             out_dtype=BF16, rope_tables=_rope_tables(s_len, hd),
                             q_scale=q_scale, tm=1024, tn=1024, **seg)
            n_attn, past = cache_attn_k.shape[1:3]
            mix_s = _diff_attention(qkv_s, cache_attn_k.reshape(dec_batch, n_attn, past, d),
                                    cache_attn_v.reshape(dec_batch, n_attn, past, d),
                                    attn_lambda[j], attn_subln[j], n_seq=dec_batch,
                                    seq_len=s_len, heads=diff_heads, lam_init=lam_init,
                                    q_scale=1.0, tq=512, cache_layer=j)
            w_out = w_attn_out[j]
        x = _out_project([mix_p, mix_s], w_out.astype(BF16), x, mod, 2, **seg)
        last = l == depth - 1
        x = _ffn(x, norm_ffn[l], mod, 3, 4, 5, w_ffn_up[l], ffn_conv_w[l], ffn_conv_b[l],
                 w_ffn_down[l], p_len=p_len, final_gain=norm_final if last else None, **seg)
        x = list(x) if last else [x]

    y_prompt = x[0].reshape(batch, p_len, d)
    y_sample = x[1].reshape(dec_batch, s_len, d)
    new_hgrn_state = jnp.stack(hgrn_states, axis=1)
    new_attn_k = jnp.stack(attn_ks, axis=1)
    new_attn_v = jnp.stack(attn_vs, axis=1)
    return (y_prompt, y_sample, new_hgrn_state, new_attn_k, new_attn_v)
```

```python
import functools
import math

import jax
import jax.numpy as jnp
from jax import lax
from jax.experimental import pallas as pl
from jax.experimental.pallas import tpu as pltpu

F32 = jnp.float32
BF16 = jnp.bfloat16

EPS = 1e-6
GRID_W = 64
ROPE_BASE = 10000.0
N_MIXERS = 2

LANES = 128
SUBLANES = 8
MXU_COLS = 256
MOD_ROWS = 16
HGRN_CHUNK = 64
HGRN_SUB = 16
HGRN_GROUP = 16
ATTN_ROW_BLOCK = 128
ATTN_LOOKAHEAD = 1
FFN_TILE = 256
LOG2_E = 1.4426950408889634
VMEM_LIMIT = 56 * 1024 * 1024

NT_DIMS = (((1,), (1,)), ((), ()))
TN_DIMS = (((0,), (0,)), ((), ()))


def _params(*sem):
    return pltpu.CompilerParams(dimension_semantics=sem, vmem_limit_bytes=VMEM_LIMIT)


def _tile(n, want):
    best = LANES
    for cand in range(LANES, min(n, want) + 1, LANES):
        if n % cand == 0:
            best = cand
    assert n % best == 0
    return best


def _column_tiles(w, tn):
    k, n = w.shape
    return w.astype(BF16).reshape(k, n // tn, tn).transpose(1, 0, 2)


def _silu(x):
    return x * jax.nn.sigmoid(x)


def _split_bf16(x):
    hi = x.astype(BF16)
    lo = (x - hi.astype(F32)).astype(BF16)
    return hi, lo


def _mod_kernel(c_ref, w_ref, b_ref, o_ref):
    a_hi, a_lo = _split_bf16(_silu(c_ref[...]))
    w_hi, w_lo = _split_bf16(w_ref[...])
    rows = a_hi.shape[0]
    both = jnp.dot(jnp.concatenate([a_hi, a_lo], axis=0), w_hi, preferred_element_type=F32)
    acc = both[:rows] + both[rows:] + jnp.dot(a_hi, w_lo, preferred_element_type=F32)
    o_ref[...] = acc + b_ref[...]


def _modulation(cvec, w_mod, b_mod, tn=1024):
    depth, d, n = w_mod.shape
    tn = _tile(n, tn)
    return pl.pallas_call(
        _mod_kernel,
        grid=(depth, n // tn),
        in_specs=[
            pl.BlockSpec((MOD_ROWS, d), lambda l, j: (0, 0)),
            pl.BlockSpec((None, d, tn), lambda l, j: (l, 0, j)),
            pl.BlockSpec((None, 1, tn), lambda l, j: (l, 0, j)),
        ],
        out_specs=pl.BlockSpec((None, MOD_ROWS, tn), lambda l, j: (l, 0, j)),
        out_shape=jax.ShapeDtypeStruct((depth, MOD_ROWS, n), F32),
        compiler_params=_params("arbitrary", "arbitrary"),
        name="modulation",
    )(cvec, w_mod, b_mod.reshape(depth, 1, n))


def _norm_mod(x, g, shift, scale):
    y = x * lax.rsqrt(jnp.mean(x * x, axis=-1, keepdims=True) + EPS) * g
    return y * (1.0 + scale) + shift


def _seq_of_row(row0, p_rows, s_len):
    return jnp.where(row0 < p_rows, 0, 1 + (row0 - p_rows) // s_len)


def _mod_spec(chunk, d, tm, p_rows, s_len, tile_off=0):
    return pl.BlockSpec(
        (None, 1, d),
        lambda i, j: (_seq_of_row((i + tile_off) * tm, p_rows, s_len), 0, chunk))


def _proj_kernel(*refs, x_bounds, tile_off, rope, q_tiles, qk_tiles, q_scale, split):
    x_refs = refs[:len(x_bounds)]
    g_ref, sh_ref, sc_ref, w_ref, *rest = refs[len(x_bounds):]
    if rope:
        cos_ref, sin_ref, o_ref, h_ref = rest
    elif split:
        *o_refs, h_ref = rest
    else:
        o_ref, h_ref = rest
    tile = pl.program_id(0) + tile_off
    j = pl.program_id(1)

    for x_ref, (lo, cnt) in zip(x_refs, x_bounds):
        @pl.when((j == 0) & (tile >= lo) & (tile < lo + cnt))
        def _(x_ref=x_ref):
            h_ref[...] = _norm_mod(x_ref[...], g_ref[...], sh_ref[...],
                                   sc_ref[...]).astype(BF16)

    if split:
        lo = 0
        for cnt, out in zip(split, o_refs):
            @pl.when((j >= lo) & (j < lo + cnt))
            def _(out=out):
                out[...] = jnp.dot(h_ref[...], w_ref[...],
                                   preferred_element_type=F32).astype(out.dtype)
            lo += cnt
        return

    if not rope:
        o_ref[...] = jnp.dot(h_ref[...], w_ref[...],
                             preferred_element_type=F32).astype(o_ref.dtype)
        return

    @pl.when(j < qk_tiles)
    def _():
        tn = o_ref.shape[1]
        sub = min(tn, MXU_COLS)
        h = h_ref[...]
        accs = [jnp.dot(h, w_ref[:, c:c + sub], preferred_element_type=F32)
                for c in range(0, tn, sub)]
        scale = jnp.where(j < q_tiles, q_scale, 1.0)
        cos = cos_ref[...] * scale
        sin = sin_ref[...] * scale
        lane = lax.broadcasted_iota(jnp.int32, cos.shape, 1)
        first_of_pair = (lane // (LANES // 4)) % 2 == 0
        for ci, acc in enumerate(accs):
            for s in range(sub // LANES):
                xs = acc[:, s * LANES:(s + 1) * LANES]
                partner = jnp.where(first_of_pair,
                                    pltpu.roll(xs, LANES - LANES // 4, 1),
                                    pltpu.roll(xs, LANES // 4, 1))
                col = ci * sub + s * LANES
                o_ref[:, col:col + LANES] = (xs * cos + partner * sin).astype(o_ref.dtype)

    @pl.when(j >= qk_tiles)
    def _():
        o_ref[...] = jnp.dot(h_ref[...], w_ref[...],
                             preferred_element_type=F32).astype(o_ref.dtype)


def _row_segment_specs(segs, tm, tile_off=0, buffers=None):
    specs, bounds, lo = [], [], 0
    for k, a in enumerate(segs):
        mode = {} if buffers is None or buffers[k] is None else dict(
            pipeline_mode=pl.Buffered(buffers[k]))
        cnt = a.shape[0] // tm
        specs.append(pl.BlockSpec(
            (tm, a.shape[1]),
            lambda i, j, lo=lo, cnt=cnt: (jnp.clip(i + tile_off - lo, 0, cnt - 1), 0), **mode))
        bounds.append((lo, cnt))
        lo += cnt
    return specs, tuple(bounds)


def _project(x_segs, g, mod, shift_chunk, scale_chunk, w, *, row_off, rows, out_dtype,
             p_rows, s_len, tm=512, tn=512, rope_tables=None, q_scale=1.0, sections=None):
    d = x_segs[0].shape[1]
    n = w.shape[1]
    tm = math.gcd(tm, p_rows, s_len)
    tile_off = row_off // tm
    rope = rope_tables is not None
    tn = _tile(n // 3 if rope else math.gcd(n, *(sections or [n])), tn)
    split = tuple(c // tn for c in sections) if sections else ()
    biggest = max(range(len(x_segs)), key=lambda k: x_segs[k].shape[0])
    x_specs, x_bounds = _row_segment_specs(
        x_segs, tm, tile_off,
        buffers=[None if k == biggest else 1 for k in range(len(x_segs))])
    in_specs = x_specs + [
        pl.BlockSpec((1, d), lambda i, j: (0, 0)),
        _mod_spec(shift_chunk, d, tm, p_rows, s_len, tile_off),
        _mod_spec(scale_chunk, d, tm, p_rows, s_len, tile_off),
        pl.BlockSpec((d, tn), lambda i, j: (0, j)),
    ]
    args = list(x_segs) + [g.reshape(1, d), mod, mod, w.astype(BF16)]
    q_tiles = qk_tiles = 0
    if rope:
        cos, sin = rope_tables
        tiles_per_seq = s_len // tm
        in_specs += [pl.BlockSpec((tm, LANES), lambda i, j: (i % tiles_per_seq, 0))] * 2
        args += [cos, sin]
        q_tiles = (n // 3) // tn
        qk_tiles = 2 * q_tiles
    kern = functools.partial(_proj_kernel, x_bounds=x_bounds, tile_off=tile_off, rope=rope,
                             q_tiles=q_tiles, qk_tiles=qk_tiles, q_scale=q_scale, split=split)
    if split:
        starts = [sum(split[:s]) for s in range(len(split))]
        out_specs = [pl.BlockSpec((tm, tn),
                                  lambda i, j, lo=lo, cnt=cnt: (i, jnp.clip(j - lo, 0, cnt - 1)))
                     for lo, cnt in zip(starts, split)]
        dtypes = out_dtype if isinstance(out_dtype, (list, tuple)) else [out_dtype] * len(split)
        out_shape = [jax.ShapeDtypeStruct((rows, c), dt) for c, dt in zip(sections, dtypes)]
    else:
        out_specs = pl.BlockSpec((tm, tn), lambda i, j: (i, j))
        out_shape = jax.ShapeDtypeStruct((rows, n), out_dtype)
    return pl.pallas_call(
        kern,
        grid=(rows // tm, n // tn),
        in_specs=in_specs,
        out_specs=out_specs,
        out_shape=out_shape,
        scratch_shapes=[pltpu.VMEM((tm, d), BF16)],
        compiler_params=_params("arbitrary", "arbitrary"),
        name="norm_mod_project",
    )(*args)


def _hgrn_group_local(groups):
    c, m = HGRN_CHUNK, HGRN_SUB
    nb = c // m
    t = lax.broadcasted_iota(jnp.int32, (c, c), 0)
    s = lax.broadcasted_iota(jnp.int32, (c, c), 1)

    jobs = []
    for q, z, v, lb, direction in groups:
        dk = q.shape[1]
        tri = (s <= t) if direction == 0 else (s >= t)
        cum = jnp.where(tri, 1.0, 0.0).astype(BF16)
        f = lb + (1.0 - lb) * jax.nn.sigmoid(z)
        g2 = jnp.log(f) * LOG2_E
        k = 1.0 - f
        g_hi = g2.astype(BF16)
        g_lo = (g2 - g_hi.astype(F32)).astype(BF16)
        v_bf = v.astype(BF16)
        for ci in range(q.shape[0] // c):
            rows = slice(ci * c, (ci + 1) * c)
            jobs.append(dict(direction=direction, tri=tri, cum=cum, q=q[rows], k=k[rows],
                             v=v_bf[rows], g=jnp.concatenate([g_hi[rows], g_lo[rows]], axis=1)))

    def padded(x, lo, hi):
        parts = [jnp.zeros((lo, dk), BF16)] if lo else []
        parts.append(x)
        if hi < c:
            parts.append(jnp.zeros((c - hi, dk), BF16))
        return jnp.concatenate(parts, axis=0) if len(parts) > 1 else x

    for job in jobs:
        job["sums"] = jnp.dot(job["cum"], job["g"], preferred_element_type=F32)

    for job in jobs:
        direction = job["direction"]
        sums = job["sums"]
        b = sums[:, :dk] + sums[:, dk:]
        anchors = []
        for i in range(nb):
            a_row = i * m + (m // 2 - 1 if direction == 0 else m // 2)
            anchors.append(b[a_row:a_row + 1, :])
        b_anchor = jnp.concatenate([jnp.broadcast_to(r, (m, dk)) for r in anchors], axis=0)
        q_rel = (job["q"] * jnp.exp2(b - b_anchor)).astype(BF16)
        full = nb - 1 if direction == 0 else 0
        q_cat, k_cat = [], []
        k_full = None
        for i in range(nb):
            lo, hi = (0, m * (i + 1)) if direction == 0 else (m * i, c)
            ki = job["k"][lo:hi] * jnp.exp2(anchors[i] - b[lo:hi])
            if i == full:
                k_full = ki
            k_cat.append(padded(ki.astype(BF16), lo, hi))
            q_cat.append(padded(q_rel[m * i:m * (i + 1)], m * i, m * (i + 1)))
        job["a"] = lax.dot_general(jnp.concatenate(q_cat, axis=1),
                                   jnp.concatenate(k_cat, axis=1), NT_DIMS,
                                   preferred_element_type=F32)
        edge = c - 1 if direction == 0 else 0
        total = b[edge:edge + 1, :]
        job["k_dec"] = (k_full * jnp.exp2(total - anchors[full])).astype(BF16)
        job["q_in"] = (job["q"] * jnp.exp2(b)).astype(BF16)
        job["decay"] = jnp.exp2(total)

    for job in jobs:
        job["u_t"] = lax.dot_general(job["v"], job["k_dec"], TN_DIMS,
                                     preferred_element_type=F32)

    out, pos = [], 0
    for q, *_ in groups:
        n = q.shape[0] // c
        out.append([(jnp.where(j["tri"], j["a"], 0.0).astype(BF16), j["v"], j["q_in"], j["u_t"],
                     j["decay"]) for j in jobs[pos:pos + n]])
        pos += n
    return out


def _hgrn_kernel(*refs, seq_len, layer_j, n_lb, has_s0, emit_state):
    q_ref, zf_ref, zb_ref, i_ref, g_ref, lg_ref, on_ref = refs[:7]
    pos = 7
    s0_ref = st_ref = None
    if has_s0:
        s0_ref = refs[pos]
        pos += 1
    o_ref = refs[pos]
    pos += 1
    if emit_state:
        st_ref = refs[pos]
        pos += 1
    of_scr, ob_scr = refs[pos], refs[pos + 1]

    grp = math.gcd(HGRN_GROUP, seq_len // HGRN_CHUNK)
    rows = grp * HGRN_CHUNK
    n = seq_len // rows
    dv = q_ref.shape[1]

    lbs = []
    for d in range(2):
        lg = lg_ref[d * n_lb:(d + 1) * n_lb, :]
        e = jnp.exp(lg - jnp.max(lg, axis=0, keepdims=True))
        lbs.append(jnp.sum(e[:layer_j + 1], axis=0, keepdims=True) / jnp.sum(e, axis=0, keepdims=True))

    if has_s0:
        st0 = (s0_ref[0].T, s0_ref[1].T)
    else:
        st0 = (jnp.zeros((dv, dv), F32), jnp.zeros((dv, dv), F32))

    def advance(local, st):
        _, _, q_in, u_t, decay = local
        o = lax.dot_general(q_in, st.astype(BF16), NT_DIMS, preferred_element_type=F32)
        return st * decay + u_t, o

    def add_intra(local, o_inter):
        a, v = local[:2]
        return o_inter + jnp.dot(a, v, preferred_element_type=F32)

    def body(gi, carry):
        st_f, st_b = carry
        rf = pl.ds(pl.multiple_of(gi * rows, rows), rows)
        rb = pl.ds(pl.multiple_of((n - 1 - gi) * rows, rows), rows)
        loc_f, loc_b = _hgrn_group_local([
            (q_ref[rf, :], zf_ref[rf, :], i_ref[rf, :], lbs[0], 0),
            (q_ref[rb, :], zb_ref[rb, :], i_ref[rb, :], lbs[1], 1)])
        o_f, o_b = [None] * grp, [None] * grp
        for ci in range(grp):
            st_f, o_f[ci] = advance(loc_f[ci], st_f)
            st_b, o_b[grp - 1 - ci] = advance(loc_b[grp - 1 - ci], st_b)
        o_f = [add_intra(l, o) for l, o in zip(loc_f, o_f)]
        o_b = [add_intra(l, o) for l, o in zip(loc_b, o_b)]
        of_scr[rf, :] = jnp.concatenate(o_f, axis=0)
        ob_scr[rb, :] = jnp.concatenate(o_b, axis=0)
        return st_f, st_b

    st_f, st_b = lax.fori_loop(0, n, body, st0)
    if emit_state:
        st_ref[0] = st_f.T
        st_ref[1] = st_b.T

    fin_rows = math.gcd(seq_len, 256)
    onorm = on_ref[...]

    def finish(ri, _):
        r = pl.ds(pl.multiple_of(ri * fin_rows, fin_rows), fin_rows)
        o = of_scr[r, :] + ob_scr[r, :]
        y = o * lax.rsqrt(jnp.mean(o * o, axis=-1, keepdims=True) + EPS) * onorm
        o_ref[r, :] = (y * _silu(g_ref[r, :].astype(F32))).astype(o_ref.dtype)
        return 0

    lax.fori_loop(0, seq_len // fin_rows, finish, 0)


def _hgrn_scan(proj, lb_logits, onorm, s0, *, layer_j, n_seq, seq_len, row_off, heads, emit_state):
    dk = LANES
    d = heads * dk
    blk_off = row_off // seq_len
    n_lb = lb_logits.shape[1]

    def sec(k):
        return pl.BlockSpec((seq_len, dk), lambda b, h: (b + blk_off, k * heads + h))

    in_specs = [sec(0), sec(1), sec(2), sec(3), sec(4),
                pl.BlockSpec((2 * n_lb, dk), lambda b, h: (0, h)),
                pl.BlockSpec((1, dk), lambda b, h: (0, 0))]
    args = [proj] * 5 + [lb_logits.reshape(2 * n_lb, d), onorm.reshape(1, dk)]
    if s0 is not None:
        in_specs.append(pl.BlockSpec((None, None, 2, None, dk, dk),
                                     lambda b, h: (b, layer_j, 0, h, 0, 0)))
        args.append(s0)
    out_shape = [jax.ShapeDtypeStruct((n_seq * seq_len, d), BF16)]
    out_specs = [pl.BlockSpec((seq_len, dk), lambda b, h: (b, h))]
    if emit_state:
        out_shape.append(jax.ShapeDtypeStruct((n_seq, 2, heads, dk, dk), F32))
        out_specs.append(pl.BlockSpec((None, 2, None, dk, dk), lambda b, h: (b, 0, h, 0, 0)))
    kern = functools.partial(_hgrn_kernel, seq_len=seq_len, layer_j=layer_j, n_lb=n_lb,
                             has_s0=s0 is not None, emit_state=emit_state)
    return pl.pallas_call(
        kern,
        grid=(n_seq, heads),
        in_specs=in_specs,
        out_specs=out_specs,
        out_shape=out_shape,
        scratch_shapes=[pltpu.VMEM((seq_len, dk), F32), pltpu.VMEM((seq_len, dk), F32)],
        compiler_params=_params("arbitrary", "arbitrary"),
        name="hgrn_scan",
    )(*args)


def _attn_kernel(*refs, has_cache, lam_init, q_scale):
    q_ref, k_ref, v_ref = refs[:3]
    pos = 3
    kc_ref = vc_ref = None
    if has_cache:
        kc_ref, vc_ref = refs[3], refs[4]
        pos = 5
    lam_ref, sub_ref, o_ref = refs[pos:pos + 3]

    lp = lam_ref[...]
    lam = (jnp.exp(jnp.sum(lp[0:1] * lp[1:2], axis=-1, keepdims=True))
           - jnp.exp(jnp.sum(lp[2:3] * lp[3:4], axis=-1, keepdims=True)) + lam_init)

    hd = LANES
    v_new = v_ref[...].astype(BF16)
    v_old = vc_ref[...].astype(BF16) if has_cache else None
    tq = q_ref.shape[0]
    rb = min(tq, ATTN_ROW_BLOCK)
    chains = [(r, j) for r in range(0, tq, rb) for j in range(2)]
    def score(r, j):
        cols = slice(j * hd, (j + 1) * hd)
        qj = q_ref[r:r + rb, cols]
        if q_scale != 1.0:
            qj = qj.astype(F32) * q_scale
        qj = qj.astype(BF16)
        s_new = lax.dot_general(qj, k_ref[:, cols].astype(BF16), NT_DIMS,
                                preferred_element_type=F32)
        s_old = None
        if has_cache:
            s_old = lax.dot_general(qj, kc_ref[:, cols].astype(BF16), NT_DIMS,
                                    preferred_element_type=F32)
        return s_new, s_old

    ahead = min(ATTN_LOOKAHEAD, len(chains))
    scores = [score(*c) for c in chains[:ahead]]
    outs = {}
    for ci, (r, j) in enumerate(chains):
        if ci + ahead < len(chains):
            scores.append(score(*chains[ci + ahead]))
        s_new, s_old = scores[ci]
        mx = jnp.max(s_new, axis=-1, keepdims=True)
        if has_cache:
            mx = jnp.maximum(mx, jnp.max(s_old, axis=-1, keepdims=True))
        p_new = jnp.exp2(s_new - mx)
        den = jnp.sum(p_new, axis=-1, keepdims=True)
        acc = jnp.dot(p_new.astype(BF16), v_new, preferred_element_type=F32)
        if has_cache:
            p_old = jnp.exp2(s_old - mx)
            den += jnp.sum(p_old, axis=-1, keepdims=True)
            acc += jnp.dot(p_old.astype(BF16), v_old, preferred_element_type=F32)
        outs[r, j] = acc / den
    sub = sub_ref[...]
    for r in range(0, tq, rb):
        o = outs[r, 0] - lam * outs[r, 1]
        y = o * lax.rsqrt(jnp.mean(o * o, axis=-1, keepdims=True) + EPS) * sub
        o_ref[r:r + rb, :] = (y * (1.0 - lam_init)).astype(o_ref.dtype)


def _diff_attention(qkv, cache_k, cache_v, lam_p, subln, *, n_seq, seq_len, heads, lam_init,
                    q_scale, tq, cache_layer=0):
    hd2 = 2 * LANES
    d = heads * hd2
    qb = seq_len // tq
    fused = not isinstance(qkv, (list, tuple))
    k_off, v_off = (heads, 2 * heads) if fused else (0, 0)
    in_specs = [
        pl.BlockSpec((tq, hd2), lambda b, h, i: (b * qb + i, h)),
        pl.BlockSpec((seq_len, hd2), lambda b, h, i: (b, k_off + h)),
        pl.BlockSpec((seq_len, hd2), lambda b, h, i: (b, v_off + h)),
    ]
    args = [qkv, qkv, qkv] if fused else list(qkv)
    if cache_k is not None:
        past = cache_k.shape[2]
        in_specs += [pl.BlockSpec((None, None, past, hd2),
                                  lambda b, h, i: (b, cache_layer, 0, h))] * 2
        args += [cache_k, cache_v]
    in_specs += [pl.BlockSpec(lam_p.shape, lambda b, h, i: (0, 0)),
                 pl.BlockSpec((1, hd2), lambda b, h, i: (0, 0))]
    args += [lam_p, subln.reshape(1, hd2)]
    kern = functools.partial(_attn_kernel, has_cache=cache_k is not None, lam_init=lam_init,
                             q_scale=q_scale)
    return pl.pallas_call(
        kern,
        grid=(n_seq, heads, qb),
        in_specs=in_specs,
        out_specs=pl.BlockSpec((tq, hd2), lambda b, h, i: (b * qb + i, h)),
        out_shape=jax.ShapeDtypeStruct((n_seq * seq_len, d), BF16),
        compiler_params=_params("arbitrary", "arbitrary", "arbitrary"),
        name="diff_attention",
    )(*args)


def _out_proj_kernel(*refs, a_bounds, x_bounds):
    a_refs = refs[:len(a_bounds)]
    x_refs = refs[len(a_bounds):len(a_bounds) + len(x_bounds)]
    w_ref, gate_ref, o_ref = refs[len(a_bounds) + len(x_bounds):]
    i = pl.program_id(0)
    for a_ref, (a_lo, a_cnt) in zip(a_refs, a_bounds):
        for x_ref, (x_lo, x_cnt) in zip(x_refs, x_bounds):
            lo, hi = max(a_lo, x_lo), min(a_lo + a_cnt, x_lo + x_cnt)
            if lo >= hi:
                continue

            @pl.when((i >= lo) & (i < hi))
            def _(a_ref=a_ref, x_ref=x_ref):
                y = jnp.dot(a_ref[...], w_ref[...], preferred_element_type=F32)
                o_ref[...] = x_ref[...] + gate_ref[...] * y


def _out_project(a_segs, w, x_segs, mod, gate_chunk, *, p_rows, s_len, tm=512):
    d = x_segs[0].shape[1]
    t = sum(x.shape[0] for x in x_segs)
    kdim = w.shape[0]
    a_specs, a_bounds = _row_segment_specs(a_segs, tm)
    x_specs, x_bounds = _row_segment_specs(x_segs, tm)
    in_specs = a_specs + x_specs + [
        pl.BlockSpec((kdim, d), lambda i, j: (0, 0)),
        _mod_spec(gate_chunk, d, tm, p_rows, s_len),
    ]
    return pl.pallas_call(
        functools.partial(_out_proj_kernel, a_bounds=a_bounds, x_bounds=x_bounds),
        grid=(t // tm, 1),
        in_specs=in_specs,
        out_specs=pl.BlockSpec((tm, d), lambda i, j: (i, 0)),
        out_shape=jax.ShapeDtypeStruct((t, d), F32),
        compiler_params=_params("arbitrary", "arbitrary"),
        name="out_project_residual",
    )(*a_segs, *x_segs, w, mod)


def _ffn_kernel(x_ref, xp_ref, xn_ref, g_ref, sh_ref, sc_ref, gate_ref, wg_ref, wv_ref,
                conv_a, wd_a, conv_b, wd_b, *rest, tm, nj, p_rows, p_len, s_len, final):
    if final:
        gf_ref, yp_ref, ys_ref, h_ref, act_ref, ug_scr, uv_scr, acc_ref = rest
    else:
        o_ref, h_ref, act_ref, ug_scr, uv_scr, acc_ref = rest
    i = pl.program_id(0)
    j = pl.program_id(1)
    hs = xp_ref.shape[0]
    sub = FFN_TILE
    ext = hs + tm
    row0 = i * tm
    in_prompt = row0 < p_rows
    inner_edges = jnp.where(in_prompt, 1.0, 0.0)
    win = lax.broadcasted_iota(jnp.int32, (2 * SUBLANES, 1), 0)

    def up(cols):
        h = h_ref[...]
        return (jnp.dot(h, wg_ref[:, cols], preferred_element_type=F32),
                jnp.dot(h, wv_ref[:, cols], preferred_element_type=F32))

    def conv(u_all, u_scr, cw, cb):
        u_scr[0:ext, :] = u_all
        u_scr[ext:ext + 1, :] = u_all[hs - 2:hs - 1]
        u = u_all[hs:hs + tm]
        prev = u_scr[hs - 1:hs - 1 + tm, :]
        nxt = u_scr[hs + 1:hs + 1 + tm, :]
        y = prev * cw[0:1] + u * cw[1:2] + nxt * cw[2:3] + cb
        pieces, done = [], 0
        for edge in range(p_len, tm, p_len):
            lo, hi = edge - SUBLANES, edge + SUBLANES
            leak = (jnp.where(win == SUBLANES - 1, nxt[lo:hi] * cw[2:3], 0.0)
                    + jnp.where(win == SUBLANES, prev[lo:hi] * cw[0:1], 0.0))
            pieces += [y[done:lo], y[lo:hi] - inner_edges * leak]
            done = hi
        return jnp.concatenate(pieces + [y[done:]], axis=0) if pieces else y

    def activate(ug, uv, conv_ref):
        p = conv_ref[...]
        gate = conv(ug, ug_scr, p[0:3], p[3:4])
        val = conv(uv, uv_scr, p[4:7], p[7:8])
        return (_silu(gate) * val).astype(BF16)

    def activate_a(ug, uv):
        return activate(ug, uv, conv_a)

    def project_down(act, wd):
        acc_ref[...] += jnp.dot(act, wd[...], preferred_element_type=F32)

    cols_a, cols_b = slice(0, sub), slice(sub, 2 * sub)

    @pl.when(j == 0)
    def _():
        g, sh, sc = g_ref[...], sh_ref[...], sc_ref[...]
        h_ref[hs:hs + tm, :] = _norm_mod(x_ref[...], g, sh, sc).astype(BF16)
        end = row0 + tm
        at_seq_end = jnp.where(in_prompt, lax.rem(end, p_len) == 0,
                               lax.rem(end - p_rows, s_len) == 0)
        at_seq_start = jnp.where(in_prompt, lax.rem(row0, p_len) == 0,
                                 lax.rem(row0 - p_rows, s_len) == 0)
        after = _norm_mod(xn_ref[...], g, sh, sc) * jnp.where(at_seq_end, 0.0, 1.0)
        before = _norm_mod(xp_ref[...], g, sh, sc) * jnp.where(at_seq_start, 0.0, 1.0)
        r = lax.broadcasted_iota(jnp.int32, (hs, 1), 0)
        halo = jnp.where(r == hs - 1, before,
                         jnp.where(r == hs - 2, pltpu.roll(after, hs - 2, 0), 0.0))
        h_ref[0:hs, :] = halo.astype(BF16)
        acc_ref[...] = jnp.zeros_like(acc_ref)
        ua = up(cols_a)
        act_a = activate_a(*ua)
        ub = up(cols_b)
        project_down(act_a, wd_a)
        act_ref[0] = activate(*ub, conv_b)

    for parity in range(2):
        @pl.when((j > 0) & (j < nj) & (j % 2 == parity))
        def _(parity=parity):
            ua = up(cols_a)
            act_a = activate_a(*ua)
            ub = up(cols_b)
            acc_ref[...] += (
                jnp.dot(act_ref[1 - parity], wd_b[...], preferred_element_type=F32)
                + jnp.dot(act_a, wd_a[...], preferred_element_type=F32))
            act_ref[parity] = activate(*ub, conv_b)

    @pl.when(j == nj)
    def _():
        project_down(act_ref[(nj - 1) % 2], wd_b)
        res = x_ref[...] + gate_ref[...] * acc_ref[...]
        if not final:
            o_ref[...] = res
        else:
            y = res * lax.rsqrt(jnp.mean(res * res, axis=-1, keepdims=True) + EPS) * gf_ref[...]

            @pl.when(in_prompt)
            def _():
                yp_ref[...] = y

            @pl.when(jnp.logical_not(in_prompt))
            def _():
                ys_ref[...] = y


def _ffn(x, g, mod, shift_chunk, scale_chunk, gate_chunk, w_up, cw, cb, w_down, *,
         p_rows, p_len, s_len, tm=512, final_gain=None):
    t, d = x.shape
    ff = w_up.shape[1] // 2
    sub = FFN_TILE
    assert ff % (2 * sub) == 0
    tm = math.gcd(tm, p_rows, s_len)
    assert (tm % p_len == 0 or p_len % tm == 0) and p_rows % tm == 0 and s_len % tm == 0
    nj = ff // (2 * sub)
    nt = ff // sub
    hs = 16
    per = tm // hs
    last_blk = t // hs - 1

    def tile_a(i, j):
        return jnp.minimum(2 * j, nt - 2)

    def tile_b(i, j):
        return jnp.minimum(2 * j + 1, nt - 1)

    def parked_b(i, j):
        return jnp.maximum(2 * j - 1, 1)

    def tile_specs(conv_tile, down_tile):
        return [
            pl.BlockSpec((None, 2 * (taps + 1), sub), lambda i, j: (conv_tile(i, j), 0, 0)),
            pl.BlockSpec((sub, d), lambda i, j: (down_tile(i, j), 0)),
        ]

    taps = cw.shape[0]
    conv_p = jnp.concatenate([cw[:, :ff], cb[None, :ff], cw[:, ff:], cb[None, ff:]], axis=0)
    conv_p = conv_p.reshape(2 * (taps + 1), nt, sub).transpose(1, 0, 2)

    in_specs = [
        pl.BlockSpec((tm, d), lambda i, j: (i, 0)),
        pl.BlockSpec((hs, d), lambda i, j: (jnp.maximum(i * per - 1, 0), 0)),
        pl.BlockSpec((hs, d), lambda i, j: (jnp.minimum((i + 1) * per, last_blk), 0)),
        pl.BlockSpec((1, d), lambda i, j: (0, 0)),
        _mod_spec(shift_chunk, d, tm, p_rows, s_len),
        _mod_spec(scale_chunk, d, tm, p_rows, s_len),
        _mod_spec(gate_chunk, d, tm, p_rows, s_len),
        pl.BlockSpec((None, d, 2 * sub), lambda i, j: (jnp.minimum(j, nj - 1), 0, 0)),
        pl.BlockSpec((None, d, 2 * sub), lambda i, j: (jnp.minimum(j, nj - 1) + nj, 0, 0)),
    ] + tile_specs(tile_a, tile_a) + tile_specs(tile_b, parked_b)
    w_up = _column_tiles(w_up, 2 * sub)
    tile_args = [conv_p, w_down.astype(BF16)]
    final = final_gain is not None
    kern = functools.partial(_ffn_kernel, tm=tm, nj=nj, p_rows=p_rows, p_len=p_len,
                             s_len=s_len, final=final)
    args = [x, x, x, g.reshape(1, d), mod, mod, mod, w_up, w_up, *tile_args, *tile_args]
    if final:
        pt = p_rows // tm
        in_specs.append(pl.BlockSpec((1, d), lambda i, j: (0, 0)))
        args.append(final_gain.reshape(1, d))
        out_specs = [pl.BlockSpec((tm, d), lambda i, j: (jnp.minimum(i, pt - 1), 0)),
                     pl.BlockSpec((tm, d), lambda i, j: (jnp.maximum(i - pt, 0), 0))]
        out_shape = [jax.ShapeDtypeStruct((p_rows, d), F32),
                     jax.ShapeDtypeStruct((t - p_rows, d), F32)]
    else:
        out_specs = pl.BlockSpec((tm, d), lambda i, j: (i, 0))
        out_shape = jax.ShapeDtypeStruct((t, d), F32)
    return pl.pallas_call(
        kern,
        grid=(t // tm, nj + 1),
        in_specs=in_specs,
        out_specs=out_specs,
        out_shape=out_shape,
        scratch_shapes=[pltpu.VMEM((tm + hs, d), BF16),
                        pltpu.VMEM((2, tm, sub), BF16),
                        pltpu.VMEM((tm + hs + SUBLANES, sub), F32),
                        pltpu.VMEM((tm + hs + SUBLANES, sub), F32),
                        pltpu.VMEM((tm, d), F32)],
        compiler_params=_params("arbitrary", "arbitrary"),
        name="conv_ffn",
    )(*args)


def _rope_tables(seq_len, hd):
    pos = jnp.arange(seq_len)
    row = (pos // GRID_W).astype(F32)
    col = (pos % GRID_W).astype(F32)
    nf = hd // 4
    inv = ROPE_BASE ** (-jnp.arange(nf, dtype=F32) / nf)
    ar = row[:, None] * inv
    ac = col[:, None] * inv
    cos = jnp.concatenate([jnp.cos(ar), jnp.cos(ar), jnp.cos(ac), jnp.cos(ac)], axis=1)
    sin = jnp.concatenate([-jnp.sin(ar), jnp.sin(ar), -jnp.sin(ac), jnp.sin(ac)], axis=1)
    return cos, sin


def kernel(x_prompt, x_sample, c, cache_hgrn_state, cache_attn_k, cache_attn_v, c_ctx, w_mod, b_mod, norm_mix, norm_ffn, w_hgrn_in, hgrn_lb_logits, hgrn_onorm, w_hgrn_out, w_attn_in, attn_lambda, attn_subln, w_attn_out, w_ffn_up, ffn_conv_w, ffn_conv_b, w_ffn_down, norm_final):
    batch, p_len, d = x_prompt.shape
    dec_batch, s_len, _ = x_sample.shape
    depth = w_mod.shape[0]
    p_rows = batch * p_len
    s_rows = dec_batch * s_len
    hgrn_heads = d // LANES
    diff_heads = d // (2 * LANES)
    hd = LANES
    assert 1 + dec_batch <= MOD_ROWS and p_rows % s_len == 0

    x = [x_prompt.reshape(p_rows, d), x_sample.reshape(s_rows, d)]
    cvec = jnp.concatenate(
        [c_ctx[None, :], c, jnp.zeros((MOD_ROWS - 1 - dec_batch, d), F32)], axis=0)
    mods = _modulation(cvec, w_mod, b_mod)
    seg = dict(p_rows=p_rows, s_len=s_len)

    hgrn_states, attn_ks, attn_vs = [], [], []
    for l in range(depth):
        mod = mods[l][:, None, :]
        j = l // N_MIXERS
        if l % N_MIXERS == 0:
            proj = _project(x, norm_mix[l], mod, 0, 1, w_hgrn_in[j], row_off=0,
                            rows=p_rows + s_rows, out_dtype=F32, tm=1024, tn=1024, **seg)
            mix_p, st = _hgrn_scan(proj, hgrn_lb_logits, hgrn_onorm[j], None, layer_j=j,
                                   n_seq=batch, seq_len=p_len, row_off=0, heads=hgrn_heads,
                                   emit_state=True)
            hgrn_states.append(st)
            mix_s, = _hgrn_scan(proj, hgrn_lb_logits, hgrn_onorm[j], cache_hgrn_state,
                                layer_j=j, n_seq=dec_batch, seq_len=s_len, row_off=p_rows,
                                heads=hgrn_heads, emit_state=False)
            w_out = w_hgrn_out[j]
        else:
            lam_init = 0.8 - 0.6 * math.exp(-0.3 * l)
            q_scale = hd ** -0.5 * LOG2_E
            w_in = w_attn_in[j]
            qkv_p = _project(x, norm_mix[l], mod, 0, 1, w_in, row_off=0, rows=p_rows,
                             out_dtype=F32, tn=1024, sections=[d, d, d], **seg)
            attn_ks.append(qkv_p[1].reshape(batch, p_len, 2 * diff_heads, hd))
            attn_vs.append(qkv_p[2].reshape(batch, p_len, diff_heads, 2 * hd))
            mix_p = _diff_attention(qkv_p, None, None, attn_lambda[j], attn_subln[j],
                                    n_seq=batch, seq_len=p_len, heads=diff_heads,
                                    lam_init=lam_init, q_scale=q_scale, tq=p_len)
            qkv_s = _project(x, norm_mix[l], mod, 0, 1, w_in, row_off=p_rows, rows=s_rows,
                             out_dtype=BF16, rope_tables=_rope_tables(s_len, hd),
                             q_scale=q_scale, tm=1024, tn=1024, **seg)
            n_attn, past = cache_attn_k.shape[1:3]
            mix_s = _diff_attention(qkv_s, cache_attn_k.reshape(dec_batch, n_attn, past, d),
                                    cache_attn_v.reshape(dec_batch, n_attn, past, d),
                                    attn_lambda[j], attn_subln[j], n_seq=dec_batch,
                                    seq_len=s_len, heads=diff_heads, lam_init=lam_init,
                                    q_scale=1.0, tq=512, cache_layer=j)
            w_out = w_attn_out[j]
        x = _out_project([mix_p, mix_s], w_out.astype(BF16), x, mod, 2, **seg)
        last = l == depth - 1
        x = _ffn(x, norm_ffn[l], mod, 3, 4, 5, w_ffn_up[l], ffn_conv_w[l], ffn_conv_b[l],
                 w_ffn_down[l], p_len=p_len, final_gain=norm_final if last else None, **seg)
        x = list(x) if last else [x]

    y_prompt = x[0].reshape(batch, p_len, d)
    y_sample = x[1].reshape(dec_batch, s_len, d)
    new_hgrn_state = jnp.stack(hgrn_states, axis=1)
    new_attn_k = jnp.stack(attn_ks, axis=1)
    new_attn_v = jnp.stack(attn_vs, axis=1)
    return (y_prompt, y_sample, new_hgrn_state, new_attn_k, new_attn_v)
```

```python
import functools
import math

import jax
import jax.numpy as jnp
from jax import lax
from jax.experimental import pallas as pl
from jax.experimental.pallas import tpu as pltpu

F32 = jnp.float32
BF16 = jnp.bfloat16

EPS = 1e-6
GRID_W = 64
ROPE_BASE = 10000.0
N_MIXERS = 2

LANES = 128
SUBLANES = 8
MXU_COLS = 256
MOD_ROWS = 16
HGRN_CHUNK = 64
HGRN_SUB = 16
HGRN_GROUP = 16
ATTN_ROW_BLOCK = 128
ATTN_LOOKAHEAD = 1
FFN_TILE = 256
LOG2_E = 1.4426950408889634
VMEM_LIMIT = 56 * 1024 * 1024

NT_DIMS = (((1,), (1,)), ((), ()))
TN_DIMS = (((0,), (0,)), ((), ()))


def _params(*sem):
    return pltpu.CompilerParams(dimension_semantics=sem, vmem_limit_bytes=VMEM_LIMIT)


def _tile(n, want):
    best = LANES
    for cand in range(LANES, min(n, want) + 1, LANES):
        if n % cand == 0:
            best = cand
    assert n % best == 0
    return best


def _column_tiles(w, tn):
    k, n = w.shape
    return w.astype(BF16).reshape(k, n // tn, tn).transpose(1, 0, 2)


def _silu(x):
    return x * jax.nn.sigmoid(x)


def _split_bf16(x):
    hi = x.astype(BF16)
    lo = (x - hi.astype(F32)).astype(BF16)
    return hi, lo


def _mod_kernel(c_ref, w_ref, b_ref, o_ref):
    a_hi, a_lo = _split_bf16(_silu(c_ref[...]))
    w_hi, w_lo = _split_bf16(w_ref[...])
    rows = a_hi.shape[0]
    both = jnp.dot(jnp.concatenate([a_hi, a_lo], axis=0), w_hi, preferred_element_type=F32)
    acc = both[:rows] + both[rows:] + jnp.dot(a_hi, w_lo, preferred_element_type=F32)
    o_ref[...] = acc + b_ref[...]


def _modulation(cvec, w_mod, b_mod, tn=1024):
    depth, d, n = w_mod.shape
    tn = _tile(n, tn)
    return pl.pallas_call(
        _mod_kernel,
        grid=(depth, n // tn),
        in_specs=[
            pl.BlockSpec((MOD_ROWS, d), lambda l, j: (0, 0)),
            pl.BlockSpec((None, d, tn), lambda l, j: (l, 0, j)),
            pl.BlockSpec((None, 1, tn), lambda l, j: (l, 0, j)),
        ],
        out_specs=pl.BlockSpec((None, MOD_ROWS, tn), lambda l, j: (l, 0, j)),
        out_shape=jax.ShapeDtypeStruct((depth, MOD_ROWS, n), F32),
        compiler_params=_params("arbitrary", "arbitrary"),
        name="modulation",
    )(cvec, w_mod, b_mod.reshape(depth, 1, n))


def _norm_mod(x, g, shift, scale):
    y = x * lax.rsqrt(jnp.mean(x * x, axis=-1, keepdims=True) + EPS) * g
    return y * (1.0 + scale) + shift


def _seq_of_row(row0, p_rows, s_len):
    return jnp.where(row0 < p_rows, 0, 1 + (row0 - p_rows) // s_len)


def _mod_spec(chunk, d, tm, p_rows, s_len, tile_off=0):
    return pl.BlockSpec(
        (None, 1, d),
        lambda i, j: (_seq_of_row((i + tile_off) * tm, p_rows, s_len), 0, chunk))


def _proj_kernel(*refs, x_bounds, tile_off, rope, q_tiles, qk_tiles, q_scale, split):
    x_refs = refs[:len(x_bounds)]
    g_ref, sh_ref, sc_ref, w_ref, *rest = refs[len(x_bounds):]
    if rope:
        cos_ref, sin_ref, o_ref, h_ref = rest
    elif split:
        *o_refs, h_ref = rest
    else:
        o_ref, h_ref = rest
    tile = pl.program_id(0) + tile_off
    j = pl.program_id(1)

    for x_ref, (lo, cnt) in zip(x_refs, x_bounds):
        @pl.when((j == 0) & (tile >= lo) & (tile < lo + cnt))
        def _(x_ref=x_ref):
            h_ref[...] = _norm_mod(x_ref[...], g_ref[...], sh_ref[...],
                                   sc_ref[...]).astype(BF16)

    if split:
        lo = 0
        for cnt, out in zip(split, o_refs):
            @pl.when((j >= lo) & (j < lo + cnt))
            def _(out=out):
                out[...] = jnp.dot(h_ref[...], w_ref[...],
                                   preferred_element_type=F32).astype(out.dtype)
            lo += cnt
        return

    if not rope:
        o_ref[...] = jnp.dot(h_ref[...], w_ref[...],
                             preferred_element_type=F32).astype(o_ref.dtype)
        return

    @pl.when(j < qk_tiles)
    def _():
        tn = o_ref.shape[1]
        sub = min(tn, MXU_COLS)
        h = h_ref[...]
        accs = [jnp.dot(h, w_ref[:, c:c + sub], preferred_element_type=F32)
                for c in range(0, tn, sub)]
        scale = jnp.where(j < q_tiles, q_scale, 1.0)
        cos = cos_ref[...] * scale
        sin = sin_ref[...] * scale
        lane = lax.broadcasted_iota(jnp.int32, cos.shape, 1)
        first_of_pair = (lane // (LANES // 4)) % 2 == 0
        for ci, acc in enumerate(accs):
            for s in range(sub // LANES):
                xs = acc[:, s * LANES:(s + 1) * LANES]
                partner = jnp.where(first_of_pair,
                                    pltpu.roll(xs, LANES - LANES // 4, 1),
                                    pltpu.roll(xs, LANES // 4, 1))
                col = ci * sub + s * LANES
                o_ref[:, col:col + LANES] = (xs * cos + partner * sin).astype(o_ref.dtype)

    @pl.when(j >= qk_tiles)
    def _():
        o_ref[...] = jnp.dot(h_ref[...], w_ref[...],
                             preferred_element_type=F32).astype(o_ref.dtype)


def _row_segment_specs(segs, tm, tile_off=0, buffers=None):
    specs, bounds, lo = [], [], 0
    for k, a in enumerate(segs):
        mode = {} if buffers is None or buffers[k] is None else dict(
            pipeline_mode=pl.Buffered(buffers[k]))
        cnt = a.shape[0] // tm
        specs.append(pl.BlockSpec(
            (tm, a.shape[1]),
            lambda i, j, lo=lo, cnt=cnt: (jnp.clip(i + tile_off - lo, 0, cnt - 1), 0), **mode))
        bounds.append((lo, cnt))
        lo += cnt
    return specs, tuple(bounds)


def _project(x_segs, g, mod, shift_chunk, scale_chunk, w, *, row_off, rows, out_dtype,
             p_rows, s_len, tm=512, tn=512, rope_tables=None, q_scale=1.0, sections=None):
    d = x_segs[0].shape[1]
    n = w.shape[1]
    tm = math.gcd(tm, p_rows, s_len)
    tile_off = row_off // tm
    rope = rope_tables is not None
    tn = _tile(n // 3 if rope else math.gcd(n, *(sections or [n])), tn)
    split = tuple(c // tn for c in sections) if sections else ()
    biggest = max(range(len(x_segs)), key=lambda k: x_segs[k].shape[0])
    x_specs, x_bounds = _row_segment_specs(
        x_segs, tm, tile_off,
        buffers=[None if k == biggest else 1 for k in range(len(x_segs))])
    in_specs = x_specs + [
        pl.BlockSpec((1, d), lambda i, j: (0, 0)),
        _mod_spec(shift_chunk, d, tm, p_rows, s_len, tile_off),
        _mod_spec(scale_chunk, d, tm, p_rows, s_len, tile_off),
        pl.BlockSpec((d, tn), lambda i, j: (0, j)),
    ]
    args = list(x_segs) + [g.reshape(1, d), mod, mod, w.astype(BF16)]
    q_tiles = qk_tiles = 0
    if rope:
        cos, sin = rope_tables
        tiles_per_seq = s_len // tm
        in_specs += [pl.BlockSpec((tm, LANES), lambda i, j: (i % tiles_per_seq, 0))] * 2
        args += [cos, sin]
        q_tiles = (n // 3) // tn
        qk_tiles = 2 * q_tiles
    kern = functools.partial(_proj_kernel, x_bounds=x_bounds, tile_off=tile_off, rope=rope,
                             q_tiles=q_tiles, qk_tiles=qk_tiles, q_scale=q_scale, split=split)
    if split:
        starts = [sum(split[:s]) for s in range(len(split))]
        out_specs = [pl.BlockSpec((tm, tn),
                                  lambda i, j, lo=lo, cnt=cnt: (i, jnp.clip(j - lo, 0, cnt - 1)))
                     for lo, cnt in zip(starts, split)]
        dtypes = out_dtype if isinstance(out_dtype, (list, tuple)) else [out_dtype] * len(split)
        out_shape = [jax.ShapeDtypeStruct((rows, c), dt) for c, dt in zip(sections, dtypes)]
    else:
        out_specs = pl.BlockSpec((tm, tn), lambda i, j: (i, j))
        out_shape = jax.ShapeDtypeStruct((rows, n), out_dtype)
    return pl.pallas_call(
        kern,
        grid=(rows // tm, n // tn),
        in_specs=in_specs,
        out_specs=out_specs,
        out_shape=out_shape,
        scratch_shapes=[pltpu.VMEM((tm, d), BF16)],
        compiler_params=_params("arbitrary", "arbitrary"),
        name="norm_mod_project",
    )(*args)


def _hgrn_group_local(groups):
    c, m = HGRN_CHUNK, HGRN_SUB
    nb = c // m
    t = lax.broadcasted_iota(jnp.int32, (c, c), 0)
    s = lax.broadcasted_iota(jnp.int32, (c, c), 1)

    jobs = []
    for q, z, v, lb, direction in groups:
        dk = q.shape[1]
        tri = (s <= t) if direction == 0 else (s >= t)
        cum = jnp.where(tri, 1.0, 0.0).astype(BF16)
        f = lb + (1.0 - lb) * jax.nn.sigmoid(z)
        g2 = jnp.log(f) * LOG2_E
        k = 1.0 - f
        g_hi = g2.astype(BF16)
        g_lo = (g2 - g_hi.astype(F32)).astype(BF16)
        v_bf = v.astype(BF16)
        for ci in range(q.shape[0] // c):
            rows = slice(ci * c, (ci + 1) * c)
            jobs.append(dict(direction=direction, tri=tri, cum=cum, q=q[rows], k=k[rows],
                             v=v_bf[rows], g=jnp.concatenate([g_hi[rows], g_lo[rows]], axis=1)))

    def padded(x, lo, hi):
        parts = [jnp.zeros((lo, dk), BF16)] if lo else []
        parts.append(x)
        if hi < c:
            parts.append(jnp.zeros((c - hi, dk), BF16))
        return jnp.concatenate(parts, axis=0) if len(parts) > 1 else x

    for job in jobs:
        job["sums"] = jnp.dot(job["cum"], job["g"], preferred_element_type=F32)

    for job in jobs:
        direction = job["direction"]
        sums = job["sums"]
        b = sums[:, :dk] + sums[:, dk:]
        anchors = []
        for i in range(nb):
            a_row = i * m + (m // 2 - 1 if direction == 0 else m // 2)
            anchors.append(b[a_row:a_row + 1, :])
        b_anchor = jnp.concatenate([jnp.broadcast_to(r, (m, dk)) for r in anchors], axis=0)
        q_rel = (job["q"] * jnp.exp2(b - b_anchor)).astype(BF16)
        full = nb - 1 if direction == 0 else 0
        q_cat, k_cat = [], []
        k_full = None
        for i in range(nb):
            lo, hi = (0, m * (i + 1)) if direction == 0 else (m * i, c)
            ki = job["k"][lo:hi] * jnp.exp2(anchors[i] - b[lo:hi])
            if i == full:
                k_full = ki
            k_cat.append(padded(ki.astype(BF16), lo, hi))
            q_cat.append(padded(q_rel[m * i:m * (i + 1)], m * i, m * (i + 1)))
        job["a"] = lax.dot_general(jnp.concatenate(q_cat, axis=1),
                                   jnp.concatenate(k_cat, axis=1), NT_DIMS,
                                   preferred_element_type=F32)
        edge = c - 1 if direction == 0 else 0
        total = b[edge:edge + 1, :]
        job["k_dec"] = (k_full * jnp.exp2(total - anchors[full])).astype(BF16)
        job["q_in"] = (job["q"] * jnp.exp2(b)).astype(BF16)
        job["decay"] = jnp.exp2(total)

    for job in jobs:
        job["u_t"] = lax.dot_general(job["v"], job["k_dec"], TN_DIMS,
                                     preferred_element_type=F32)

    out, pos = [], 0
    for q, *_ in groups:
        n = q.shape[0] // c
        out.append([(jnp.where(j["tri"], j["a"], 0.0).astype(BF16), j["v"], j["q_in"], j["u_t"],
                     j["decay"]) for j in jobs[pos:pos + n]])
        pos += n
    return out


def _hgrn_kernel(*refs, seq_len, layer_j, n_lb, has_s0, emit_state):
    q_ref, zf_ref, zb_ref, i_ref, g_ref, lg_ref, on_ref = refs[:7]
    pos = 7
    s0_ref = st_ref = None
    if has_s0:
        s0_ref = refs[pos]
        pos += 1
    o_ref = refs[pos]
    pos += 1
    if emit_state:
        st_ref = refs[pos]
        pos += 1
    of_scr, ob_scr = refs[pos], refs[pos + 1]

    grp = math.gcd(HGRN_GROUP, seq_len // HGRN_CHUNK)
    rows = grp * HGRN_CHUNK
    n = seq_len // rows
    dv = q_ref.shape[1]

    lbs = []
    for d in range(2):
        lg = lg_ref[d * n_lb:(d + 1) * n_lb, :]
        e = jnp.exp(lg - jnp.max(lg, axis=0, keepdims=True))
        lbs.append(jnp.sum(e[:layer_j + 1], axis=0, keepdims=True) / jnp.sum(e, axis=0, keepdims=True))

    if has_s0:
        st0 = (s0_ref[0].T, s0_ref[1].T)
    else:
        st0 = (jnp.zeros((dv, dv), F32), jnp.zeros((dv, dv), F32))

    def advance(local, st):
        _, _, q_in, u_t, decay = local
        o = lax.dot_general(q_in, st.astype(BF16), NT_DIMS, preferred_element_type=F32)
        return st * decay + u_t, o

    def add_intra(local, o_inter):
        a, v = local[:2]
        return o_inter + jnp.dot(a, v, preferred_element_type=F32)

    def body(gi, carry):
        st_f, st_b = carry
        rf = pl.ds(pl.multiple_of(gi * rows, rows), rows)
        rb = pl.ds(pl.multiple_of((n - 1 - gi) * rows, rows), rows)
        loc_f, loc_b = _hgrn_group_local([
            (q_ref[rf, :], zf_ref[rf, :], i_ref[rf, :], lbs[0], 0),
            (q_ref[rb, :], zb_ref[rb, :], i_ref[rb, :], lbs[1], 1)])
        o_f, o_b = [None] * grp, [None] * grp
        for ci in range(grp):
            st_f, o_f[ci] = advance(loc_f[ci], st_f)
            st_b, o_b[grp - 1 - ci] = advance(loc_b[grp - 1 - ci], st_b)
        o_f = [add_intra(l, o) for l, o in zip(loc_f, o_f)]
        o_b = [add_intra(l, o) for l, o in zip(loc_b, o_b)]
        of_scr[rf, :] = jnp.concatenate(o_f, axis=0)
        ob_scr[rb, :] = jnp.concatenate(o_b, axis=0)
        return st_f, st_b

    st_f, st_b = lax.fori_loop(0, n, body, st0)
    if emit_state:
        st_ref[0] = st_f.T
        st_ref[1] = st_b.T

    fin_rows = math.gcd(seq_len, 256)
    onorm = on_ref[...]

    def finish(ri, _):
        r = pl.ds(pl.multiple_of(ri * fin_rows, fin_rows), fin_rows)
        o = of_scr[r, :] + ob_scr[r, :]
        y = o * lax.rsqrt(jnp.mean(o * o, axis=-1, keepdims=True) + EPS) * onorm
        o_ref[r, :] = (y * _silu(g_ref[r, :].astype(F32))).astype(o_ref.dtype)
        return 0

    lax.fori_loop(0, seq_len // fin_rows, finish, 0)


def _hgrn_scan(proj, lb_logits, onorm, s0, *, layer_j, n_seq, seq_len, row_off, heads, emit_state):
    dk = LANES
    d = heads * dk
    blk_off = row_off // seq_len
    n_lb = lb_logits.shape[1]

    def sec(k):
        return pl.BlockSpec((seq_len, dk), lambda b, h: (b + blk_off, k * heads + h))

    in_specs = [sec(0), sec(1), sec(2), sec(3), sec(4),
                pl.BlockSpec((2 * n_lb, dk), lambda b, h: (0, h)),
                pl.BlockSpec((1, dk), lambda b, h: (0, 0))]
    args = [proj] * 5 + [lb_logits.reshape(2 * n_lb, d), onorm.reshape(1, dk)]
    if s0 is not None:
        in_specs.append(pl.BlockSpec((None, None, 2, None, dk, dk),
                                     lambda b, h: (b, layer_j, 0, h, 0, 0)))
        args.append(s0)
    out_shape = [jax.ShapeDtypeStruct((n_seq * seq_len, d), BF16)]
    out_specs = [pl.BlockSpec((seq_len, dk), lambda b, h: (b, h))]
    if emit_state:
        out_shape.append(jax.ShapeDtypeStruct((n_seq, 2, heads, dk, dk), F32))
        out_specs.append(pl.BlockSpec((None, 2, None, dk, dk), lambda b, h: (b, 0, h, 0, 0)))
    kern = functools.partial(_hgrn_kernel, seq_len=seq_len, layer_j=layer_j, n_lb=n_lb,
                             has_s0=s0 is not None, emit_state=emit_state)
    return pl.pallas_call(
        kern,
        grid=(n_seq, heads),
        in_specs=in_specs,
        out_specs=out_specs,
        out_shape=out_shape,
        scratch_shapes=[pltpu.VMEM((seq_len, dk), F32), pltpu.VMEM((seq_len, dk), F32)],
        compiler_params=_params("arbitrary", "arbitrary"),
        name="hgrn_scan",
    )(*args)


def _attn_kernel(*refs, has_cache, lam_init, q_scale):
    q_ref, k_ref, v_ref = refs[:3]
    pos = 3
    kc_ref = vc_ref = None
    if has_cache:
        kc_ref, vc_ref = refs[3], refs[4]
        pos = 5
    lam_ref, sub_ref, o_ref = refs[pos:pos + 3]

    lp = lam_ref[...]
    lam = (jnp.exp(jnp.sum(lp[0:1] * lp[1:2], axis=-1, keepdims=True))
           - jnp.exp(jnp.sum(lp[2:3] * lp[3:4], axis=-1, keepdims=True)) + lam_init)

    hd = LANES
    v_new = v_ref[...].astype(BF16)
    v_old = vc_ref[...].astype(BF16) if has_cache else None
    tq = q_ref.shape[0]
    rb = min(tq, ATTN_ROW_BLOCK)
    chains = [(r, j) for r in range(0, tq, rb) for j in range(2)]
    def score(r, j):
        cols = slice(j * hd, (j + 1) * hd)
        qj = q_ref[r:r + rb, cols]
        if q_scale != 1.0:
            qj = qj.astype(F32) * q_scale
        qj = qj.astype(BF16)
        s_new = lax.dot_general(qj, k_ref[:, cols].astype(BF16), NT_DIMS,
                                preferred_element_type=F32)
        s_old = None
        if has_cache:
            s_old = lax.dot_general(qj, kc_ref[:, cols].astype(BF16), NT_DIMS,
                                    preferred_element_type=F32)
        return s_new, s_old

    ahead = min(ATTN_LOOKAHEAD, len(chains))
    scores = [score(*c) for c in chains[:ahead]]
    outs = {}
    for ci, (r, j) in enumerate(chains):
        if ci + ahead < len(chains):
            scores.append(score(*chains[ci + ahead]))
        s_new, s_old = scores[ci]
        mx = jnp.max(s_new, axis=-1, keepdims=True)
        if has_cache:
            mx = jnp.maximum(mx, jnp.max(s_old, axis=-1, keepdims=True))
        p_new = jnp.exp2(s_new - mx)
        den = jnp.sum(p_new, axis=-1, keepdims=True)
        acc = jnp.dot(p_new.astype(BF16), v_new, preferred_element_type=F32)
        if has_cache:
            p_old = jnp.exp2(s_old - mx)
            den += jnp.sum(p_old, axis=-1, keepdims=True)
            acc += jnp.dot(p_old.astype(BF16), v_old, preferred_element_type=F32)
        outs[r, j] = acc / den
    sub = sub_ref[...]
    for r in range(0, tq, rb):
        o = outs[r, 0] - lam * outs[r, 1]
        y = o * lax.rsqrt(jnp.mean(o * o, axis=-1, keepdims=True) + EPS) * sub
        o_ref[r:r + rb, :] = (y * (1.0 - lam_init)).astype(o_ref.dtype)


def _diff_attention(qkv, cache_k, cache_v, lam_p, subln, *, n_seq, seq_len, heads, lam_init,
                    q_scale, tq, cache_layer=0):
    hd2 = 2 * LANES
    d = heads * hd2
    qb = seq_len // tq
    fused = not isinstance(qkv, (list, tuple))
    k_off, v_off = (heads, 2 * heads) if fused else (0, 0)
    in_specs = [
        pl.BlockSpec((tq, hd2), lambda b, h, i: (b * qb + i, h)),
        pl.BlockSpec((seq_len, hd2), lambda b, h, i: (b, k_off + h)),
        pl.BlockSpec((seq_len, hd2), lambda b, h, i: (b, v_off + h)),
    ]
    args = [qkv, qkv, qkv] if fused else list(qkv)
    if cache_k is not None:
        past = cache_k.shape[2]
        in_specs += [pl.BlockSpec((None, None, past, hd2),
                                  lambda b, h, i: (b, cache_layer, 0, h))] * 2
        args += [cache_k, cache_v]
    in_specs += [pl.BlockSpec(lam_p.shape, lambda b, h, i: (0, 0)),
                 pl.BlockSpec((1, hd2), lambda b, h, i: (0, 0))]
    args += [lam_p, subln.reshape(1, hd2)]
    kern = functools.partial(_attn_kernel, has_cache=cache_k is not None, lam_init=lam_init,
                             q_scale=q_scale)
    return pl.pallas_call(
        kern,
        grid=(n_seq, heads, qb),
        in_specs=in_specs,
        out_specs=pl.BlockSpec((tq, hd2), lambda b, h, i: (b * qb + i, h)),
        out_shape=jax.ShapeDtypeStruct((n_seq * seq_len, d), BF16),
        compiler_params=_params("arbitrary", "arbitrary", "arbitrary"),
        name="diff_attention",
    )(*args)


def _out_proj_kernel(*refs, a_bounds, x_bounds):
    a_refs = refs[:len(a_bounds)]
    x_refs = refs[len(a_bounds):len(a_bounds) + len(x_bounds)]
    w_ref, gate_ref, o_ref = refs[len(a_bounds) + len(x_bounds):]
    i = pl.program_id(0)
    for a_ref, (a_lo, a_cnt) in zip(a_refs, a_bounds):
        for x_ref, (x_lo, x_cnt) in zip(x_refs, x_bounds):
            lo, hi = max(a_lo, x_lo), min(a_lo + a_cnt, x_lo + x_cnt)
            if lo >= hi:
                continue

            @pl.when((i >= lo) & (i < hi))
            def _(a_ref=a_ref, x_ref=x_ref):
                y = jnp.dot(a_ref[...], w_ref[...], preferred_element_type=F32)
                o_ref[...] = x_ref[...] + gate_ref[...] * y


def _out_project(a_segs, w, x_segs, mod, gate_chunk, *, p_rows, s_len, tm=512):
    d = x_segs[0].shape[1]
    t = sum(x.shape[0] for x in x_segs)
    kdim = w.shape[0]
    a_specs, a_bounds = _row_segment_specs(a_segs, tm)
    x_specs, x_bounds = _row_segment_specs(x_segs, tm)
    in_specs = a_specs + x_specs + [
        pl.BlockSpec((kdim, d), lambda i, j: (0, 0)),
        _mod_spec(gate_chunk, d, tm, p_rows, s_len),
    ]
    return pl.pallas_call(
        functools.partial(_out_proj_kernel, a_bounds=a_bounds, x_bounds=x_bounds),
        grid=(t // tm, 1),
        in_specs=in_specs,
        out_specs=pl.BlockSpec((tm, d), lambda i, j: (i, 0)),
        out_shape=jax.ShapeDtypeStruct((t, d), F32),
        compiler_params=_params("arbitrary", "arbitrary"),
        name="out_project_residual",
    )(*a_segs, *x_segs, w, mod)


def _ffn_kernel(x_ref, xp_ref, xn_ref, g_ref, sh_ref, sc_ref, gate_ref, wg_ref, wv_ref,
                conv_a, wd_a, conv_b, wd_b, *rest, tm, nj, p_rows, p_len, s_len, final):
    if final:
        gf_ref, yp_ref, ys_ref, h_ref, act_ref, ug_scr, uv_scr, acc_ref = rest
    else:
        o_ref, h_ref, act_ref, ug_scr, uv_scr, acc_ref = rest
    i = pl.program_id(0)
    j = pl.program_id(1)
    hs = xp_ref.shape[0]
    sub = FFN_TILE
    ext = hs + tm
    row0 = i * tm
    in_prompt = row0 < p_rows
    inner_edges = jnp.where(in_prompt, 1.0, 0.0)
    win = lax.broadcasted_iota(jnp.int32, (2 * SUBLANES, 1), 0)

    def up(cols):
        h = h_ref[...]
        return (jnp.dot(h, wg_ref[:, cols], preferred_element_type=F32),
                jnp.dot(h, wv_ref[:, cols], preferred_element_type=F32))

    def conv(u_all, u_scr, cw, cb):
        u_scr[0:ext, :] = u_all
        u_scr[ext:ext + 1, :] = u_all[hs - 2:hs - 1]
        u = u_all[hs:hs + tm]
        prev = u_scr[hs - 1:hs - 1 + tm, :]
        nxt = u_scr[hs + 1:hs + 1 + tm, :]
        y = prev * cw[0:1] + u * cw[1:2] + nxt * cw[2:3] + cb
        pieces, done = [], 0
        for edge in range(p_len, tm, p_len):
            lo, hi = edge - SUBLANES, edge + SUBLANES
            leak = (jnp.where(win == SUBLANES - 1, nxt[lo:hi] * cw[2:3], 0.0)
                    + jnp.where(win == SUBLANES, prev[lo:hi] * cw[0:1], 0.0))
            pieces += [y[done:lo], y[lo:hi] - inner_edges * leak]
            done = hi
        return jnp.concatenate(pieces + [y[done:]], axis=0) if pieces else y

    def activate(ug, uv, conv_ref):
        p = conv_ref[...]
        gate = conv(ug, ug_scr, p[0:3], p[3:4])
        val = conv(uv, uv_scr, p[4:7], p[7:8])
        return (_silu(gate) * val).astype(BF16)

    def activate_a(ug, uv):
        return activate(ug, uv, conv_a)

    cols_a, cols_b = slice(0, sub), slice(sub, 2 * sub)

    @pl.when(j == 0)
    def _():
        g, sh, sc = g_ref[...], sh_ref[...], sc_ref[...]
        h_ref[hs:hs + tm, :] = _norm_mod(x_ref[...], g, sh, sc).astype(BF16)
        end = row0 + tm
        at_seq_end = jnp.where(in_prompt, lax.rem(end, p_len) == 0,
                               lax.rem(end - p_rows, s_len) == 0)
        at_seq_start = jnp.where(in_prompt, lax.rem(row0, p_len) == 0,
                                 lax.rem(row0 - p_rows, s_len) == 0)
        after = _norm_mod(xn_ref[...], g, sh, sc) * jnp.where(at_seq_end, 0.0, 1.0)
        before = _norm_mod(xp_ref[...], g, sh, sc) * jnp.where(at_seq_start, 0.0, 1.0)
        r = lax.broadcasted_iota(jnp.int32, (hs, 1), 0)
        halo = jnp.where(r == hs - 1, before,
                         jnp.where(r == hs - 2, pltpu.roll(after, hs - 2, 0), 0.0))
        h_ref[0:hs, :] = halo.astype(BF16)
        ua = up(cols_a)
        act_a = activate_a(*ua)
        ub = up(cols_b)
        acc_ref[...] = jnp.dot(act_a, wd_a[...], preferred_element_type=F32)
        act_ref[0] = activate(*ub, conv_b)

    for parity in range(2):
        @pl.when((j > 0) & (j < nj) & (j % 2 == parity))
        def _(parity=parity):
            ua = up(cols_a)
            act_a = activate_a(*ua)
            ub = up(cols_b)
            acc_ref[...] += (
                jnp.dot(act_ref[1 - parity], wd_b[...], preferred_element_type=F32)
                + jnp.dot(act_a, wd_a[...], preferred_element_type=F32))
            act_ref[parity] = activate(*ub, conv_b)

    @pl.when(j == nj)
    def _():
        down = acc_ref[...] + jnp.dot(act_ref[(nj - 1) % 2], wd_b[...],
                                      preferred_element_type=F32)
        res = x_ref[...] + gate_ref[...] * down
        if not final:
            o_ref[...] = res
        else:
            y = res * lax.rsqrt(jnp.mean(res * res, axis=-1, keepdims=True) + EPS) * gf_ref[...]

            @pl.when(in_prompt)
            def _():
                yp_ref[...] = y

            @pl.when(jnp.logical_not(in_prompt))
            def _():
                ys_ref[...] = y


def _ffn(x, g, mod, shift_chunk, scale_chunk, gate_chunk, w_up, cw, cb, w_down, *,
         p_rows, p_len, s_len, tm=512, final_gain=None):
    t, d = x.shape
    ff = w_up.shape[1] // 2
    sub = FFN_TILE
    assert ff % (2 * sub) == 0
    tm = math.gcd(tm, p_rows, s_len)
    assert (tm % p_len == 0 or p_len % tm == 0) and p_rows % tm == 0 and s_len % tm == 0
    nj = ff // (2 * sub)
    nt = ff // sub
    hs = 16
    per = tm // hs
    last_blk = t // hs - 1

    def tile_a(i, j):
        return jnp.minimum(2 * j, nt - 2)

    def tile_b(i, j):
        return jnp.minimum(2 * j + 1, nt - 1)

    def parked_b(i, j):
        return jnp.maximum(2 * j - 1, 1)

    def tile_specs(conv_tile, down_tile):
        return [
            pl.BlockSpec((None, 2 * (taps + 1), sub), lambda i, j: (conv_tile(i, j), 0, 0)),
            pl.BlockSpec((sub, d), lambda i, j: (down_tile(i, j), 0)),
        ]

    taps = cw.shape[0]
    conv_p = jnp.concatenate([cw[:, :ff], cb[None, :ff], cw[:, ff:], cb[None, ff:]], axis=0)
    conv_p = conv_p.reshape(2 * (taps + 1), nt, sub).transpose(1, 0, 2)

    in_specs = [
        pl.BlockSpec((tm, d), lambda i, j: (i, 0)),
        pl.BlockSpec((hs, d), lambda i, j: (jnp.maximum(i * per - 1, 0), 0)),
        pl.BlockSpec((hs, d), lambda i, j: (jnp.minimum((i + 1) * per, last_blk), 0)),
        pl.BlockSpec((1, d), lambda i, j: (0, 0)),
        _mod_spec(shift_chunk, d, tm, p_rows, s_len),
        _mod_spec(scale_chunk, d, tm, p_rows, s_len),
        _mod_spec(gate_chunk, d, tm, p_rows, s_len),
        pl.BlockSpec((None, d, 2 * sub), lambda i, j: (jnp.minimum(j, nj - 1), 0, 0)),
        pl.BlockSpec((None, d, 2 * sub), lambda i, j: (jnp.minimum(j, nj - 1) + nj, 0, 0)),
    ] + tile_specs(tile_a, tile_a) + tile_specs(tile_b, parked_b)
    w_up = _column_tiles(w_up, 2 * sub)
    tile_args = [conv_p, w_down.astype(BF16)]
    final = final_gain is not None
    kern = functools.partial(_ffn_kernel, tm=tm, nj=nj, p_rows=p_rows, p_len=p_len,
                             s_len=s_len, final=final)
    args = [x, x, x, g.reshape(1, d), mod, mod, mod, w_up, w_up, *tile_args, *tile_args]
    if final:
        pt = p_rows // tm
        in_specs.append(pl.BlockSpec((1, d), lambda i, j: (0, 0)))
        args.append(final_gain.reshape(1, d))
        out_specs = [pl.BlockSpec((tm, d), lambda i, j: (jnp.minimum(i, pt - 1), 0)),
                     pl.BlockSpec((tm, d), lambda i, j: (jnp.maximum(i - pt, 0), 0))]
        out_shape = [jax.ShapeDtypeStruct((p_rows, d), F32),
                     jax.ShapeDtypeStruct((t - p_rows, d), F32)]
    else:
        out_specs = pl.BlockSpec((tm, d), lambda i, j: (i, 0))
        out_shape = jax.ShapeDtypeStruct((t, d), F32)
    return pl.pallas_call(
        kern,
        grid=(t // tm, nj + 1),
        in_specs=in_specs,
        out_specs=out_specs,
        out_shape=out_shape,
        scratch_shapes=[pltpu.VMEM((tm + hs, d), BF16),
                        pltpu.VMEM((2, tm, sub), BF16),
                        pltpu.VMEM((tm + hs + SUBLANES, sub), F32),
                        pltpu.VMEM((tm + hs + SUBLANES, sub), F32),
                        pltpu.VMEM((tm, d), F32)],
        compiler_params=_params("arbitrary", "arbitrary"),
        name="conv_ffn",
    )(*args)


def _rope_tables(seq_len, hd):
    pos = jnp.arange(seq_len)
    row = (pos // GRID_W).astype(F32)
    col = (pos % GRID_W).astype(F32)
    nf = hd // 4
    inv = ROPE_BASE ** (-jnp.arange(nf, dtype=F32) / nf)
    ar = row[:, None] * inv
    ac = col[:, None] * inv
    cos = jnp.concatenate([jnp.cos(ar), jnp.cos(ar), jnp.cos(ac), jnp.cos(ac)], axis=1)
    sin = jnp.concatenate([-jnp.sin(ar), jnp.sin(ar), -jnp.sin(ac), jnp.sin(ac)], axis=1)
    return cos, sin


def kernel(x_prompt, x_sample, c, cache_hgrn_state, cache_attn_k, cache_attn_v, c_ctx, w_mod, b_mod, norm_mix, norm_ffn, w_hgrn_in, hgrn_lb_logits, hgrn_onorm, w_hgrn_out, w_attn_in, attn_lambda, attn_subln, w_attn_out, w_ffn_up, ffn_conv_w, ffn_conv_b, w_ffn_down, norm_final):
    batch, p_len, d = x_prompt.shape
    dec_batch, s_len, _ = x_sample.shape
    depth = w_mod.shape[0]
    p_rows = batch * p_len
    s_rows = dec_batch * s_len
    hgrn_heads = d // LANES
    diff_heads = d // (2 * LANES)
    hd = LANES
    assert 1 + dec_batch <= MOD_ROWS and p_rows % s_len == 0

    x = [x_prompt.reshape(p_rows, d), x_sample.reshape(s_rows, d)]
    cvec = jnp.concatenate(
        [c_ctx[None, :], c, jnp.zeros((MOD_ROWS - 1 - dec_batch, d), F32)], axis=0)
    mods = _modulation(cvec, w_mod, b_mod)
    seg = dict(p_rows=p_rows, s_len=s_len)

    hgrn_states, attn_ks, attn_vs = [], [], []
    for l in range(depth):
        mod = mods[l][:, None, :]
        j = l // N_MIXERS
        if l % N_MIXERS == 0:
            proj = _project(x, norm_mix[l], mod, 0, 1, w_hgrn_in[j], row_off=0,
                            rows=p_rows + s_rows, out_dtype=F32, tm=1024, tn=1024, **seg)
            mix_p, st = _hgrn_scan(proj, hgrn_lb_logits, hgrn_onorm[j], None, layer_j=j,
                                   n_seq=batch, seq_len=p_len, row_off=0, heads=hgrn_heads,
                                   emit_state=True)
            hgrn_states.append(st)
            mix_s, = _hgrn_scan(proj, hgrn_lb_logits, hgrn_onorm[j], cache_hgrn_state,
                                layer_j=j, n_seq=dec_batch, seq_len=s_len, row_off=p_rows,
                                heads=hgrn_heads, emit_state=False)
            w_out = w_hgrn_out[j]
        else:
            lam_init = 0.8 - 0.6 * math.exp(-0.3 * l)
            q_scale = hd ** -0.5 * LOG2_E
            w_in = w_attn_in[j]
            qkv_p = _project(x, norm_mix[l], mod, 0, 1, w_in, row_off=0, rows=p_rows,
                             out_dtype=F32, tn=1024, sections=[d, d, d], **seg)
            attn_ks.append(qkv_p[1].reshape(batch, p_len, 2 * diff_heads, hd))
            attn_vs.append(qkv_p[2].reshape(batch, p_len, diff_heads, 2 * hd))
            mix_p = _diff_attention(qkv_p, None, None, attn_lambda[j], attn_subln[j],
                                    n_seq=batch, seq_len=p_len, heads=diff_heads,
                                    lam_init=lam_init, q_scale=q_scale, tq=p_len)
            qkv_s = _project(x, norm_mix[l], mod, 0, 1, w_in, row_off=p_rows, rows=s_rows,
                             out_dtype=BF16, rope_tables=_rope_tables(s_len, hd),
                             q_scale=q_scale, tm=1024, tn=1024, **seg)
            n_attn, past = cache_attn_k.shape[1:3]
            mix_s = _diff_attention(qkv_s, cache_attn_k.reshape(dec_batch, n_attn, past, d),
                                    cache_attn_v.reshape(dec_batch, n_attn, past, d),
                                    attn_lambda[j], attn_subln[j], n_seq=dec_batch,
                                    seq_len=s_len, heads=diff_heads, lam_init=lam_init,
                                    q_scale=1.0, tq=512, cache_layer=j)
            w_out = w_attn_out[j]
        x = _out_project([mix_p, mix_s], w_out.astype(BF16), x, mod, 2, **seg)
        last = l == depth - 1
        x = _ffn(x, norm_ffn[l], mod, 3, 4, 5, w_ffn_up[l], ffn_conv_w[l], ffn_conv_b[l],
                 w_ffn_down[l], p_len=p_len, final_gain=norm_final if last else None, **seg)
        x = list(x) if last else [x]

    y_prompt = x[0].reshape(batch, p_len, d)
    y_sample = x[1].reshape(dec_batch, s_len, d)
    new_hgrn_state = jnp.stack(hgrn_states, axis=1)
    new_attn_k = jnp.stack(attn_ks, axis=1)
    new_attn_v = jnp.stack(attn_vs, axis=1)
    return (y_prompt, y_sample, new_hgrn_state, new_attn_k, new_attn_v)
```

```python
import functools
import math

import jax
import jax.numpy as jnp
from jax import lax
from jax.experimental import pallas as pl
from jax.experimental.pallas import tpu as pltpu

F32 = jnp.float32
BF16 = jnp.bfloat16

EPS = 1e-6
GRID_W = 64
ROPE_BASE = 10000.0
N_MIXERS = 2

LANES = 128
SUBLANES = 8
MXU_COLS = 256
MOD_ROWS = 16
HGRN_CHUNK = 64
HGRN_SUB = 16
HGRN_GROUP = 16
ATTN_ROW_BLOCK = 128
ATTN_LOOKAHEAD = 1
FFN_TILE = 256
LOG2_E = 1.4426950408889634
VMEM_LIMIT = 56 * 1024 * 1024

NT_DIMS = (((1,), (1,)), ((), ()))
TN_DIMS = (((0,), (0,)), ((), ()))


def _params(*sem):
    return pltpu.CompilerParams(dimension_semantics=sem, vmem_limit_bytes=VMEM_LIMIT)


def _tile(n, want):
    best = LANES
    for cand in range(LANES, min(n, want) + 1, LANES):
        if n % cand == 0:
            best = cand
    assert n % best == 0
    return best


def _column_tiles(w, tn):
    k, n = w.shape
    return w.astype(BF16).reshape(k, n // tn, tn).transpose(1, 0, 2)


def _silu(x):
    h = 0.5 * x
    return h + h * jnp.tanh(h)


def _split_bf16(x):
    hi = x.astype(BF16)
    lo = (x - hi.astype(F32)).astype(BF16)
    return hi, lo


def _mod_kernel(c_ref, w_ref, b_ref, o_ref):
    a_hi, a_lo = _split_bf16(_silu(c_ref[...]))
    w_hi, w_lo = _split_bf16(w_ref[...])
    rows = a_hi.shape[0]
    both = jnp.dot(jnp.concatenate([a_hi, a_lo], axis=0), w_hi, preferred_element_type=F32)
    acc = both[:rows] + both[rows:] + jnp.dot(a_hi, w_lo, preferred_element_type=F32)
    o_ref[...] = acc + b_ref[...]


def _modulation(cvec, w_mod, b_mod, tn=1024):
    depth, d, n = w_mod.shape
    tn = _tile(n, tn)
    return pl.pallas_call(
        _mod_kernel,
        grid=(depth, n // tn),
        in_specs=[
            pl.BlockSpec((MOD_ROWS, d), lambda l, j: (0, 0)),
            pl.BlockSpec((None, d, tn), lambda l, j: (l, 0, j)),
            pl.BlockSpec((None, 1, tn), lambda l, j: (l, 0, j)),
        ],
        out_specs=pl.BlockSpec((None, MOD_ROWS, tn), lambda l, j: (l, 0, j)),
        out_shape=jax.ShapeDtypeStruct((depth, MOD_ROWS, n), F32),
        compiler_params=_params("arbitrary", "arbitrary"),
        name="modulation",
    )(cvec, w_mod, b_mod.reshape(depth, 1, n))


def _norm_mod(x, g, shift, scale):
    y = x * lax.rsqrt(jnp.mean(x * x, axis=-1, keepdims=True) + EPS) * g
    return y * (1.0 + scale) + shift


def _seq_of_row(row0, p_rows, s_len):
    return jnp.where(row0 < p_rows, 0, 1 + (row0 - p_rows) // s_len)


def _mod_spec(chunk, d, tm, p_rows, s_len, tile_off=0):
    return pl.BlockSpec(
        (None, 1, d),
        lambda i, j: (_seq_of_row((i + tile_off) * tm, p_rows, s_len), 0, chunk))


def _proj_kernel(*refs, x_bounds, tile_off, rope, q_tiles, qk_tiles, q_scale, split):
    x_refs = refs[:len(x_bounds)]
    g_ref, sh_ref, sc_ref, w_ref, *rest = refs[len(x_bounds):]
    if rope:
        cos_ref, sin_ref, o_ref, h_ref = rest
    elif split:
        *o_refs, h_ref = rest
    else:
        o_ref, h_ref = rest
    tile = pl.program_id(0) + tile_off
    j = pl.program_id(1)

    for x_ref, (lo, cnt) in zip(x_refs, x_bounds):
        @pl.when((j == 0) & (tile >= lo) & (tile < lo + cnt))
        def _(x_ref=x_ref):
            h_ref[...] = _norm_mod(x_ref[...], g_ref[...], sh_ref[...],
                                   sc_ref[...]).astype(BF16)

    if split:
        lo = 0
        for cnt, out in zip(split, o_refs):
            @pl.when((j >= lo) & (j < lo + cnt))
            def _(out=out):
                out[...] = jnp.dot(h_ref[...], w_ref[...],
                                   preferred_element_type=F32).astype(out.dtype)
            lo += cnt
        return

    if not rope:
        o_ref[...] = jnp.dot(h_ref[...], w_ref[...],
                             preferred_element_type=F32).astype(o_ref.dtype)
        return

    @pl.when(j < qk_tiles)
    def _():
        tn = o_ref.shape[1]
        sub = min(tn, MXU_COLS)
        h = h_ref[...]
        accs = [jnp.dot(h, w_ref[:, c:c + sub], preferred_element_type=F32)
                for c in range(0, tn, sub)]
        scale = jnp.where(j < q_tiles, q_scale, 1.0)
        cos = cos_ref[...] * scale
        sin = sin_ref[...] * scale
        lane = lax.broadcasted_iota(jnp.int32, cos.shape, 1)
        first_of_pair = (lane // (LANES // 4)) % 2 == 0
        for ci, acc in enumerate(accs):
            for s in range(sub // LANES):
                xs = acc[:, s * LANES:(s + 1) * LANES]
                partner = jnp.where(first_of_pair,
                                    pltpu.roll(xs, LANES - LANES // 4, 1),
                                    pltpu.roll(xs, LANES // 4, 1))
                col = ci * sub + s * LANES
                o_ref[:, col:col + LANES] = (xs * cos + partner * sin).astype(o_ref.dtype)

    @pl.when(j >= qk_tiles)
    def _():
        o_ref[...] = jnp.dot(h_ref[...], w_ref[...],
                             preferred_element_type=F32).astype(o_ref.dtype)


def _row_segment_specs(segs, tm, tile_off=0, buffers=None):
    specs, bounds, lo = [], [], 0
    for k, a in enumerate(segs):
        mode = {} if buffers is None or buffers[k] is None else dict(
            pipeline_mode=pl.Buffered(buffers[k]))
        cnt = a.shape[0] // tm
        specs.append(pl.BlockSpec(
            (tm, a.shape[1]),
            lambda i, j, lo=lo, cnt=cnt: (jnp.clip(i + tile_off - lo, 0, cnt - 1), 0), **mode))
        bounds.append((lo, cnt))
        lo += cnt
    return specs, tuple(bounds)


def _project(x_segs, g, mod, shift_chunk, scale_chunk, w, *, row_off, rows, out_dtype,
             p_rows, s_len, tm=512, tn=512, rope_tables=None, q_scale=1.0, sections=None):
    d = x_segs[0].shape[1]
    n = w.shape[1]
    tm = math.gcd(tm, p_rows, s_len)
    tile_off = row_off // tm
    rope = rope_tables is not None
    tn = _tile(n // 3 if rope else math.gcd(n, *(sections or [n])), tn)
    split = tuple(c // tn for c in sections) if sections else ()
    biggest = max(range(len(x_segs)), key=lambda k: x_segs[k].shape[0])
    x_specs, x_bounds = _row_segment_specs(
        x_segs, tm, tile_off,
        buffers=[None if k == biggest else 1 for k in range(len(x_segs))])
    in_specs = x_specs + [
        pl.BlockSpec((1, d), lambda i, j: (0, 0)),
        _mod_spec(shift_chunk, d, tm, p_rows, s_len, tile_off),
        _mod_spec(scale_chunk, d, tm, p_rows, s_len, tile_off),
        pl.BlockSpec((d, tn), lambda i, j: (0, j)),
    ]
    args = list(x_segs) + [g.reshape(1, d), mod, mod, w.astype(BF16)]
    q_tiles = qk_tiles = 0
    if rope:
        cos, sin = rope_tables
        tiles_per_seq = s_len // tm
        in_specs += [pl.BlockSpec((tm, LANES), lambda i, j: (i % tiles_per_seq, 0))] * 2
        args += [cos, sin]
        q_tiles = (n // 3) // tn
        qk_tiles = 2 * q_tiles
    kern = functools.partial(_proj_kernel, x_bounds=x_bounds, tile_off=tile_off, rope=rope,
                             q_tiles=q_tiles, qk_tiles=qk_tiles, q_scale=q_scale, split=split)
    if split:
        starts = [sum(split[:s]) for s in range(len(split))]
        out_specs = [pl.BlockSpec((tm, tn),
                                  lambda i, j, lo=lo, cnt=cnt: (i, jnp.clip(j - lo, 0, cnt - 1)))
                     for lo, cnt in zip(starts, split)]
        dtypes = out_dtype if isinstance(out_dtype, (list, tuple)) else [out_dtype] * len(split)
        out_shape = [jax.ShapeDtypeStruct((rows, c), dt) for c, dt in zip(sections, dtypes)]
    else:
        out_specs = pl.BlockSpec((tm, tn), lambda i, j: (i, j))
        out_shape = jax.ShapeDtypeStruct((rows, n), out_dtype)
    return pl.pallas_call(
        kern,
        grid=(rows // tm, n // tn),
        in_specs=in_specs,
        out_specs=out_specs,
        out_shape=out_shape,
        scratch_shapes=[pltpu.VMEM((tm, d), BF16)],
        compiler_params=_params("arbitrary", "arbitrary"),
        name="norm_mod_project",
    )(*args)


def _hgrn_group_local(groups):
    c, m = HGRN_CHUNK, HGRN_SUB
    nb = c // m
    t = lax.broadcasted_iota(jnp.int32, (c, c), 0)
    s = lax.broadcasted_iota(jnp.int32, (c, c), 1)

    jobs = []
    for q, z, v, lb, direction in groups:
        dk = q.shape[1]
        tri = (s <= t) if direction == 0 else (s >= t)
        cum = jnp.where(tri, 1.0, 0.0).astype(BF16)
        f = lb + (1.0 - lb) * jax.nn.sigmoid(z)
        g2 = jnp.log(f) * LOG2_E
        k = 1.0 - f
        g_hi = g2.astype(BF16)
        g_lo = (g2 - g_hi.astype(F32)).astype(BF16)
        v_bf = v.astype(BF16)
        for ci in range(q.shape[0] // c):
            rows = slice(ci * c, (ci + 1) * c)
            jobs.append(dict(direction=direction, tri=tri, cum=cum, q=q[rows], k=k[rows],
                             v=v_bf[rows], g=jnp.concatenate([g_hi[rows], g_lo[rows]], axis=1)))

    def padded(x, lo, hi):
        parts = [jnp.zeros((lo, dk), BF16)] if lo else []
        parts.append(x)
        if hi < c:
            parts.append(jnp.zeros((c - hi, dk), BF16))
        return jnp.concatenate(parts, axis=0) if len(parts) > 1 else x

    for job in jobs:
        job["sums"] = jnp.dot(job["cum"], job["g"], preferred_element_type=F32)

    for job in jobs:
        direction = job["direction"]
        sums = job["sums"]
        b = sums[:, :dk] + sums[:, dk:]
        anchors = []
        for i in range(nb):
            a_row = i * m + (m // 2 - 1 if direction == 0 else m // 2)
            anchors.append(b[a_row:a_row + 1, :])
        b_anchor = jnp.concatenate([jnp.broadcast_to(r, (m, dk)) for r in anchors], axis=0)
        q_rel = (job["q"] * jnp.exp2(b - b_anchor)).astype(BF16)
        full = nb - 1 if direction == 0 else 0
        q_cat, k_cat = [], []
        k_full = None
        for i in range(nb):
            lo, hi = (0, m * (i + 1)) if direction == 0 else (m * i, c)
            ki = job["k"][lo:hi] * jnp.exp2(anchors[i] - b[lo:hi])
            if i == full:
                k_full = ki
            k_cat.append(padded(ki.astype(BF16), lo, hi))
            q_cat.append(padded(q_rel[m * i:m * (i + 1)], m * i, m * (i + 1)))
        job["a"] = lax.dot_general(jnp.concatenate(q_cat, axis=1),
                                   jnp.concatenate(k_cat, axis=1), NT_DIMS,
                                   preferred_element_type=F32)
        edge = c - 1 if direction == 0 else 0
        total = b[edge:edge + 1, :]
        job["k_dec"] = (k_full * jnp.exp2(total - anchors[full])).astype(BF16)
        job["q_in"] = (job["q"] * jnp.exp2(b)).astype(BF16)
        job["decay"] = jnp.exp2(total)

    for job in jobs:
        job["u_t"] = lax.dot_general(job["v"], job["k_dec"], TN_DIMS,
                                     preferred_element_type=F32)

    out, pos = [], 0
    for q, *_ in groups:
        n = q.shape[0] // c
        out.append([(jnp.where(j["tri"], j["a"], 0.0).astype(BF16), j["v"], j["q_in"], j["u_t"],
                     j["decay"]) for j in jobs[pos:pos + n]])
        pos += n
    return out


def _hgrn_kernel(*refs, seq_len, layer_j, n_lb, has_s0, emit_state):
    q_ref, zf_ref, zb_ref, i_ref, g_ref, lg_ref, on_ref = refs[:7]
    pos = 7
    s0_ref = st_ref = None
    if has_s0:
        s0_ref = refs[pos]
        pos += 1
    o_ref = refs[pos]
    pos += 1
    if emit_state:
        st_ref = refs[pos]
        pos += 1
    of_scr, ob_scr = refs[pos], refs[pos + 1]

    grp = math.gcd(HGRN_GROUP, seq_len // HGRN_CHUNK)
    rows = grp * HGRN_CHUNK
    n = seq_len // rows
    dv = q_ref.shape[1]

    lbs = []
    for d in range(2):
        lg = lg_ref[d * n_lb:(d + 1) * n_lb, :]
        e = jnp.exp(lg - jnp.max(lg, axis=0, keepdims=True))
        lbs.append(jnp.sum(e[:layer_j + 1], axis=0, keepdims=True) / jnp.sum(e, axis=0, keepdims=True))

    if has_s0:
        st0 = (s0_ref[0].T, s0_ref[1].T)
    else:
        st0 = (jnp.zeros((dv, dv), F32), jnp.zeros((dv, dv), F32))

    def advance(local, st):
        _, _, q_in, u_t, decay = local
        o = lax.dot_general(q_in, st.astype(BF16), NT_DIMS, preferred_element_type=F32)
        return st * decay + u_t, o

    def add_intra(local, o_inter):
        a, v = local[:2]
        return o_inter + jnp.dot(a, v, preferred_element_type=F32)

    def body(gi, carry):
        st_f, st_b = carry
        rf = pl.ds(pl.multiple_of(gi * rows, rows), rows)
        rb = pl.ds(pl.multiple_of((n - 1 - gi) * rows, rows), rows)
        loc_f, loc_b = _hgrn_group_local([
            (q_ref[rf, :], zf_ref[rf, :], i_ref[rf, :], lbs[0], 0),
            (q_ref[rb, :], zb_ref[rb, :], i_ref[rb, :], lbs[1], 1)])
        o_f, o_b = [None] * grp, [None] * grp
        for ci in range(grp):
            st_f, o_f[ci] = advance(loc_f[ci], st_f)
            st_b, o_b[grp - 1 - ci] = advance(loc_b[grp - 1 - ci], st_b)
        o_f = [add_intra(l, o) for l, o in zip(loc_f, o_f)]
        o_b = [add_intra(l, o) for l, o in zip(loc_b, o_b)]
        of_scr[rf, :] = jnp.concatenate(o_f, axis=0)
        ob_scr[rb, :] = jnp.concatenate(o_b, axis=0)
        return st_f, st_b

    st_f, st_b = lax.fori_loop(0, n, body, st0)
    if emit_state:
        st_ref[0] = st_f.T
        st_ref[1] = st_b.T

    fin_rows = math.gcd(seq_len, 256)
    onorm = on_ref[...]

    def finish(ri, _):
        r = pl.ds(pl.multiple_of(ri * fin_rows, fin_rows), fin_rows)
        o = of_scr[r, :] + ob_scr[r, :]
        y = o * lax.rsqrt(jnp.mean(o * o, axis=-1, keepdims=True) + EPS) * onorm
        o_ref[r, :] = (y * _silu(g_ref[r, :].astype(F32))).astype(o_ref.dtype)
        return 0

    lax.fori_loop(0, seq_len // fin_rows, finish, 0)


def _hgrn_scan(proj, lb_logits, onorm, s0, *, layer_j, n_seq, seq_len, row_off, heads, emit_state):
    dk = LANES
    d = heads * dk
    blk_off = row_off // seq_len
    n_lb = lb_logits.shape[1]

    def sec(k):
        return pl.BlockSpec((seq_len, dk), lambda b, h: (b + blk_off, k * heads + h))

    in_specs = [sec(0), sec(1), sec(2), sec(3), sec(4),
                pl.BlockSpec((2 * n_lb, dk), lambda b, h: (0, h)),
                pl.BlockSpec((1, dk), lambda b, h: (0, 0))]
    args = [proj] * 5 + [lb_logits.reshape(2 * n_lb, d), onorm.reshape(1, dk)]
    if s0 is not None:
        in_specs.append(pl.BlockSpec((None, None, 2, None, dk, dk),
                                     lambda b, h: (b, layer_j, 0, h, 0, 0)))
        args.append(s0)
    out_shape = [jax.ShapeDtypeStruct((n_seq * seq_len, d), BF16)]
    out_specs = [pl.BlockSpec((seq_len, dk), lambda b, h: (b, h))]
    if emit_state:
        out_shape.append(jax.ShapeDtypeStruct((n_seq, 2, heads, dk, dk), F32))
        out_specs.append(pl.BlockSpec((None, 2, None, dk, dk), lambda b, h: (b, 0, h, 0, 0)))
    kern = functools.partial(_hgrn_kernel, seq_len=seq_len, layer_j=layer_j, n_lb=n_lb,
                             has_s0=s0 is not None, emit_state=emit_state)
    return pl.pallas_call(
        kern,
        grid=(n_seq, heads),
        in_specs=in_specs,
        out_specs=out_specs,
        out_shape=out_shape,
        scratch_shapes=[pltpu.VMEM((seq_len, dk), F32), pltpu.VMEM((seq_len, dk), F32)],
        compiler_params=_params("arbitrary", "arbitrary"),
        name="hgrn_scan",
    )(*args)


def _attn_kernel(*refs, has_cache, lam_init, q_scale):
    q_ref, k_ref, v_ref = refs[:3]
    pos = 3
    kc_ref = vc_ref = None
    if has_cache:
        kc_ref, vc_ref = refs[3], refs[4]
        pos = 5
    lam_ref, sub_ref, o_ref = refs[pos:pos + 3]

    lp = lam_ref[...]
    lam = (jnp.exp(jnp.sum(lp[0:1] * lp[1:2], axis=-1, keepdims=True))
           - jnp.exp(jnp.sum(lp[2:3] * lp[3:4], axis=-1, keepdims=True)) + lam_init)

    hd = LANES
    v_new = v_ref[...].astype(BF16)
    v_old = vc_ref[...].astype(BF16) if has_cache else None
    tq = q_ref.shape[0]
    rb = min(tq, ATTN_ROW_BLOCK)
    chains = [(r, j) for r in range(0, tq, rb) for j in range(2)]
    def score(r, j):
        cols = slice(j * hd, (j + 1) * hd)
        qj = q_ref[r:r + rb, cols]
        if q_scale != 1.0:
            qj = qj.astype(F32) * q_scale
        qj = qj.astype(BF16)
        s_new = lax.dot_general(qj, k_ref[:, cols].astype(BF16), NT_DIMS,
                                preferred_element_type=F32)
        s_old = None
        if has_cache:
            s_old = lax.dot_general(qj, kc_ref[:, cols].astype(BF16), NT_DIMS,
                                    preferred_element_type=F32)
        return s_new, s_old

    ahead = min(ATTN_LOOKAHEAD, len(chains))
    scores = [score(*c) for c in chains[:ahead]]
    outs = {}
    for ci, (r, j) in enumerate(chains):
        if ci + ahead < len(chains):
            scores.append(score(*chains[ci + ahead]))
        s_new, s_old = scores[ci]
        mx = jnp.max(s_new, axis=-1, keepdims=True)
        if has_cache:
            mx = jnp.maximum(mx, jnp.max(s_old, axis=-1, keepdims=True))
        p_new = jnp.exp2(s_new - mx)
        den = jnp.sum(p_new, axis=-1, keepdims=True)
        acc = jnp.dot(p_new.astype(BF16), v_new, preferred_element_type=F32)
        if has_cache:
            p_old = jnp.exp2(s_old - mx)
            den += jnp.sum(p_old, axis=-1, keepdims=True)
            acc += jnp.dot(p_old.astype(BF16), v_old, preferred_element_type=F32)
        outs[r, j] = acc / den
    sub = sub_ref[...]
    for r in range(0, tq, rb):
        o = outs[r, 0] - lam * outs[r, 1]
        y = o * lax.rsqrt(jnp.mean(o * o, axis=-1, keepdims=True) + EPS) * sub
        o_ref[r:r + rb, :] = (y * (1.0 - lam_init)).astype(o_ref.dtype)


def _diff_attention(qkv, cache_k, cache_v, lam_p, subln, *, n_seq, seq_len, heads, lam_init,
                    q_scale, tq, cache_layer=0):
    hd2 = 2 * LANES
    d = heads * hd2
    qb = seq_len // tq
    fused = not isinstance(qkv, (list, tuple))
    k_off, v_off = (heads, 2 * heads) if fused else (0, 0)
    in_specs = [
        pl.BlockSpec((tq, hd2), lambda b, h, i: (b * qb + i, h)),
        pl.BlockSpec((seq_len, hd2), lambda b, h, i: (b, k_off + h)),
        pl.BlockSpec((seq_len, hd2), lambda b, h, i: (b, v_off + h)),
    ]
    args = [qkv, qkv, qkv] if fused else list(qkv)
    if cache_k is not None:
        past = cache_k.shape[2]
        in_specs += [pl.BlockSpec((None, None, past, hd2),
                                  lambda b, h, i: (b, cache_layer, 0, h))] * 2
        args += [cache_k, cache_v]
    in_specs += [pl.BlockSpec(lam_p.shape, lambda b, h, i: (0, 0)),
                 pl.BlockSpec((1, hd2), lambda b, h, i: (0, 0))]
    args += [lam_p, subln.reshape(1, hd2)]
    kern = functools.partial(_attn_kernel, has_cache=cache_k is not None, lam_init=lam_init,
                             q_scale=q_scale)
    return pl.pallas_call(
        kern,
        grid=(n_seq, heads, qb),
        in_specs=in_specs,
        out_specs=pl.BlockSpec((tq, hd2), lambda b, h, i: (b * qb + i, h)),
        out_shape=jax.ShapeDtypeStruct((n_seq * seq_len, d), BF16),
        compiler_params=_params("arbitrary", "arbitrary", "arbitrary"),
        name="diff_attention",
    )(*args)


def _out_proj_kernel(*refs, a_bounds, x_bounds):
    a_refs = refs[:len(a_bounds)]
    x_refs = refs[len(a_bounds):len(a_bounds) + len(x_bounds)]
    w_ref, gate_ref, o_ref = refs[len(a_bounds) + len(x_bounds):]
    i = pl.program_id(0)
    for a_ref, (a_lo, a_cnt) in zip(a_refs, a_bounds):
        for x_ref, (x_lo, x_cnt) in zip(x_refs, x_bounds):
            lo, hi = max(a_lo, x_lo), min(a_lo + a_cnt, x_lo + x_cnt)
            if lo >= hi:
                continue

            @pl.when((i >= lo) & (i < hi))
            def _(a_ref=a_ref, x_ref=x_ref):
                y = jnp.dot(a_ref[...], w_ref[...], preferred_element_type=F32)
                o_ref[...] = x_ref[...] + gate_ref[...] * y


def _out_project(a_segs, w, x_segs, mod, gate_chunk, *, p_rows, s_len, tm=512):
    d = x_segs[0].shape[1]
    t = sum(x.shape[0] for x in x_segs)
    kdim = w.shape[0]
    a_specs, a_bounds = _row_segment_specs(a_segs, tm)
    x_specs, x_bounds = _row_segment_specs(x_segs, tm)
    in_specs = a_specs + x_specs + [
        pl.BlockSpec((kdim, d), lambda i, j: (0, 0)),
        _mod_spec(gate_chunk, d, tm, p_rows, s_len),
    ]
    return pl.pallas_call(
        functools.partial(_out_proj_kernel, a_bounds=a_bounds, x_bounds=x_bounds),
        grid=(t // tm, 1),
        in_specs=in_specs,
        out_specs=pl.BlockSpec((tm, d), lambda i, j: (i, 0)),
        out_shape=jax.ShapeDtypeStruct((t, d), F32),
        compiler_params=_params("arbitrary", "arbitrary"),
        name="out_project_residual",
    )(*a_segs, *x_segs, w, mod)


def _ffn_kernel(x_ref, xp_ref, xn_ref, g_ref, sh_ref, sc_ref, gate_ref, wg_ref, wv_ref,
                conv_a, wd_a, conv_b, wd_b, *rest, tm, nj, p_rows, p_len, s_len, final):
    if final:
        gf_ref, yp_ref, ys_ref, h_ref, act_ref, ug_scr, uv_scr, acc_ref = rest
    else:
        o_ref, h_ref, act_ref, ug_scr, uv_scr, acc_ref = rest
    i = pl.program_id(0)
    j = pl.program_id(1)
    hs = xp_ref.shape[0]
    sub = FFN_TILE
    ext = hs + tm
    row0 = i * tm
    in_prompt = row0 < p_rows
    inner_edges = jnp.where(in_prompt, 1.0, 0.0)
    win = lax.broadcasted_iota(jnp.int32, (2 * SUBLANES, 1), 0)

    def up(cols):
        h = h_ref[...]
        return (jnp.dot(h, wg_ref[:, cols], preferred_element_type=F32),
                jnp.dot(h, wv_ref[:, cols], preferred_element_type=F32))

    def conv(u_all, u_scr, cw, cb):
        u_scr[0:ext, :] = u_all
        u_scr[ext:ext + 1, :] = u_all[hs - 2:hs - 1]
        u = u_all[hs:hs + tm]
        prev = u_scr[hs - 1:hs - 1 + tm, :]
        nxt = u_scr[hs + 1:hs + 1 + tm, :]
        y = prev * cw[0:1] + u * cw[1:2] + nxt * cw[2:3] + cb
        pieces, done = [], 0
        for edge in range(p_len, tm, p_len):
            lo, hi = edge - SUBLANES, edge + SUBLANES
            leak = (jnp.where(win == SUBLANES - 1, nxt[lo:hi] * cw[2:3], 0.0)
                    + jnp.where(win == SUBLANES, prev[lo:hi] * cw[0:1], 0.0))
            pieces += [y[done:lo], y[lo:hi] - inner_edges * leak]
            done = hi
        return jnp.concatenate(pieces + [y[done:]], axis=0) if pieces else y

    def activate(ug, uv, conv_ref):
        p = conv_ref[...]
        gate = conv(ug, ug_scr, p[0:3], p[3:4])
        val = conv(uv, uv_scr, p[4:7], p[7:8])
        return (_silu(gate) * val).astype(BF16)

    def activate_a(ug, uv):
        return activate(ug, uv, conv_a)

    def project_down(act, wd):
        acc_ref[...] += jnp.dot(act, wd[...], preferred_element_type=F32)

    cols_a, cols_b = slice(0, sub), slice(sub, 2 * sub)

    @pl.when(j == 0)
    def _():
        g, sh, sc = g_ref[...], sh_ref[...], sc_ref[...]
        h_ref[hs:hs + tm, :] = _norm_mod(x_ref[...], g, sh, sc).astype(BF16)
        end = row0 + tm
        at_seq_end = jnp.where(in_prompt, lax.rem(end, p_len) == 0,
                               lax.rem(end - p_rows, s_len) == 0)
        at_seq_start = jnp.where(in_prompt, lax.rem(row0, p_len) == 0,
                                 lax.rem(row0 - p_rows, s_len) == 0)
        after = _norm_mod(xn_ref[...], g, sh, sc) * jnp.where(at_seq_end, 0.0, 1.0)
        before = _norm_mod(xp_ref[...], g, sh, sc) * jnp.where(at_seq_start, 0.0, 1.0)
        r = lax.broadcasted_iota(jnp.int32, (hs, 1), 0)
        halo = jnp.where(r == hs - 1, before,
                         jnp.where(r == hs - 2, pltpu.roll(after, hs - 2, 0), 0.0))
        h_ref[0:hs, :] = halo.astype(BF16)
        acc_ref[...] = jnp.zeros_like(acc_ref)
        ua = up(cols_a)
        act_a = activate_a(*ua)
        ub = up(cols_b)
        project_down(act_a, wd_a)
        act_ref[0] = activate(*ub, conv_b)

    for parity in range(2):
        @pl.when((j > 0) & (j < nj) & (j % 2 == parity))
        def _(parity=parity):
            ua = up(cols_a)
            act_a = activate_a(*ua)
            ub = up(cols_b)
            acc_ref[...] += (
                jnp.dot(act_ref[1 - parity], wd_b[...], preferred_element_type=F32)
                + jnp.dot(act_a, wd_a[...], preferred_element_type=F32))
            act_ref[parity] = activate(*ub, conv_b)

    @pl.when(j == nj)
    def _():
        project_down(act_ref[(nj - 1) % 2], wd_b)
        res = x_ref[...] + gate_ref[...] * acc_ref[...]
        if not final:
            o_ref[...] = res
        else:
            y = res * lax.rsqrt(jnp.mean(res * res, axis=-1, keepdims=True) + EPS) * gf_ref[...]

            @pl.when(in_prompt)
            def _():
                yp_ref[...] = y

            @pl.when(jnp.logical_not(in_prompt))
            def _():
                ys_ref[...] = y


def _ffn(x, g, mod, shift_chunk, scale_chunk, gate_chunk, w_up, cw, cb, w_down, *,
         p_rows, p_len, s_len, tm=512, final_gain=None):
    t, d = x.shape
    ff = w_up.shape[1] // 2
    sub = FFN_TILE
    assert ff % (2 * sub) == 0
    tm = math.gcd(tm, p_rows, s_len)
    assert (tm % p_len == 0 or p_len % tm == 0) and p_rows % tm == 0 and s_len % tm == 0
    nj = ff // (2 * sub)
    nt = ff // sub
    hs = 16
    per = tm // hs
    last_blk = t // hs - 1

    def tile_a(i, j):
        return jnp.minimum(2 * j, nt - 2)

    def tile_b(i, j):
        return jnp.minimum(2 * j + 1, nt - 1)

    def parked_b(i, j):
        return jnp.maximum(2 * j - 1, 1)

    def tile_specs(conv_tile, down_tile):
        return [
            pl.BlockSpec((None, 2 * (taps + 1), sub), lambda i, j: (conv_tile(i, j), 0, 0)),
            pl.BlockSpec((sub, d), lambda i, j: (down_tile(i, j), 0)),
        ]

    taps = cw.shape[0]
    conv_p = jnp.concatenate([cw[:, :ff], cb[None, :ff], cw[:, ff:], cb[None, ff:]], axis=0)
    conv_p = conv_p.reshape(2 * (taps + 1), nt, sub).transpose(1, 0, 2)

    in_specs = [
        pl.BlockSpec((tm, d), lambda i, j: (i, 0)),
        pl.BlockSpec((hs, d), lambda i, j: (jnp.maximum(i * per - 1, 0), 0)),
        pl.BlockSpec((hs, d), lambda i, j: (jnp.minimum((i + 1) * per, last_blk), 0)),
        pl.BlockSpec((1, d), lambda i, j: (0, 0)),
        _mod_spec(shift_chunk, d, tm, p_rows, s_len),
        _mod_spec(scale_chunk, d, tm, p_rows, s_len),
        _mod_spec(gate_chunk, d, tm, p_rows, s_len),
        pl.BlockSpec((None, d, 2 * sub), lambda i, j: (jnp.minimum(j, nj - 1), 0, 0)),
        pl.BlockSpec((None, d, 2 * sub), lambda i, j: (jnp.minimum(j, nj - 1) + nj, 0, 0)),
    ] + tile_specs(tile_a, tile_a) + tile_specs(tile_b, parked_b)
    w_up = _column_tiles(w_up, 2 * sub)
    tile_args = [conv_p, w_down.astype(BF16)]
    final = final_gain is not None
    kern = functools.partial(_ffn_kernel, tm=tm, nj=nj, p_rows=p_rows, p_len=p_len,
                             s_len=s_len, final=final)
    args = [x, x, x, g.reshape(1, d), mod, mod, mod, w_up, w_up, *tile_args, *tile_args]
    if final:
        pt = p_rows // tm
        in_specs.append(pl.BlockSpec((1, d), lambda i, j: (0, 0)))
        args.append(final_gain.reshape(1, d))
        out_specs = [pl.BlockSpec((tm, d), lambda i, j: (jnp.minimum(i, pt - 1), 0)),
                     pl.BlockSpec((tm, d), lambda i, j: (jnp.maximum(i - pt, 0), 0))]
        out_shape = [jax.ShapeDtypeStruct((p_rows, d), F32),
                     jax.ShapeDtypeStruct((t - p_rows, d), F32)]
    else:
        out_specs = pl.BlockSpec((tm, d), lambda i, j: (i, 0))
        out_shape = jax.ShapeDtypeStruct((t, d), F32)
    return pl.pallas_call(
        kern,
        grid=(t // tm, nj + 1),
        in_specs=in_specs,
        out_specs=out_specs,
        out_shape=out_shape,
        scratch_shapes=[pltpu.VMEM((tm + hs, d), BF16),
                        pltpu.VMEM((2, tm, sub), BF16),
                        pltpu.VMEM((tm + hs + SUBLANES, sub), F32),
                        pltpu.VMEM((tm + hs + SUBLANES, sub), F32),
                        pltpu.VMEM((tm, d), F32)],
        compiler_params=_params("arbitrary", "arbitrary"),
        name="conv_ffn",
    )(*args)


def _rope_tables(seq_len, hd):
    pos = jnp.arange(seq_len)
    row = (pos // GRID_W).astype(F32)
    col = (pos % GRID_W).astype(F32)
    nf = hd // 4
    inv = ROPE_BASE ** (-jnp.arange(nf, dtype=F32) / nf)
    ar = row[:, None] * inv
    ac = col[:, None] * inv
    cos = jnp.concatenate([jnp.cos(ar), jnp.cos(ar), jnp.cos(ac), jnp.cos(ac)], axis=1)
    sin = jnp.concatenate([-jnp.sin(ar), jnp.sin(ar), -jnp.sin(ac), jnp.sin(ac)], axis=1)
    return cos, sin


def kernel(x_prompt, x_sample, c, cache_hgrn_state, cache_attn_k, cache_attn_v, c_ctx, w_mod, b_mod, norm_mix, norm_ffn, w_hgrn_in, hgrn_lb_logits, hgrn_onorm, w_hgrn_out, w_attn_in, attn_lambda, attn_subln, w_attn_out, w_ffn_up, ffn_conv_w, ffn_conv_b, w_ffn_down, norm_final):
    batch, p_len, d = x_prompt.shape
    dec_batch, s_len, _ = x_sample.shape
    depth = w_mod.shape[0]
    p_rows = batch * p_len
    s_rows = dec_batch * s_len
    hgrn_heads = d // LANES
    diff_heads = d // (2 * LANES)
    hd = LANES
    assert 1 + dec_batch <= MOD_ROWS and p_rows % s_len == 0

    x = [x_prompt.reshape(p_rows, d), x_sample.reshape(s_rows, d)]
    cvec = jnp.concatenate(
        [c_ctx[None, :], c, jnp.zeros((MOD_ROWS - 1 - dec_batch, d), F32)], axis=0)
    mods = _modulation(cvec, w_mod, b_mod)
    seg = dict(p_rows=p_rows, s_len=s_len)

    hgrn_states, attn_ks, attn_vs = [], [], []
    for l in range(depth):
        mod = mods[l][:, None, :]
        j = l // N_MIXERS
        if l % N_MIXERS == 0:
            proj = _project(x, norm_mix[l], mod, 0, 1, w_hgrn_in[j], row_off=0,
                            rows=p_rows + s_rows, out_dtype=F32, tm=1024, tn=1024, **seg)
            mix_p, st = _hgrn_scan(proj, hgrn_lb_logits, hgrn_onorm[j], None, layer_j=j,
                                   n_seq=batch, seq_len=p_len, row_off=0, heads=hgrn_heads,
                                   emit_state=True)
            hgrn_states.append(st)
            mix_s, = _hgrn_scan(proj, hgrn_lb_logits, hgrn_onorm[j], cache_hgrn_state,
                                layer_j=j, n_seq=dec_batch, seq_len=s_len, row_off=p_rows,
                                heads=hgrn_heads, emit_state=False)
            w_out = w_hgrn_out[j]
        else:
            lam_init = 0.8 - 0.6 * math.exp(-0.3 * l)
            q_scale = hd ** -0.5 * LOG2_E
            w_in = w_attn_in[j]
            qkv_p = _project(x, norm_mix[l], mod, 0, 1, w_in, row_off=0, rows=p_rows,
                             out_dtype=F32, tn=1024, sections=[d, d, d], **seg)
            attn_ks.append(qkv_p[1].reshape(batch, p_len, 2 * diff_heads, hd))
            attn_vs.append(qkv_p[2].reshape(batch, p_len, diff_heads, 2 * hd))
            mix_p = _diff_attention(qkv_p, None, None, attn_lambda[j], attn_subln[j],
                                    n_seq=batch, seq_len=p_len, heads=diff_heads,
                                    lam_init=lam_init, q_scale=q_scale, tq=p_len)
            qkv_s = _project(x, norm_mix[l], mod, 0, 1, w_in, row_off=p_rows, rows=s_rows,
                             out_dtype=BF16, rope_tables=_rope_tables(s_len, hd),
                             q_scale=q_scale, tm=1024, tn=1024, **seg)
            n_attn, past = cache_attn_k.shape[1:3]
            mix_s = _diff_attention(qkv_s, cache_attn_k.reshape(dec_batch, n_attn, past, d),
                                    cache_attn_v.reshape(dec_batch, n_attn, past, d),
                                    attn_lambda[j], attn_subln[j], n_seq=dec_batch,
                                    seq_len=s_len, heads=diff_heads, lam_init=lam_init,
                                    q_scale=1.0, tq=512, cache_layer=j)
            w_out = w_attn_out[j]
        x = _out_project([mix_p, mix_s], w_out.astype(BF16), x, mod, 2, **seg)
        last = l == depth - 1
        x = _ffn(x, norm_ffn[l], mod, 3, 4, 5, w_ffn_up[l], ffn_conv_w[l], ffn_conv_b[l],
                 w_ffn_down[l], p_len=p_len, final_gain=norm_final if last else None, **seg)
        x = list(x) if last else [x]

    y_prompt = x[0].reshape(batch, p_len, d)
    y_sample = x[1].reshape(dec_batch, s_len, d)
    new_hgrn_state = jnp.stack(hgrn_states, axis=1)
    new_attn_k = jnp.stack(attn_ks, axis=1)
    new_attn_v = jnp.stack(attn_vs, axis=1)
    return (y_prompt, y_sample, new_hgrn_state, new_attn_k, new_attn_v)
```
